```python
import math
import jax, jax.numpy as jnp
from jax import lax
import numpy as np

D_MODEL = 1024
BATCH = 8
SEQ = 8192
DEPTH = 4

HEAD_DIM = 64
BRANCH_W = D_MODEL // 4
N_BRANCH = 4
N_HEADS = BRANCH_W // HEAD_DIM
SC_K = 3
SB_BLOCK = 128
SSM_GROUPS = 2
SSM_STATE = 64
SSM_CONV_K = 4
SSM_CHUNK = 256
SSM_CONV_DIM = BRANCH_W + 2 * SSM_GROUPS * SSM_STATE
MOBA_BLOCK = 256
MOBA_TOPK = 3
MOBA_QCHUNK = 128
SEQ_MULTIPLE = 256
FFN_HIDDEN = ((8 * D_MODEL + 3 * 256 - 1) // (3 * 256)) * 256
RMS_EPS = 1e-6

W_A_IN = 3 * BRANCH_W
W_B_IN = 3 * BRANCH_W
W_C_IN = BRANCH_W + SSM_CONV_DIM + N_HEADS
W_D_IN = 3 * BRANCH_W
W_G_IN = N_BRANCH * D_MODEL
IN_SPLITS = [W_A_IN, W_A_IN + W_B_IN, W_A_IN + W_B_IN + W_C_IN,
             W_A_IN + W_B_IN + W_C_IN + W_D_IN]
N_IN = W_A_IN + W_B_IN + W_C_IN + W_D_IN + W_G_IN

kernel_name = "hybrid_gated_conv_sb_ssd_moba_trunk"


def rmsnorm(x, g):
    xf = x.astype(jnp.float32)
    y = xf * lax.rsqrt(jnp.mean(xf * xf, axis=-1, keepdims=True) + RMS_EPS)
    return (y * g.astype(jnp.float32)).astype(x.dtype)


def causal_depthwise_conv(x, w):
    k = w.shape[0]
    return lax.conv_general_dilated(
        x, w[:, None, :].astype(x.dtype), window_strides=(1,), padding=[(k - 1, 0)],
        dimension_numbers=("NWC", "WIO", "NWC"), feature_group_count=x.shape[-1])


def short_conv_mixer(u, conv_w):
    xa, b_gate, c_gate = jnp.split(u, 3, axis=-1)
    return b_gate * causal_depthwise_conv(c_gate * xa, conv_w)


def stick_breaking_attention(q, k, v):
    bsz, seq, nh, dh = q.shape
    nqb = seq // SB_BLOCK
    f32 = jnp.float32
    qs = q * (dh ** -0.5)
    within_mat = jnp.tril(jnp.ones((SB_BLOCK, SB_BLOCK), f32), -1)
    outs = []
    for c in range(nqb):
        nkb = c + 1
        kc = nkb * SB_BLOCK
        qi = qs[:, c * SB_BLOCK:(c + 1) * SB_BLOCK]
        z = jnp.einsum("bqhd,bkhd->bhqk", qi, k[:, :kc], preferred_element_type=f32)
        q_pos = c * SB_BLOCK + jnp.arange(SB_BLOCK)
        past = jnp.arange(kc)[None, :] < q_pos[:, None]
        z = jnp.where(past, z, -jnp.inf)
        sp = jax.nn.softplus(z)
        lk = sp.reshape(bsz, nh, SB_BLOCK, nkb, SB_BLOCK)
        within = jnp.einsum("bhqnj,js->bhqns", lk, within_mat)
        after_mat = jnp.tril(jnp.ones((nkb, nkb), f32), -1)
        after = jnp.einsum("bhqm,mn->bhqn", jnp.sum(lk, axis=-1), after_mat)
        between = (within + after[..., None]).reshape(bsz, nh, SB_BLOCK, kc)
        w = jnp.exp(z - sp - between)
        outs.append(jnp.einsum("bhqk,bkhd->bqhd", w.astype(v.dtype), v[:, :kc]))
    return jnp.concatenate(outs, axis=1).reshape(bsz, seq, nh * dh)


def ssd_chunked(x, a, b_in, c_in):
    bsz, seq, nh, hp = x.shape
    ng, ns = b_in.shape[2], b_in.shape[3]
    ne = nh // ng
    t = SSM_CHUNK
    nc = seq // t
    f32 = jnp.float32
    x = x.astype(f32).reshape(bsz, nc, t, ng, ne, hp)
    a = a.astype(f32).reshape(bsz, nc, t, ng, ne)
    bc = b_in.astype(f32).reshape(bsz, nc, t, ng, ns)
    cc = c_in.astype(f32).reshape(bsz, nc, t, ng, ns)
    a_cs = jnp.cumsum(a, axis=2)
    causal = jnp.tril(jnp.ones((t, t), dtype=bool))[:, :, None, None]
    seg = a_cs[:, :, :, None] - a_cs[:, :, None, :]
    decay = jnp.exp(jnp.where(causal, seg, -jnp.inf))
    cb = jnp.einsum("bclgn,bcsgn->bclsg", cc, bc)
    y_diag = jnp.einsum("bclsge,bcsgep->bclgep", cb[..., None] * decay, x)
    decay_to_end = jnp.exp(a_cs[:, :, -1:] - a_cs)
    chunk_states = jnp.einsum("bclgn,bclge,bclgep->bcgepn", bc, decay_to_end, x)
    chunk_decay = jnp.exp(a_cs[:, :, -1])

    def step(h, inp):
        s_c, d_c = inp
        return h * d_c[..., None, None] + s_c, h

    h0 = jnp.zeros((bsz, ng, ne, hp, ns), f32)
    _, h_enter = lax.scan(step, h0, (jnp.moveaxis(chunk_states, 1, 0),
                                     jnp.moveaxis(chunk_decay, 1, 0)))
    h_enter = jnp.moveaxis(h_enter, 0, 1)
    y_off = jnp.einsum("bclgn,bcgepn,bclge->bclgep", cc, h_enter, jnp.exp(a_cs))
    return (y_diag + y_off).reshape(bsz, seq, nh, hp)


def mamba2_mixer(u, conv_w, conv_b, dt_bias, a_log, d_skip, norm_g):
    bsz, seq, _ = u.shape
    z, xbc, dt = jnp.split(u, [BRANCH_W, BRANCH_W + SSM_CONV_DIM], axis=-1)
    xbc = jax.nn.silu(causal_depthwise_conv(xbc, conv_w) + conv_b)
    xs, b_in, c_in = jnp.split(xbc, [BRANCH_W, BRANCH_W + SSM_GROUPS * SSM_STATE], axis=-1)
    dt = jax.nn.softplus(dt.astype(jnp.float32) + dt_bias.astype(jnp.float32))
    a = -jnp.exp(a_log.astype(jnp.float32))
    xh = xs.reshape(bsz, seq, N_HEADS, HEAD_DIM)
    y = ssd_chunked(xh.astype(jnp.float32) * dt[..., None], dt * a,
                    b_in.reshape(bsz, seq, SSM_GROUPS, SSM_STATE),
                    c_in.reshape(bsz, seq, SSM_GROUPS, SSM_STATE))
    y = y.astype(u.dtype) + xh * d_skip[:, None]
    gsz = BRANCH_W // SSM_GROUPS
    gated = y.reshape(bsz, seq, SSM_GROUPS, gsz) * jax.nn.silu(z.reshape(bsz, seq, SSM_GROUPS, gsz))
    return rmsnorm(gated, norm_g.reshape(SSM_GROUPS, gsz)).reshape(bsz, seq, BRANCH_W)


def moba_head(q, k, v):
    seq, dh = q.shape
    nb = seq // MOBA_BLOCK
    ke = max(1, min(MOBA_TOPK, nb - 1))
    cq = MOBA_QCHUNK
    n_pairs = seq * ke
    n_chunks = -(-(n_pairs + (nb + 1) * (cq - 1)) // cq)
    scale = dh ** -0.5
    f32 = jnp.float32
    kblk = k.reshape(nb, MOBA_BLOCK, dh)
    vblk = v.reshape(nb, MOBA_BLOCK, dh)
    k_mean = jnp.mean(kblk.astype(f32), axis=1)
    gate = q.astype(f32) @ k_mean.T
    own = jnp.arange(seq) // MOBA_BLOCK
    gate = jnp.where(jnp.arange(nb)[None, :] < own[:, None], gate, -jnp.inf)
    _, sel = lax.top_k(gate, ke)
    valid = jnp.arange(ke)[None, :] < own[:, None]
    grp = jnp.where(valid, sel, nb).reshape(n_pairs)
    cnt = jnp.zeros((nb + 1,), jnp.int32).at[grp].add(1)
    padded = (cnt + cq - 1) // cq * cq
    pad_end = jnp.cumsum(padded)
    pad_start = pad_end - padded
    order = jnp.argsort(grp, stable=True)
    g_sorted = grp[order]
    rank = jnp.arange(n_pairs, dtype=jnp.int32) - (jnp.cumsum(cnt) - cnt)[g_sorted]
    dest = jnp.zeros((n_pairs,), jnp.int32).at[order].set(pad_start[g_sorted] + rank)
    slot_pair = jnp.full((n_chunks * cq,), -1, jnp.int32).at[dest].set(
        jnp.arange(n_pairs, dtype=jnp.int32)).reshape(n_chunks, cq)
    chunk_grp = jnp.searchsorted(pad_end, jnp.arange(n_chunks, dtype=jnp.int32) * cq, side="right")
    slot_ok = (slot_pair >= 0) & (chunk_grp < nb)[:, None]
    blk = jnp.minimum(chunk_grp, nb - 1)
    q_c = q[jnp.maximum(slot_pair, 0) // ke]
    k_c = kblk[blk]
    v_c = vblk[blk]
    s = jnp.einsum("cqd,ckd->cqk", q_c, k_c, preferred_element_type=f32) * scale
    s = jnp.where(slot_ok[..., None], s, -jnp.inf)
    m = jnp.where(slot_ok, jnp.max(s, axis=-1), 0.0)
    p = jnp.exp(s - m[..., None])
    l = jnp.sum(p, axis=-1)
    o = jnp.einsum("cqk,ckd->cqd", p.astype(v.dtype), v_c, preferred_element_type=f32)
    m_p = m.reshape(-1)[dest].reshape(seq, ke)
    l_p = l.reshape(-1)[dest].reshape(seq, ke)
    o_p = o.reshape(-1, dh)[dest].reshape(seq, ke, dh)
    s_own = jnp.einsum("nqd,nkd->nqk", q.reshape(nb, MOBA_BLOCK, dh), kblk,
                       preferred_element_type=f32) * scale
    s_own = jnp.where(jnp.tril(jnp.ones((MOBA_BLOCK, MOBA_BLOCK), dtype=bool)), s_own, -jnp.inf)
    m_own = jnp.max(s_own, axis=-1)
    p_own = jnp.exp(s_own - m_own[..., None])
    l_own = jnp.sum(p_own, axis=-1).reshape(seq)
    o_own = jnp.einsum("nqk,nkd->nqd", p_own.astype(v.dtype), vblk,
                       preferred_element_type=f32).reshape(seq, dh)
    m_own = m_own.reshape(seq)
    m_all = jnp.maximum(m_own, jnp.max(jnp.where(valid, m_p, -jnp.inf), axis=-1))
    w_p = jnp.where(valid, jnp.exp(m_p - m_all[:, None]), 0.0)
    w_own = jnp.exp(m_own - m_all)
    num = w_own[:, None] * o_own + jnp.einsum("sr,srd->sd", w_p, o_p)
    den = w_own * l_own + jnp.sum(w_p * l_p, axis=-1)
    return (num / den[:, None]).astype(q.dtype)


def moba_attention(q, k, v):
    bsz, seq, nh, dh = q.shape
    qt, kt, vt = [jnp.moveaxis(t, 2, 1) for t in (q, k, v)]
    out = lax.map(lambda a: jax.vmap(moba_head)(*a), (qt, kt, vt))
    return jnp.moveaxis(out, 1, 2).reshape(bsz, seq, nh * dh)


def hybrid_layer(x, norm1_g, w_in, conv_a_w, ssm_conv_w, ssm_conv_b, ssm_dt_bias,
                 ssm_a_log, ssm_d, ssm_norm_g, w_branch, w_o, norm2_g, w_gate_up, w_down):
    bsz, seq, _ = x.shape
    h = rmsnorm(x, norm1_g)
    u = h @ w_in
    u_a, u_b, u_c, u_d, u_g = jnp.split(u, IN_SPLITS, axis=-1)
    y_a = short_conv_mixer(u_a, conv_a_w)
    q_b, k_b, v_b = [t.reshape(bsz, seq, N_HEADS, HEAD_DIM) for t in jnp.split(u_b, 3, axis=-1)]
    y_b = stick_breaking_attention(q_b, k_b, v_b)
    y_c = mamba2_mixer(u_c, ssm_conv_w, ssm_conv_b, ssm_dt_bias, ssm_a_log, ssm_d, ssm_norm_g)
    q_d, k_d, v_d = [t.reshape(bsz, seq, N_HEADS, HEAD_DIM) for t in jnp.split(u_d, 3, axis=-1)]
    y_d = moba_attention(q_d, k_d, v_d)
    gates = jax.nn.sigmoid(u_g.reshape(bsz, seq, N_BRANCH, D_MODEL))
    branches = [y_a, y_b, y_c, y_d]
    merged = gates[:, :, 0] * (branches[0] @ w_branch[0])
    for i in range(1, N_BRANCH):
        merged = merged + gates[:, :, i] * (branches[i] @ w_branch[i])
    x = x + merged @ w_o
    h2 = rmsnorm(x, norm2_g)
    g_ff, up_ff = jnp.split(h2 @ w_gate_up, 2, axis=-1)
    return x + (jax.nn.silu(g_ff) * up_ff) @ w_down


def setup_inputs(seed: int = 0) -> dict:
    key = jax.random.key(seed)
    ks = jax.random.split(key, 17)
    f32 = jnp.float32

    def nrm(k, shape, scale):
        return jax.random.normal(k, shape, f32) * scale

    dt0 = jnp.exp(jax.random.uniform(ks[6], (DEPTH, N_HEADS), f32, math.log(1e-3), math.log(1e-1)))
    return {
        "x": nrm(ks[0], (BATCH, SEQ, D_MODEL), 1.0),
        "norm1_g": 1.0 + nrm(ks[1], (DEPTH, D_MODEL), 0.02),
        "w_in": nrm(ks[2], (DEPTH, D_MODEL, N_IN), D_MODEL ** -0.5),
        "conv_a_w": nrm(ks[3], (DEPTH, SC_K, BRANCH_W), SC_K ** -0.5),
        "ssm_conv_w": nrm(ks[4], (DEPTH, SSM_CONV_K, SSM_CONV_DIM), SSM_CONV_K ** -0.5),
        "ssm_conv_b": nrm(ks[5], (DEPTH, SSM_CONV_DIM), 0.02),
        "ssm_dt_bias": dt0 + jnp.log(-jnp.expm1(-dt0)),
        "ssm_a_log": jnp.log(jax.random.uniform(ks[7], (DEPTH, N_HEADS), f32, 1.0, 16.0)),
        "ssm_d": 1.0 + nrm(ks[8], (DEPTH, N_HEADS), 0.1),
        "ssm_norm_g": 1.0 + nrm(ks[9], (DEPTH, BRANCH_W), 0.02),
        "w_branch": nrm(ks[10], (DEPTH, N_BRANCH, BRANCH_W, D_MODEL), BRANCH_W ** -0.5),
        "w_o": nrm(ks[11], (DEPTH, D_MODEL, D_MODEL), D_MODEL ** -0.5),
        "norm2_g": 1.0 + nrm(ks[12], (DEPTH, D_MODEL), 0.02),
        "w_gate_up": nrm(ks[13], (DEPTH, D_MODEL, 2 * FFN_HIDDEN), D_MODEL ** -0.5),
        "w_down": nrm(ks[14], (DEPTH, FFN_HIDDEN, D_MODEL), FFN_HIDDEN ** -0.5),
        "final_g": 1.0 + nrm(ks[15], (D_MODEL,), 0.02),
    }


def reference(x, norm1_g, w_in, conv_a_w, ssm_conv_w, ssm_conv_b, ssm_dt_bias, ssm_a_log,
              ssm_d, ssm_norm_g, w_branch, w_o, norm2_g, w_gate_up, w_down, final_g):
    seq = x.shape[1]
    seq_pad = -(-seq // SEQ_MULTIPLE) * SEQ_MULTIPLE
    h = jnp.pad(x, ((0, 0), (0, seq_pad - seq), (0, 0)))
    for l in range(DEPTH):
        h = hybrid_layer(h, norm1_g[l], w_in[l], conv_a_w[l], ssm_conv_w[l], ssm_conv_b[l],
                         ssm_dt_bias[l], ssm_a_log[l], ssm_d[l], ssm_norm_g[l], w_branch[l],
                         w_o[l], norm2_g[l], w_gate_up[l], w_down[l])
    return rmsnorm(h, final_g)[:, :seq]
```

```python
import functools

import jax
import jax.numpy as jnp
from jax import lax
from jax.experimental import pallas as pl
from jax.experimental.pallas import tpu as pltpu

F32 = jnp.float32
BF16 = jnp.bfloat16

D_MODEL = 1024
HEAD_DIM = 64
BRANCH_W = 256
N_BRANCH = 4
N_HEADS = 4
SC_K = 3
SSM_GROUPS = 2
SSM_STATE = 64
SSM_CONV_K = 4
SSM_CHUNK = 256
SSM_CONV_DIM = BRANCH_W + 2 * SSM_GROUPS * SSM_STATE
MOBA_BLOCK = 256
MOBA_TOPK = 3
FFN_HIDDEN = 2816
RMS_EPS = 1e-6

LANES = 128
HIST = 8
NEG = -1e30

N_PACK = 7168
COL_A = 0
COL_SB_Q, COL_SB_K, COL_SB_V = 6, 8, 10
COL_Z = 12
COL_MO_Q = 14
COL_XBC = 16
COL_MO_K, COL_MO_V = 20, 22
COL_G = 24

VMEM_LIMIT = 56 * 1024 * 1024


def _cparams(sem):
    return pltpu.CompilerParams(dimension_semantics=sem, vmem_limit_bytes=VMEM_LIMIT)


def _sigmoid(x):
    return 1.0 / (1.0 + jnp.exp(-x))


def _inproj_kernel(x_ref, g_ref, w_ref, wdt_ref, u_ref, dt_ref, h_ref):
    @pl.when(pl.program_id(1) == 0)
    def _():
        x = x_ref[...]
        ms = jnp.mean(x * x, axis=-1, keepdims=True)
        h = (x * lax.rsqrt(ms + RMS_EPS) * g_ref[...]).astype(BF16)
        h_ref[...] = h
        dt_ref[...] = jnp.dot(h, wdt_ref[...], preferred_element_type=F32)

    u_ref[...] = jnp.dot(h_ref[...], w_ref[...], preferred_element_type=F32).astype(BF16)


def _inproj(x, g, w, wdt, tm, tn):
    m = x.shape[0]
    return pl.pallas_call(
        _inproj_kernel,
        grid=(m // tm, N_PACK // tn),
        in_specs=[
            pl.BlockSpec((tm, D_MODEL), lambda i, j: (i, 0)),
            pl.BlockSpec((1, D_MODEL), lambda i, j: (0, 0)),
            pl.BlockSpec((D_MODEL, tn), lambda i, j: (0, j)),
            pl.BlockSpec((D_MODEL, LANES), lambda i, j: (0, 0)),
        ],
        out_specs=[
            pl.BlockSpec((tm, tn), lambda i, j: (i, j)),
            pl.BlockSpec((tm, LANES), lambda i, j: (i, 0)),
        ],
        out_shape=[jax.ShapeDtypeStruct((m, N_PACK), BF16),
                   jax.ShapeDtypeStruct((m, LANES), F32)],
        scratch_shapes=[pltpu.VMEM((tm, D_MODEL), BF16)],
        compiler_params=_cparams(("parallel", "arbitrary")),
        name="inproj",
    )(x, g, w, wdt)


def _sb_kernel(q_ref, k_ref, v_ref, o_ref, acc_ref, r_ref, *, t):
    qi = pl.program_id(2)
    lane = lax.broadcasted_iota(jnp.int32, (1, LANES), 1)
    head0 = lane < HEAD_DIM
    q = q_ref[...] * (HEAD_DIM ** -0.5)
    qh = (jnp.where(head0, q, 0), jnp.where(head0, 0, q))
    row = lax.broadcasted_iota(jnp.int32, (t, t), 0)
    col = lax.broadcasted_iota(jnp.int32, (t, t), 1)
    later = jnp.where(row > col, 1.0, 0.0).astype(BF16)

    acc_ref[...] = jnp.zeros_like(acc_ref)
    r_ref[...] = jnp.zeros_like(r_ref)

    def tile(kstart, diag):
        k = k_ref[pl.ds(kstart, t), :]
        v = v_ref[pl.ds(kstart, t), :]
        vh = (jnp.where(head0, v, 0), jnp.where(head0, 0, v))
        for h in range(2):
            z = lax.dot_general(qh[h], k, (((1,), (1,)), ((), ())), preferred_element_type=F32)
            if diag:
                z = jnp.where(col < row, z, NEG)
            sp = jnp.maximum(z, 0.0) + jnp.log(1.0 + jnp.exp(-jnp.abs(z)))
            between = jnp.dot(sp.astype(BF16), later, preferred_element_type=F32)
            r = r_ref[h]
            w = jnp.exp(z - sp - between - r)
            acc_ref[...] += jnp.dot(w.astype(BF16), vh[h], preferred_element_type=F32)
            r_ref[h] = r + jnp.sum(sp, axis=-1, keepdims=True)

    tile(pl.multiple_of(qi * t, t), True)

    def body(j, carry):
        tile(pl.multiple_of((qi - 1 - j) * t, t), False)
        return carry

    lax.fori_loop(0, qi, body, 0)
    o_ref[...] = acc_ref[...].astype(o_ref.dtype)


def _sb_attention(u, bsz, seq, t):
    nq = seq // t
    return pl.pallas_call(
        functools.partial(_sb_kernel, t=t),
        grid=(bsz, N_HEADS // 2, nq),
        in_specs=[
            pl.BlockSpec((t, LANES), lambda b, hp, qi: (b * nq + qi, COL_SB_Q + hp)),
            pl.BlockSpec((seq, LANES), lambda b, hp, qi: (b, COL_SB_K + hp)),
            pl.BlockSpec((seq, LANES), lambda b, hp, qi: (b, COL_SB_V + hp)),
        ],
        out_specs=pl.BlockSpec((t, LANES), lambda b, hp, qi: (b * nq + qi, hp)),
        out_shape=jax.ShapeDtypeStruct((bsz * seq, BRANCH_W), BF16),
        scratch_shapes=[pltpu.VMEM((t, LANES), F32), pltpu.VMEM((2, t, 1), F32)],
        compiler_params=_cparams(("parallel", "parallel", "arbitrary")),
        name="sb_attention",
    )(u, u, u)


def _moba_kernel(q_ref, k_ref, v_ref, o_ref, km_ref, sel_ref, acc_ref, m_ref, l_ref, *, nb, ke):
    t = MOBA_BLOCK
    qi = pl.program_id(2)
    lane = lax.broadcasted_iota(jnp.int32, (1, LANES), 1)
    head0 = lane < HEAD_DIM

    @pl.when(qi == 0)
    def _():
        kf = k_ref[...].astype(F32).reshape(nb, t, LANES)
        km_ref[...] = jnp.zeros_like(km_ref)
        km_ref[0:nb, :] = jnp.sum(kf, axis=1) * (1.0 / t)

    q = q_ref[...]
    qs = q * (HEAD_DIM ** -0.5)
    qh = (jnp.where(head0, q, 0), jnp.where(head0, 0, q))
    qsh = (jnp.where(head0, qs, 0), jnp.where(head0, 0, qs))

    km = km_ref[...]
    km_hi = km.astype(BF16)
    km_lo = (km - km_hi.astype(F32)).astype(BF16)
    colf = lax.broadcasted_iota(jnp.int32, (t, LANES), 1).astype(F32)
    qif = qi.astype(F32)
    nt = (((1,), (1,)), ((), ()))
    for h in range(2):
        gate = (lax.dot_general(qh[h], km_hi, nt, preferred_element_type=F32)
                + lax.dot_general(qh[h], km_lo, nt, preferred_element_type=F32))
        g = jnp.where(colf < qif, gate, -jnp.inf)
        sel = jnp.zeros((t, LANES), F32)
        for r in range(ke):
            mx = jnp.max(g, axis=-1, keepdims=True)
            idx = jnp.min(jnp.where(g == mx, colf, 1e9), axis=-1, keepdims=True)
            hit = colf == idx
            sel = jnp.where(jnp.logical_and(hit, qif > r), 1.0, sel)
            g = jnp.where(hit, -jnp.inf, g)
        sel_ref[h] = sel

    row = lax.broadcasted_iota(jnp.int32, (t, t), 0)
    col = lax.broadcasted_iota(jnp.int32, (t, t), 1)

    kstart = pl.multiple_of(qi * t, t)
    k = k_ref[pl.ds(kstart, t), :]
    v = v_ref[pl.ds(kstart, t), :]
    acc = jnp.zeros((t, LANES), F32)
    for h in range(2):
        s = lax.dot_general(qsh[h], k, nt, preferred_element_type=F32)
        s = jnp.where(col <= row, s, -jnp.inf)
        m = jnp.max(s, axis=-1, keepdims=True)
        p = jnp.exp(s - m)
        m_ref[h] = m
        l_ref[h] = jnp.sum(p, axis=-1, keepdims=True)
        vh = jnp.where(head0, v, 0) if h == 0 else jnp.where(head0, 0, v)
        acc = acc + jnp.dot(p.astype(BF16), vh, preferred_element_type=F32)
    acc_ref[...] = acc

    def body(n, carry):
        kstart = pl.multiple_of(n * t, t)
        k = k_ref[pl.ds(kstart, t), :]
        v = v_ref[pl.ds(kstart, t), :]
        nf = n.astype(F32)
        pv = jnp.zeros((t, LANES), F32)
        alphas = []
        for h in range(2):
            chosen = jnp.sum(jnp.where(colf == nf, sel_ref[h], 0.0), axis=-1, keepdims=True)
            s = lax.dot_general(qsh[h], k, nt, preferred_element_type=F32)
            s = s + jnp.where(chosen > 0.5, 0.0, -jnp.inf)
            m_old = m_ref[h]
            m_new = jnp.maximum(m_old, jnp.max(s, axis=-1, keepdims=True))
            alpha = jnp.exp(m_old - m_new)
            p = jnp.exp(s - m_new)
            m_ref[h] = m_new
            l_ref[h] = alpha * l_ref[h] + jnp.sum(p, axis=-1, keepdims=True)
            vh = jnp.where(head0, v, 0) if h == 0 else jnp.where(head0, 0, v)
            pv = pv + jnp.dot(p.astype(BF16), vh, preferred_element_type=F32)
            alphas.append(alpha)
        acc_ref[...] = acc_ref[...] * jnp.where(head0, alphas[0], alphas[1]) + pv
        return carry

    lax.fori_loop(0, qi, body, 0)
    o_ref[...] = (acc_ref[...] / jnp.where(head0, l_ref[0], l_ref[1])).astype(o_ref.dtype)


def _moba_attention(u, bsz, seq):
    t = MOBA_BLOCK
    nb = seq // t
    assert nb <= LANES
    ke = max(1, min(MOBA_TOPK, nb - 1))
    return pl.pallas_call(
        functools.partial(_moba_kernel, nb=nb, ke=ke),
        grid=(bsz, N_HEADS // 2, nb),
        in_specs=[
            pl.BlockSpec((t, LANES), lambda b, hp, qi: (b * nb + qi, COL_MO_Q + hp)),
            pl.BlockSpec((seq, LANES), lambda b, hp, qi: (b, COL_MO_K + hp)),
            pl.BlockSpec((seq, LANES), lambda b, hp, qi: (b, COL_MO_V + hp)),
        ],
        out_specs=pl.BlockSpec((t, LANES), lambda b, hp, qi: (b * nb + qi, hp)),
        out_shape=jax.ShapeDtypeStruct((bsz * seq, BRANCH_W), BF16),
        scratch_shapes=[
            pltpu.VMEM((LANES, LANES), F32),
            pltpu.VMEM((2, t, LANES), F32),
            pltpu.VMEM((t, LANES), F32),
            pltpu.VMEM((2, t, 1), F32),
            pltpu.VMEM((2, t, 1), F32),
        ],
        compiler_params=_cparams(("parallel", "parallel", "arbitrary")),
        name="moba_attention",
    )(u, u, u)


def _per_head(v, lanes_per_head, width):
    head = lax.broadcasted_iota(jnp.int32, (1, width), 1) // lanes_per_head
    out = jnp.zeros((v.shape[0], width), F32)
    for h in range(N_HEADS):
        out = jnp.where(head == h, v[:, h:h + 1], out)
    return out


def _split3(a):
    hi = a.astype(BF16)
    r1 = a - hi.astype(F32)
    mid = r1.astype(BF16)
    lo = (r1 - mid.astype(F32)).astype(BF16)
    return hi, mid, lo


def _ssd_kernel(ua_ref, z_ref, xbc_ref, dt_ref, cwa_ref, cwc_ref, cbias_ref, dtb_ref, alog_ref,
                dskip_ref, ng_ref, ya_ref, yc_ref, bufa, bufc, hst):
    t = SSM_CHUNK
    w = BRANCH_W

    @pl.when(pl.program_id(1) == 0)
    def _():
        bufa[0:HIST, :] = jnp.zeros((HIST, w), F32)
        bufc[0:HIST, :] = jnp.zeros((HIST, SSM_CONV_DIM), F32)
        hst[...] = jnp.zeros_like(hst)

    ua = ua_ref[...].astype(F32)
    bufa[HIST:, :] = ua[:, 2 * w:] * ua[:, :w]
    conv = cwa_ref[0:1, :] * bufa[pl.ds(HIST - SC_K + 1, t), :]
    for kk in range(1, SC_K):
        conv = conv + cwa_ref[kk:kk + 1, :] * bufa[pl.ds(HIST - SC_K + 1 + kk, t), :]
    ya_ref[...] = (ua[:, w:2 * w] * conv).astype(ya_ref.dtype)
    bufa[0:HIST, :] = bufa[t:t + HIST, :]

    bufc[HIST:, :] = xbc_ref[...].astype(F32)
    xc = cbias_ref[...] + cwc_ref[0:1, :] * bufc[pl.ds(HIST - SSM_CONV_K + 1, t), :]
    for kk in range(1, SSM_CONV_K):
        xc = xc + cwc_ref[kk:kk + 1, :] * bufc[pl.ds(HIST - SSM_CONV_K + 1 + kk, t), :]
    bufc[0:HIST, :] = bufc[t:t + HIST, :]
    xc = xc * _sigmoid(xc)
    xs = xc[:, :w]
    b_in = xc[:, w:w + LANES]
    c_in = xc[:, w + LANES:]

    dtp = dt_ref[...] + dtb_ref[...]
    dt = jnp.maximum(dtp, 0.0) + jnp.log(1.0 + jnp.exp(-jnp.abs(dtp)))
    a = dt * (-jnp.exp(alog_ref[...]))
    row = lax.broadcasted_iota(jnp.int32, (t, t), 0)
    col = lax.broadcasted_iota(jnp.int32, (t, t), 1)
    causal = row >= col
    tri = jnp.where(causal, 1.0, 0.0).astype(BF16)
    a_hi, a_mid, a_lo = _split3(a)
    acs = (jnp.dot(tri, a_hi, preferred_element_type=F32)
           + jnp.dot(tri, a_mid, preferred_element_type=F32)
           + jnp.dot(tri, a_lo, preferred_element_type=F32))
    acs_t = acs.T
    acs_x = _per_head(acs, HEAD_DIM, w)
    last_x = acs_x[t - 1:t, :]
    x_dt = xs * _per_head(dt, HEAD_DIM, w)
    to_end_x = jnp.exp(last_x - acs_x)
    from_start_x = jnp.exp(acs_x)
    chunk_decay_x = jnp.exp(last_x)

    lane = lax.broadcasted_iota(jnp.int32, (1, LANES), 1)
    low = lane < HEAD_DIM
    c_bf = c_in.astype(BF16)
    nt = (((1,), (1,)), ((), ()))
    for g in range(SSM_GROUPS):
        gmask = low if g == 0 else jnp.logical_not(low)
        sl = slice(g * LANES, (g + 1) * LANES)
        b_g = jnp.where(gmask, b_in, 0.0)
        cb = lax.dot_general(c_bf, b_g.astype(BF16), nt, preferred_element_type=F32)
        xg = x_dt[:, sl]
        y = jnp.zeros((t, LANES), F32)
        for e in range(2):
            h = 2 * g + e
            seg = acs[:, h:h + 1] - acs_t[h:h + 1, :]
            decay = jnp.exp(jnp.where(causal, seg, -jnp.inf))
            emask = low if e == 0 else jnp.logical_not(low)
            xe = jnp.where(emask, xg, 0.0).astype(BF16)
            y = y + jnp.dot((cb * decay).astype(BF16), xe, preferred_element_type=F32)
        h_enter = hst[g]
        y = y + jnp.dot(c_bf, h_enter.astype(BF16), preferred_element_type=F32) * from_start_x[:, sl]
        state = jnp.dot(b_g.T.astype(BF16), (xg * to_end_x[:, sl]).astype(BF16),
                        preferred_element_type=F32)
        hst[g] = h_enter * chunk_decay_x[:, sl] + state

        y = y + xs[:, sl] * dskip_ref[:, sl]
        zg = z_ref[:, sl].astype(F32)
        gated = y * (zg * _sigmoid(zg))
        ms = jnp.mean(gated * gated, axis=-1, keepdims=True)
        yc_ref[:, sl] = (gated * lax.rsqrt(ms + RMS_EPS) * ng_ref[:, sl]).astype(yc_ref.dtype)


def _ssd(u, dt, cwa, cwc, cbias, dtb, alog, dskip, ng, bsz, seq):
    t = SSM_CHUNK
    nc = seq // t
    w = BRANCH_W
    small = lambda shape: pl.BlockSpec(shape, lambda b, c: (0, 0))
    return pl.pallas_call(
        _ssd_kernel,
        grid=(bsz, nc),
        in_specs=[
            pl.BlockSpec((t, 3 * w), lambda b, c: (b * nc + c, COL_A)),
            pl.BlockSpec((t, w), lambda b, c: (b * nc + c, COL_Z // 2)),
            pl.BlockSpec((t, SSM_CONV_DIM), lambda b, c: (b * nc + c, COL_XBC * LANES // SSM_CONV_DIM)),
            pl.BlockSpec((t, LANES), lambda b, c: (b * nc + c, 0)),
            small((SC_K, w)), small((SSM_CONV_K, SSM_CONV_DIM)), small((1, SSM_CONV_DIM)),
            small((1, LANES)), small((1, LANES)), small((1, w)), small((1, w)),
        ],
        out_specs=[pl.BlockSpec((t, w), lambda b, c: (b * nc + c, 0)),
                   pl.BlockSpec((t, w), lambda b, c: (b * nc + c, 0))],
        out_shape=[jax.ShapeDtypeStruct((bsz * seq, w), BF16),
                   jax.ShapeDtypeStruct((bsz * seq, w), BF16)],
        scratch_shapes=[
            pltpu.VMEM((t + HIST, w), F32),
            pltpu.VMEM((t + HIST, SSM_CONV_DIM), F32),
            pltpu.VMEM((SSM_GROUPS, LANES, LANES), F32),
        ],
        compiler_params=_cparams(("parallel", "arbitrary")),
        name="conv_ssd",
    )(u, u, u, dt, cwa, cwc, cbias, dtb, alog, dskip, ng)


def _merge_kernel(x_ref, ga_ref, gb_ref, gc_ref, gd_ref, ya_ref, yb_ref, yc_ref, yd_ref,
                  wb_ref, wo_ref, o_ref):
    merged = None
    branches = ((ga_ref, ya_ref), (gb_ref, yb_ref), (gc_ref, yc_ref), (gd_ref, yd_ref))
    for i, (g_ref, y_ref) in enumerate(branches):
        gate = _sigmoid(g_ref[...].astype(F32))
        term = gate * jnp.dot(y_ref[...], wb_ref[i], preferred_element_type=F32)
        merged = term if merged is None else merged + term
    o_ref[...] = x_ref[...] + jnp.dot(merged.astype(BF16), wo_ref[...], preferred_element_type=F32)


def _merge(x, u, ya, yb, yc, yd, wb, wo, tm):
    m = x.shape[0]
    ybr = pl.BlockSpec((tm, BRANCH_W), lambda i: (i, 0))
    g0 = COL_G * LANES // D_MODEL
    gates = [pl.BlockSpec((tm, D_MODEL), functools.partial(lambda i, c: (i, c), c=g0 + br))
             for br in range(N_BRANCH)]
    return pl.pallas_call(
        _merge_kernel,
        grid=(m // tm,),
        in_specs=[
            pl.BlockSpec((tm, D_MODEL), lambda i: (i, 0)),
            *gates,
            ybr, ybr, ybr, ybr,
            pl.BlockSpec((N_BRANCH, BRANCH_W, D_MODEL), lambda i: (0, 0, 0)),
            pl.BlockSpec((D_MODEL, D_MODEL), lambda i: (0, 0)),
        ],
        out_specs=pl.BlockSpec((tm, D_MODEL), lambda i: (i, 0)),
        out_shape=jax.ShapeDtypeStruct((m, D_MODEL), F32),
        compiler_params=_cparams(("parallel",)),
        name="merge",
    )(x, u, u, u, u, ya, yb, yc, yd, wb, wo)


def _ffn_kernel(x_ref, g_ref, wg_ref, wu_ref, wd_ref, o_ref, h_ref, acc_ref):
    j = pl.program_id(1)

    @pl.when(j == 0)
    def _():
        x = x_ref[...]
        ms = jnp.mean(x * x, axis=-1, keepdims=True)
        h_ref[...] = (x * lax.rsqrt(ms + RMS_EPS) * g_ref[...]).astype(BF16)
        acc_ref[...] = jnp.zeros_like(acc_ref)

    h = h_ref[...]
    gate = jnp.dot(h, wg_ref[...], preferred_element_type=F32)
    up = jnp.dot(h, wu_ref[...], preferred_element_type=F32)
    act = (gate * _sigmoid(gate)) * up
    acc_ref[...] += jnp.dot(act.astype(BF16), wd_ref[...], preferred_element_type=F32)

    @pl.when(j == pl.num_programs(1) - 1)
    def _():
        o_ref[...] = x_ref[...] + acc_ref[...]


def _ffn(x, g, wgu, wd, tm, th):
    m = x.shape[0]
    nh = FFN_HIDDEN // th
    return pl.pallas_call(
        _ffn_kernel,
        grid=(m // tm, nh),
        in_specs=[
            pl.BlockSpec((tm, D_MODEL), lambda i, j: (i, 0)),
            pl.BlockSpec((1, D_MODEL), lambda i, j: (0, 0)),
            pl.BlockSpec((D_MODEL, th), lambda i, j: (0, j)),
            pl.BlockSpec((D_MODEL, th), lambda i, j: (0, nh + j)),
            pl.BlockSpec((th, D_MODEL), lambda i, j: (j, 0)),
        ],
        out_specs=pl.BlockSpec((tm, D_MODEL), lambda i, j: (i, 0)),
        out_shape=jax.ShapeDtypeStruct((m, D_MODEL), F32),
        scratch_shapes=[pltpu.VMEM((tm, D_MODEL), BF16), pltpu.VMEM((tm, D_MODEL), F32)],
        compiler_params=_cparams(("parallel", "arbitrary")),
        name="ffn",
    )(x, g, wgu, wgu, wd)


def _norm_kernel(x_ref, g_ref, o_ref):
    x = x_ref[...]
    ms = jnp.mean(x * x, axis=-1, keepdims=True)
    o_ref[...] = x * lax.rsqrt(ms + RMS_EPS) * g_ref[...]


def _final_norm(x, g, tm):
    m = x.shape[0]
    return pl.pallas_call(
        _norm_kernel,
        grid=(m // tm,),
        in_specs=[pl.BlockSpec((tm, D_MODEL), lambda i: (i, 0)),
                  pl.BlockSpec((1, D_MODEL), lambda i: (0, 0))],
        out_specs=pl.BlockSpec((tm, D_MODEL), lambda i: (i, 0)),
        out_shape=jax.ShapeDtypeStruct((m, D_MODEL), F32),
        compiler_params=_cparams(("parallel",)),
        name="final_norm",
    )(x, g)


def _row_tile(m, want):
    while m % want:
        want //= 2
    return want


def _pad_lanes(v):
    return jnp.pad(v.astype(F32), (0, LANES - v.shape[0]))[None, :]


def kernel(x, norm1_g, w_in, conv_a_w, ssm_conv_w, ssm_conv_b, ssm_dt_bias, ssm_a_log, ssm_d,
           ssm_norm_g, w_branch, w_o, norm2_g, w_gate_up, w_down, final_g):
    bsz, seq, _ = x.shape
    assert seq % MOBA_BLOCK == 0 and seq % SSM_CHUNK == 0
    depth = w_in.shape[0]
    m = bsz * seq
    xbc_col = 3 * BRANCH_W + 3 * BRANCH_W + BRANCH_W
    dt_col = xbc_col + SSM_CONV_DIM
    d_col = dt_col + N_HEADS
    h = x.reshape(m, D_MODEL)
    tm_big = _row_tile(m, 1024)
    tm_mid = _row_tile(m, 512)
    for l in range(depth):
        w_main = jnp.concatenate([w_in[l, :, :xbc_col], w_in[l, :, d_col:d_col + BRANCH_W],
                                  w_in[l, :, xbc_col:dt_col], w_in[l, :, d_col + BRANCH_W:]],
                                 axis=1).astype(BF16)
        w_dt = jnp.pad(w_in[l, :, dt_col:dt_col + N_HEADS], ((0, 0), (0, LANES - N_HEADS))).astype(BF16)
        u, dt = _inproj(h, norm1_g[l][None, :], w_main, w_dt, tm_big, 1024)
        y_b = _sb_attention(u, bsz, seq, 256)
        y_d = _moba_attention(u, bsz, seq)
        y_a, y_c = _ssd(u, dt, conv_a_w[l], ssm_conv_w[l], ssm_conv_b[l][None, :],
                        _pad_lanes(ssm_dt_bias[l]), _pad_lanes(ssm_a_log[l]),
                        jnp.repeat(ssm_d[l], HEAD_DIM)[None, :], ssm_norm_g[l][None, :], bsz, seq)
        h = _merge(h, u, y_a, y_b, y_c, y_d, w_branch[l].astype(BF16), w_o[l].astype(BF16), tm_mid)
        h = _ffn(h, norm2_g[l][None, :], w_gate_up[l].astype(BF16), w_down[l].astype(BF16), tm_big, 256)
    out = _final_norm(h, final_g[None, :], tm_big)
    return out.reshape(bsz, seq, D_MODEL)
```

```python
import functools

import jax
import jax.numpy as jnp
from jax import lax
from jax.experimental import pallas as pl
from jax.experimental.pallas import tpu as pltpu

F32 = jnp.float32
BF16 = jnp.bfloat16

D_MODEL = 1024
HEAD_DIM = 64
BRANCH_W = 256
N_BRANCH = 4
N_HEADS = 4
SC_K = 3
SSM_GROUPS = 2
SSM_STATE = 64
SSM_CONV_K = 4
SSM_CHUNK = 256
SSM_CONV_DIM = BRANCH_W + 2 * SSM_GROUPS * SSM_STATE
MOBA_BLOCK = 256
MOBA_TOPK = 3
FFN_HIDDEN = 2816
RMS_EPS = 1e-6

LANES = 128
HIST = 8
NEG = -1e30

N_PACK = 7168
COL_A = 0
COL_SB_Q, COL_SB_K, COL_SB_V = 6, 8, 10
COL_Z = 12
COL_MO_Q = 14
COL_XBC = 16
COL_MO_K, COL_MO_V = 20, 22
COL_G = 24

VMEM_LIMIT = 56 * 1024 * 1024


def _cparams(sem):
    return pltpu.CompilerParams(dimension_semantics=sem, vmem_limit_bytes=VMEM_LIMIT)


def _sigmoid(x):
    return 1.0 / (1.0 + jnp.exp(-x))


def _inproj_kernel(x_ref, g_ref, w_ref, wdt_ref, u_ref, dt_ref, h_ref):
    @pl.when(pl.program_id(1) == 0)
    def _():
        x = x_ref[...]
        ms = jnp.mean(x * x, axis=-1, keepdims=True)
        h = (x * lax.rsqrt(ms + RMS_EPS) * g_ref[...]).astype(BF16)
        h_ref[...] = h
        dt_ref[...] = jnp.dot(h, wdt_ref[...], preferred_element_type=F32)

    u_ref[...] = jnp.dot(h_ref[...], w_ref[...], preferred_element_type=F32).astype(BF16)


def _inproj(x, g, w, wdt, tm, tn):
    m = x.shape[0]
    return pl.pallas_call(
        _inproj_kernel,
        grid=(m // tm, N_PACK // tn),
        in_specs=[
            pl.BlockSpec((tm, D_MODEL), lambda i, j: (i, 0)),
            pl.BlockSpec((1, D_MODEL), lambda i, j: (0, 0)),
            pl.BlockSpec((D_MODEL, tn), lambda i, j: (0, j)),
            pl.BlockSpec((D_MODEL, LANES), lambda i, j: (0, 0)),
        ],
        out_specs=[
            pl.BlockSpec((tm, tn), lambda i, j: (i, j)),
            pl.BlockSpec((tm, LANES), lambda i, j: (i, 0)),
        ],
        out_shape=[jax.ShapeDtypeStruct((m, N_PACK), BF16),
                   jax.ShapeDtypeStruct((m, LANES), F32)],
        scratch_shapes=[pltpu.VMEM((tm, D_MODEL), BF16)],
        compiler_params=_cparams(("parallel", "arbitrary")),
        name="inproj",
    )(x, g, w, wdt)


def _neg_abs(x):
    return pltpu.bitcast(pltpu.bitcast(x, jnp.uint32) | jnp.uint32(0x80000000), F32)


def _sb_kernel(q_ref, k_ref, v_ref, o_ref, vt_ref, z_ref, sp_ref, d_ref, w_ref, r_ref, acc_ref,
               *, t, nt):
    qi = pl.program_id(2)

    @pl.when(qi == 0)
    def _():
        head0_rows = lax.broadcasted_iota(jnp.int32, (LANES, t), 0) < HEAD_DIM

        def tr(c, carry):
            vt = v_ref[pl.ds(pl.multiple_of(c * t, t), t), :].astype(F32).T
            vt_ref[0, c] = jnp.where(head0_rows, vt, 0.0).astype(BF16)
            vt_ref[1, c] = jnp.where(head0_rows, 0.0, vt).astype(BF16)
            return carry

        lax.fori_loop(0, nt, tr, 0)

    lane = lax.broadcasted_iota(jnp.int32, (1, LANES), 1)
    head0 = lane < HEAD_DIM
    q = q_ref[...] * (HEAD_DIM ** -0.5)
    qh = (jnp.where(head0, q, 0), jnp.where(head0, 0, q))
    row = lax.broadcasted_iota(jnp.int32, (t, t), 0)
    col = lax.broadcasted_iota(jnp.int32, (t, t), 1)
    later_t = jnp.where(col > row, 1.0, 0.0).astype(BF16)
    nt_dims = (((1,), (1,)), ((), ()))
    n_tiles = qi + 1

    def tile_of(p):
        return jnp.maximum(qi - p, 0)

    def score_mm(p, par):
        k = k_ref[pl.ds(pl.multiple_of(tile_of(p) * t, t), t), :]
        for h in range(2):
            z_ref[par, h] = lax.dot_general(k, qh[h], nt_dims, preferred_element_type=F32)

    def softplus(par, masked):
        for h in range(2):
            z = z_ref[par, h]
            if masked:
                z = jnp.where(row < col, z, NEG)
            sp = jnp.maximum(z, 0.0) + jnp.log(1.0 + jnp.exp(_neg_abs(z)))
            sp_ref[par, h] = sp.astype(BF16)
            d_ref[par, h] = z - sp

    def weights(p, par):
        pad = jnp.where(p < n_tiles, 0.0, NEG)
        for h in range(2):
            sp = sp_ref[par, h]
            between = jnp.dot(later_t, sp, preferred_element_type=F32)
            r = r_ref[h]
            w_ref[par, h] = jnp.exp(d_ref[par, h] - between - (r - pad)).astype(BF16)
            r_ref[h] = r + between[0:1, :] + sp[0:1, :].astype(F32)

    def value_mm(p, par):
        tile = tile_of(p)
        acc_ref[...] += (jnp.dot(vt_ref[0, tile], w_ref[par, 0], preferred_element_type=F32)
                         + jnp.dot(vt_ref[1, tile], w_ref[par, 1], preferred_element_type=F32))

    def step(s, par):
        value_mm(s - 3, 1 - par)
        weights(s - 2, par)
        softplus(1 - par, False)
        score_mm(s, par)

    acc_ref[...] = jnp.zeros_like(acc_ref)
    r_ref[...] = jnp.zeros_like(r_ref)
    score_mm(0, 0)
    softplus(0, True)
    score_mm(1, 1)
    weights(0, 0)
    softplus(1, False)
    score_mm(2, 0)

    def body(i, carry):
        s = 3 + 2 * i
        step(s, 1)
        step(s + 1, 0)
        return carry

    lax.fori_loop(0, (n_tiles + 1) // 2, body, 0)
    o_ref[...] = acc_ref[...].T.astype(o_ref.dtype)


def _sb_attention(u, bsz, seq, t):
    nq = seq // t
    tile_f32 = pltpu.VMEM((2, 2, t, t), F32)
    tile_bf16 = pltpu.VMEM((2, 2, t, t), BF16)
    return pl.pallas_call(
        functools.partial(_sb_kernel, t=t, nt=nq),
        grid=(bsz, N_HEADS // 2, nq),
        in_specs=[
            pl.BlockSpec((t, LANES), lambda b, hp, qi: (b * nq + qi, COL_SB_Q + hp)),
            pl.BlockSpec((seq, LANES), lambda b, hp, qi: (b, COL_SB_K + hp)),
            pl.BlockSpec((seq, LANES), lambda b, hp, qi: (b, COL_SB_V + hp)),
        ],
        out_specs=pl.BlockSpec((t, LANES), lambda b, hp, qi: (b * nq + qi, hp)),
        out_shape=jax.ShapeDtypeStruct((bsz * seq, BRANCH_W), BF16),
        scratch_shapes=[
            pltpu.VMEM((2, nq, LANES, t), BF16),
            tile_f32,
            tile_bf16,
            tile_f32,
            tile_bf16,
            pltpu.VMEM((2, 1, t), F32),
            pltpu.VMEM((LANES, t), F32),
        ],
        compiler_params=_cparams(("parallel", "parallel", "arbitrary")),
        name="sb_attention",
    )(u, u, u)


def _moba_kernel(q_ref, k_ref, v_ref, o_ref, vt_ref, km_ref, sel_ref, z_ref, p_ref, alpha_ref, m_ref,
                 acc_ref, *, nb, ke):
    t = MOBA_BLOCK
    qi = pl.program_id(2)
    n_tiles = qi + 1
    ones_row = (HEAD_DIM, 0)

    @pl.when(qi == 0)
    def _():
        rows = lax.broadcasted_iota(jnp.int32, (LANES, t), 0)

        def tr(c, carry):
            kv = pl.ds(pl.multiple_of(c * t, t), t)
            vt = v_ref[kv, :].astype(F32).T
            vt_ref[0, c] = jnp.where(rows < HEAD_DIM, vt,
                                     jnp.where(rows == ones_row[0], 1.0, 0.0)).astype(BF16)
            vt_ref[1, c] = jnp.where(rows >= HEAD_DIM, vt,
                                     jnp.where(rows == ones_row[1], 1.0, 0.0)).astype(BF16)
            km_ref[pl.ds(c, 1), :] = jnp.sum(k_ref[kv, :].astype(F32), axis=0, keepdims=True) * (1.0 / t)
            return carry

        km_ref[...] = jnp.zeros_like(km_ref)
        lax.fori_loop(0, nb, tr, 0)

    lane = lax.broadcasted_iota(jnp.int32, (1, LANES), 1)
    head0 = lane < HEAD_DIM
    q = q_ref[...]
    qs = q * (HEAD_DIM ** -0.5)
    qh = (jnp.where(head0, q, 0), jnp.where(head0, 0, q))
    qsh = (jnp.where(head0, qs, 0), jnp.where(head0, 0, qs))
    nt_dims = (((1,), (1,)), ((), ()))

    km = km_ref[...]
    km_hi = km.astype(BF16)
    km_lo = (km - km_hi.astype(F32)).astype(BF16)
    blk = lax.broadcasted_iota(jnp.int32, (LANES, t), 0).astype(F32)
    qif = qi.astype(F32)
    for h in range(2):
        gate = (lax.dot_general(km_hi, qh[h], nt_dims, preferred_element_type=F32)
                + lax.dot_general(km_lo, qh[h], nt_dims, preferred_element_type=F32))
        g = jnp.where(blk < qif, gate, -jnp.inf)
        sel = jnp.zeros((LANES, t), F32)
        for r in range(ke):
            mx = jnp.max(g, axis=0, keepdims=True)
            idx = jnp.min(jnp.where(g == mx, blk, 1e9), axis=0, keepdims=True)
            hit = blk == idx
            sel = jnp.where(jnp.logical_and(hit, qif > r), 1.0, sel)
            g = jnp.where(hit, -jnp.inf, g)
        sel_ref[h] = sel

    row = lax.broadcasted_iota(jnp.int32, (t, t), 0)
    col = lax.broadcasted_iota(jnp.int32, (t, t), 1)

    def tile_of(p):
        return jnp.maximum(qi - p, 0)

    def score_mm(p, par):
        k = k_ref[pl.ds(pl.multiple_of(tile_of(p) * t, t), t), :]
        for h in range(2):
            z_ref[par, h] = lax.dot_general(k, qsh[h], nt_dims, preferred_element_type=F32)

    def softmax(p, par, own):
        for h in range(2):
            s = z_ref[par, h]
            if own:
                s = jnp.where(row <= col, s, -jnp.inf)
                m_old = jnp.full((1, t), -jnp.inf, F32)
            else:
                chosen = sel_ref[h, pl.ds(tile_of(p), 1), :] * jnp.where(p < n_tiles, 1.0, 0.0)
                s = s + jnp.where(chosen > 0.5, 0.0, -jnp.inf)
                m_old = m_ref[h]
            m_new = jnp.maximum(m_old, jnp.max(s, axis=0, keepdims=True))
            alpha_ref[par, h] = jnp.zeros((1, t), F32) if own else jnp.exp(m_old - m_new)
            p_ref[par, h] = jnp.exp(s - m_new).astype(BF16)
            m_ref[h] = m_new

    def value_mm(p, par):
        tile = tile_of(p)
        for h in range(2):
            acc_ref[h] = (acc_ref[h] * alpha_ref[par, h]
                          + jnp.dot(vt_ref[h, tile], p_ref[par, h], preferred_element_type=F32))

    def step(s, par):
        value_mm(s - 2, par)
        softmax(s - 1, 1 - par, False)
        score_mm(s, par)

    acc_ref[...] = jnp.zeros_like(acc_ref)
    score_mm(0, 0)
    softmax(0, 0, True)
    score_mm(1, 1)

    def body(i, carry):
        s = 2 + 2 * i
        step(s, 0)
        step(s + 1, 1)
        return carry

    lax.fori_loop(0, (n_tiles + 1) // 2, body, 0)
    rows = lax.broadcasted_iota(jnp.int32, (LANES, t), 0)
    a0 = acc_ref[0]
    a1 = acc_ref[1]
    out_t = jnp.where(rows < HEAD_DIM, a0 / a0[ones_row[0]:ones_row[0] + 1, :],
                      a1 / a1[ones_row[1]:ones_row[1] + 1, :])
    o_ref[...] = out_t.T.astype(o_ref.dtype)


def _moba_attention(u, bsz, seq):
    t = MOBA_BLOCK
    nb = seq // t
    assert nb <= LANES
    ke = max(1, min(MOBA_TOPK, nb - 1))
    return pl.pallas_call(
        functools.partial(_moba_kernel, nb=nb, ke=ke),
        grid=(bsz, N_HEADS // 2, nb),
        in_specs=[
            pl.BlockSpec((t, LANES), lambda b, hp, qi: (b * nb + qi, COL_MO_Q + hp)),
            pl.BlockSpec((seq, LANES), lambda b, hp, qi: (b, COL_MO_K + hp)),
            pl.BlockSpec((seq, LANES), lambda b, hp, qi: (b, COL_MO_V + hp)),
        ],
        out_specs=pl.BlockSpec((t, LANES), lambda b, hp, qi: (b * nb + qi, hp)),
        out_shape=jax.ShapeDtypeStruct((bsz * seq, BRANCH_W), BF16),
        scratch_shapes=[
            pltpu.VMEM((2, nb, LANES, t), BF16),
            pltpu.VMEM((LANES, LANES), F32),
            pltpu.VMEM((2, LANES, t), F32),
            pltpu.VMEM((2, 2, t, t), F32),
            pltpu.VMEM((2, 2, t, t), BF16),
            pltpu.VMEM((2, 2, 1, t), F32),
            pltpu.VMEM((2, 1, t), F32),
            pltpu.VMEM((2, LANES, t), F32),
        ],
        compiler_params=_cparams(("parallel", "parallel", "arbitrary")),
        name="moba_attention",
    )(u, u, u)


def _per_head(v, lanes_per_head, width):
    head = lax.broadcasted_iota(jnp.int32, (1, width), 1) // lanes_per_head
    out = jnp.zeros((v.shape[0], width), F32)
    for h in range(N_HEADS):
        out = jnp.where(head == h, v[:, h:h + 1], out)
    return out


def _split3(a):
    hi = a.astype(BF16)
    r1 = a - hi.astype(F32)
    mid = r1.astype(BF16)
    lo = (r1 - mid.astype(F32)).astype(BF16)
    return hi, mid, lo


def _ssd_kernel(ua_ref, z_ref, xbc_ref, dt_ref, cwa_ref, cwc_ref, cbias_ref, dtb_ref, alog_ref,
                dskip_ref, ng_ref, ya_ref, yc_ref, bufa, bufc, hst):
    t = SSM_CHUNK
    w = BRANCH_W

    @pl.when(pl.program_id(1) == 0)
    def _():
        bufa[0:HIST, :] = jnp.zeros((HIST, w), F32)
        bufc[0:HIST, :] = jnp.zeros((HIST, SSM_CONV_DIM), F32)
        hst[...] = jnp.zeros_like(hst)

    ua = ua_ref[...].astype(F32)
    bufa[HIST:, :] = ua[:, 2 * w:] * ua[:, :w]
    conv = cwa_ref[0:1, :] * bufa[pl.ds(HIST - SC_K + 1, t), :]
    for kk in range(1, SC_K):
        conv = conv + cwa_ref[kk:kk + 1, :] * bufa[pl.ds(HIST - SC_K + 1 + kk, t), :]
    ya_ref[...] = (ua[:, w:2 * w] * conv).astype(ya_ref.dtype)
    bufa[0:HIST, :] = bufa[t:t + HIST, :]

    bufc[HIST:, :] = xbc_ref[...].astype(F32)
    xc = cbias_ref[...] + cwc_ref[0:1, :] * bufc[pl.ds(HIST - SSM_CONV_K + 1, t), :]
    for kk in range(1, SSM_CONV_K):
        xc = xc + cwc_ref[kk:kk + 1, :] * bufc[pl.ds(HIST - SSM_CONV_K + 1 + kk, t), :]
    bufc[0:HIST, :] = bufc[t:t + HIST, :]
    xc = xc * _sigmoid(xc)
    xs = xc[:, :w]
    b_in = xc[:, w:w + LANES]
    c_in = xc[:, w + LANES:]

    dtp = dt_ref[...] + dtb_ref[...]
    dt = jnp.maximum(dtp, 0.0) + jnp.log(1.0 + jnp.exp(-jnp.abs(dtp)))
    a = dt * (-jnp.exp(alog_ref[...]))
    row = lax.broadcasted_iota(jnp.int32, (t, t), 0)
    col = lax.broadcasted_iota(jnp.int32, (t, t), 1)
    causal = row >= col
    tri = jnp.where(causal, 1.0, 0.0).astype(BF16)
    a_hi, a_mid, a_lo = _split3(a)
    acs = (jnp.dot(tri, a_hi, preferred_element_type=F32)
           + jnp.dot(tri, a_mid, preferred_element_type=F32)
           + jnp.dot(tri, a_lo, preferred_element_type=F32))
    acs_t = acs.T
    acs_x = _per_head(acs, HEAD_DIM, w)
    last_x = acs_x[t - 1:t, :]
    x_dt = xs * _per_head(dt, HEAD_DIM, w)
    to_end_x = jnp.exp(last_x - acs_x)
    from_start_x = jnp.exp(acs_x)
    chunk_decay_x = jnp.exp(last_x)

    lane = lax.broadcasted_iota(jnp.int32, (1, LANES), 1)
    low = lane < HEAD_DIM
    c_bf = c_in.astype(BF16)
    nt = (((1,), (1,)), ((), ()))
    for g in range(SSM_GROUPS):
        gmask = low if g == 0 else jnp.logical_not(low)
        sl = slice(g * LANES, (g + 1) * LANES)
        b_g = jnp.where(gmask, b_in, 0.0)
        cb = lax.dot_general(c_bf, b_g.astype(BF16), nt, preferred_element_type=F32)
        xg = x_dt[:, sl]
        y = jnp.zeros((t, LANES), F32)
        for e in range(2):
            h = 2 * g + e
            seg = acs[:, h:h + 1] - acs_t[h:h + 1, :]
            decay = jnp.exp(jnp.where(causal, seg, -jnp.inf))
            emask = low if e == 0 else jnp.logical_not(low)
            xe = jnp.where(emask, xg, 0.0).astype(BF16)
            y = y + jnp.dot((cb * decay).astype(BF16), xe, preferred_element_type=F32)
        h_enter = hst[g]
        y = y + jnp.dot(c_bf, h_enter.astype(BF16), preferred_element_type=F32) * from_start_x[:, sl]
        state = jnp.dot(b_g.T.astype(BF16), (xg * to_end_x[:, sl]).astype(BF16),
                        preferred_element_type=F32)
        hst[g] = h_enter * chunk_decay_x[:, sl] + state

        y = y + xs[:, sl] * dskip_ref[:, sl]
        zg = z_ref[:, sl].astype(F32)
        gated = y * (zg * _sigmoid(zg))
        ms = jnp.mean(gated * gated, axis=-1, keepdims=True)
        yc_ref[:, sl] = (gated * lax.rsqrt(ms + RMS_EPS) * ng_ref[:, sl]).astype(yc_ref.dtype)


def _ssd(u, dt, cwa, cwc, cbias, dtb, alog, dskip, ng, bsz, seq):
    t = SSM_CHUNK
    nc = seq // t
    w = BRANCH_W
    small = lambda shape: pl.BlockSpec(shape, lambda b, c: (0, 0))
    return pl.pallas_call(
        _ssd_kernel,
        grid=(bsz, nc),
        in_specs=[
            pl.BlockSpec((t, 3 * w), lambda b, c: (b * nc + c, COL_A)),
            pl.BlockSpec((t, w), lambda b, c: (b * nc + c, COL_Z // 2)),
            pl.BlockSpec((t, SSM_CONV_DIM), lambda b, c: (b * nc + c, COL_XBC * LANES // SSM_CONV_DIM)),
            pl.BlockSpec((t, LANES), lambda b, c: (b * nc + c, 0)),
            small((SC_K, w)), small((SSM_CONV_K, SSM_CONV_DIM)), small((1, SSM_CONV_DIM)),
            small((1, LANES)), small((1, LANES)), small((1, w)), small((1, w)),
        ],
        out_specs=[pl.BlockSpec((t, w), lambda b, c: (b * nc + c, 0)),
                   pl.BlockSpec((t, w), lambda b, c: (b * nc + c, 0))],
        out_shape=[jax.ShapeDtypeStruct((bsz * seq, w), BF16),
                   jax.ShapeDtypeStruct((bsz * seq, w), BF16)],
        scratch_shapes=[
            pltpu.VMEM((t + HIST, w), F32),
            pltpu.VMEM((t + HIST, SSM_CONV_DIM), F32),
            pltpu.VMEM((SSM_GROUPS, LANES, LANES), F32),
        ],
        compiler_params=_cparams(("parallel", "arbitrary")),
        name="conv_ssd",
    )(u, u, u, dt, cwa, cwc, cbias, dtb, alog, dskip, ng)


def _merge_kernel(x_ref, ga_ref, gb_ref, gc_ref, gd_ref, ya_ref, yb_ref, yc_ref, yd_ref,
                  wb_ref, wo_ref, o_ref):
    merged = None
    branches = ((ga_ref, ya_ref), (gb_ref, yb_ref), (gc_ref, yc_ref), (gd_ref, yd_ref))
    for i, (g_ref, y_ref) in enumerate(branches):
        gate = _sigmoid(g_ref[...].astype(F32))
        term = gate * jnp.dot(y_ref[...], wb_ref[i], preferred_element_type=F32)
        merged = term if merged is None else merged + term
    o_ref[...] = x_ref[...] + jnp.dot(merged.astype(BF16), wo_ref[...], preferred_element_type=F32)


def _merge(x, u, ya, yb, yc, yd, wb, wo, tm):
    m = x.shape[0]
    ybr = pl.BlockSpec((tm, BRANCH_W), lambda i: (i, 0))
    g0 = COL_G * LANES // D_MODEL
    gates = [pl.BlockSpec((tm, D_MODEL), functools.partial(lambda i, c: (i, c), c=g0 + br))
             for br in range(N_BRANCH)]
    return pl.pallas_call(
        _merge_kernel,
        grid=(m // tm,),
        in_specs=[
            pl.BlockSpec((tm, D_MODEL), lambda i: (i, 0)),
            *gates,
            ybr, ybr, ybr, ybr,
            pl.BlockSpec((N_BRANCH, BRANCH_W, D_MODEL), lambda i: (0, 0, 0)),
            pl.BlockSpec((D_MODEL, D_MODEL), lambda i: (0, 0)),
        ],
        out_specs=pl.BlockSpec((tm, D_MODEL), lambda i: (i, 0)),
        out_shape=jax.ShapeDtypeStruct((m, D_MODEL), F32),
        compiler_params=_cparams(("parallel",)),
        name="merge",
    )(x, u, u, u, u, ya, yb, yc, yd, wb, wo)


def _ffn_kernel(x_ref, g_ref, wg_ref, wu_ref, wd_ref, o_ref, h_ref, acc_ref):
    j = pl.program_id(1)

    @pl.when(j == 0)
    def _():
        x = x_ref[...]
        ms = jnp.mean(x * x, axis=-1, keepdims=True)
        h_ref[...] = (x * lax.rsqrt(ms + RMS_EPS) * g_ref[...]).astype(BF16)
        acc_ref[...] = jnp.zeros_like(acc_ref)

    h = h_ref[...]
    gate = jnp.dot(h, wg_ref[...], preferred_element_type=F32)
    up = jnp.dot(h, wu_ref[...], preferred_element_type=F32)
    act = (gate * _sigmoid(gate)) * up
    acc_ref[...] += jnp.dot(act.astype(BF16), wd_ref[...], preferred_element_type=F32)

    @pl.when(j == pl.num_programs(1) - 1)
    def _():
        o_ref[...] = x_ref[...] + acc_ref[...]


def _ffn(x, g, wgu, wd, tm, th):
    m = x.shape[0]
    nh = FFN_HIDDEN // th
    return pl.pallas_call(
        _ffn_kernel,
        grid=(m // tm, nh),
        in_specs=[
            pl.BlockSpec((tm, D_MODEL), lambda i, j: (i, 0)),
            pl.BlockSpec((1, D_MODEL), lambda i, j: (0, 0)),
            pl.BlockSpec((D_MODEL, th), lambda i, j: (0, j)),
            pl.BlockSpec((D_MODEL, th), lambda i, j: (0, nh + j)),
            pl.BlockSpec((th, D_MODEL), lambda i, j: (j, 0)),
        ],
        out_specs=pl.BlockSpec((tm, D_MODEL), lambda i, j: (i, 0)),
        out_shape=jax.ShapeDtypeStruct((m, D_MODEL), F32),
        scratch_shapes=[pltpu.VMEM((tm, D_MODEL), BF16), pltpu.VMEM((tm, D_MODEL), F32)],
        compiler_params=_cparams(("parallel", "arbitrary")),
        name="ffn",
    )(x, g, wgu, wgu, wd)


def _norm_kernel(x_ref, g_ref, o_ref):
    x = x_ref[...]
    ms = jnp.mean(x * x, axis=-1, keepdims=True)
    o_ref[...] = x * lax.rsqrt(ms + RMS_EPS) * g_ref[...]


def _final_norm(x, g, tm):
    m = x.shape[0]
    return pl.pallas_call(
        _norm_kernel,
        grid=(m // tm,),
        in_specs=[pl.BlockSpec((tm, D_MODEL), lambda i: (i, 0)),
                  pl.BlockSpec((1, D_MODEL), lambda i: (0, 0))],
        out_specs=pl.BlockSpec((tm, D_MODEL), lambda i: (i, 0)),
        out_shape=jax.ShapeDtypeStruct((m, D_MODEL), F32),
        compiler_params=_cparams(("parallel",)),
        name="final_norm",
    )(x, g)


def _row_tile(m, want):
    while m % want:
        want //= 2
    return want


def _pad_lanes(v):
    return jnp.pad(v.astype(F32), (0, LANES - v.shape[0]))[None, :]


def kernel(x, norm1_g, w_in, conv_a_w, ssm_conv_w, ssm_conv_b, ssm_dt_bias, ssm_a_log, ssm_d,
           ssm_norm_g, w_branch, w_o, norm2_g, w_gate_up, w_down, final_g):
    bsz, seq, _ = x.shape
    assert seq % MOBA_BLOCK == 0 and seq % SSM_CHUNK == 0
    depth = w_in.shape[0]
    m = bsz * seq
    xbc_col = 3 * BRANCH_W + 3 * BRANCH_W + BRANCH_W
    dt_col = xbc_col + SSM_CONV_DIM
    d_col = dt_col + N_HEADS
    h = x.reshape(m, D_MODEL)
    tm_big = _row_tile(m, 1024)
    tm_mid = _row_tile(m, 512)
    for l in range(depth):
        w_main = jnp.concatenate([w_in[l, :, :xbc_col], w_in[l, :, d_col:d_col + BRANCH_W],
                                  w_in[l, :, xbc_col:dt_col], w_in[l, :, d_col + BRANCH_W:]],
                                 axis=1).astype(BF16)
        w_dt = jnp.pad(w_in[l, :, dt_col:dt_col + N_HEADS], ((0, 0), (0, LANES - N_HEADS))).astype(BF16)
        u, dt = _inproj(h, norm1_g[l][None, :], w_main, w_dt, tm_big, 1024)
        y_b = _sb_attention(u, bsz, seq, 256)
        y_d = _moba_attention(u, bsz, seq)
        y_a, y_c = _ssd(u, dt, conv_a_w[l], ssm_conv_w[l], ssm_conv_b[l][None, :],
                        _pad_lanes(ssm_dt_bias[l]), _pad_lanes(ssm_a_log[l]),
                        jnp.repeat(ssm_d[l], HEAD_DIM)[None, :], ssm_norm_g[l][None, :], bsz, seq)
        h = _merge(h, u, y_a, y_b, y_c, y_d, w_branch[l].astype(BF16), w_o[l].astype(BF16), tm_mid)
        h = _ffn(h, norm2_g[l][None, :], w_gate_up[l].astype(BF16), w_down[l].astype(BF16), tm_big, 256)
    out = _final_norm(h, final_g[None, :], tm_big)
    return out.reshape(bsz, seq, D_MODEL)
```

```python
import functools

import jax
import jax.numpy as jnp
from jax import lax
from jax.experimental import pallas as pl
from jax.experimental.pallas import tpu as pltpu

F32 = jnp.float32
BF16 = jnp.bfloat16

D_MODEL = 1024
HEAD_DIM = 64
BRANCH_W = 256
N_BRANCH = 4
N_HEADS = 4
SC_K = 3
SSM_GROUPS = 2
SSM_STATE = 64
SSM_CONV_K = 4
SSM_CHUNK = 256
SSM_CONV_DIM = BRANCH_W + 2 * SSM_GROUPS * SSM_STATE
MOBA_BLOCK = 256
MOBA_TOPK = 3
FFN_HIDDEN = 2816
RMS_EPS = 1e-6

LANES = 128
HIST = 8
NEG = -1e30

N_PACK = 7168
COL_A = 0
COL_SB_Q, COL_SB_K, COL_SB_V = 6, 8, 10
COL_Z = 12
COL_MO_Q = 14
COL_XBC = 16
COL_MO_K, COL_MO_V = 20, 22
COL_G = 24

VMEM_LIMIT = 56 * 1024 * 1024

def _cparams(sem):
    return pltpu.CompilerParams(dimension_semantics=sem, vmem_limit_bytes=VMEM_LIMIT)


def _sigmoid(x):
    return 1.0 / (1.0 + jnp.exp(-x))


def _inproj_kernel(x_ref, g_ref, w_ref, wdt_ref, u_ref, dt_ref, h_ref):
    @pl.when(pl.program_id(1) == 0)
    def _():
        x = x_ref[...]
        ms = jnp.mean(x * x, axis=-1, keepdims=True)
        h = (x * lax.rsqrt(ms + RMS_EPS) * g_ref[...]).astype(BF16)
        h_ref[...] = h
        dt_ref[...] = jnp.dot(h, wdt_ref[...], preferred_element_type=F32)

    u_ref[...] = jnp.dot(h_ref[...], w_ref[...], preferred_element_type=F32).astype(BF16)


def _inproj(x, g, w, wdt, tm, tn):
    m = x.shape[0]
    return pl.pallas_call(
        _inproj_kernel,
        grid=(m // tm, N_PACK // tn),
        in_specs=[
            pl.BlockSpec((tm, D_MODEL), lambda i, j: (i, 0)),
            pl.BlockSpec((1, D_MODEL), lambda i, j: (0, 0)),
            pl.BlockSpec((D_MODEL, tn), lambda i, j: (0, j)),
            pl.BlockSpec((D_MODEL, LANES), lambda i, j: (0, 0)),
        ],
        out_specs=[
            pl.BlockSpec((tm, tn), lambda i, j: (i, j)),
            pl.BlockSpec((tm, LANES), lambda i, j: (i, 0)),
        ],
        out_shape=[jax.ShapeDtypeStruct((m, N_PACK), BF16),
                   jax.ShapeDtypeStruct((m, LANES), F32)],
        scratch_shapes=[pltpu.VMEM((tm, D_MODEL), BF16)],
        compiler_params=_cparams(("parallel", "arbitrary")),
        name="inproj",
    )(x, g, w, wdt)


def _neg_abs(x):
    return pltpu.bitcast(pltpu.bitcast(x, jnp.uint32) | jnp.uint32(0x80000000), F32)


def _sb_kernel(q_ref, k_ref, v_ref, o_ref, vt_ref, z_ref, sp_ref, d_ref, w_ref, r_ref, acc_ref,
               *, t, nt):
    qi = pl.program_id(2)

    @pl.when(qi == 0)
    def _():
        head0_rows = lax.broadcasted_iota(jnp.int32, (LANES, t), 0) < HEAD_DIM

        def tr(c, carry):
            vt = v_ref[pl.ds(pl.multiple_of(c * t, t), t), :].astype(F32).T
            vt_ref[0, c] = jnp.where(head0_rows, vt, 0.0).astype(BF16)
            vt_ref[1, c] = jnp.where(head0_rows, 0.0, vt).astype(BF16)
            return carry

        lax.fori_loop(0, nt, tr, 0)

    lane = lax.broadcasted_iota(jnp.int32, (1, LANES), 1)
    head0 = lane < HEAD_DIM
    q = q_ref[...] * (HEAD_DIM ** -0.5)
    qh = (jnp.where(head0, q, 0), jnp.where(head0, 0, q))
    row = lax.broadcasted_iota(jnp.int32, (t, t), 0)
    col = lax.broadcasted_iota(jnp.int32, (t, t), 1)
    later_t = jnp.where(col > row, 1.0, 0.0).astype(BF16)
    nt_dims = (((1,), (1,)), ((), ()))
    n_tiles = qi + 1

    def tile_of(p):
        return jnp.maximum(qi - p, 0)

    def score_mm(p, par):
        k = k_ref[pl.ds(pl.multiple_of(tile_of(p) * t, t), t), :]
        for h in range(2):
            z_ref[par, h] = lax.dot_general(k, qh[h], nt_dims, preferred_element_type=F32)

    def softplus(par, masked):
        for h in range(2):
            z = z_ref[par, h]
            if masked:
                z = jnp.where(row < col, z, NEG)
            sp = jnp.maximum(z, 0.0) + jnp.log(1.0 + jnp.exp(_neg_abs(z)))
            sp_ref[par, h] = sp.astype(BF16)
            d_ref[par, h] = z - sp

    def weights(p, par):
        pad = jnp.where(p < n_tiles, 0.0, NEG)
        for h in range(2):
            sp = sp_ref[par, h]
            between = jnp.dot(later_t, sp, preferred_element_type=F32)
            r = r_ref[h]
            w_ref[par, h] = jnp.exp(d_ref[par, h] - between - (r - pad)).astype(BF16)
            r_ref[h] = r + between[0:1, :] + sp[0:1, :].astype(F32)

    def value_mm(p, par):
        tile = tile_of(p)
        acc_ref[...] += (jnp.dot(vt_ref[0, tile], w_ref[par, 0], preferred_element_type=F32)
                         + jnp.dot(vt_ref[1, tile], w_ref[par, 1], preferred_element_type=F32))

    def step(s, par):
        value_mm(s - 3, 1 - par)
        weights(s - 2, par)
        softplus(1 - par, False)
        score_mm(s, par)

    acc_ref[...] = jnp.zeros_like(acc_ref)
    r_ref[...] = jnp.zeros_like(r_ref)
    score_mm(0, 0)
    softplus(0, True)
    score_mm(1, 1)
    weights(0, 0)
    softplus(1, False)
    score_mm(2, 0)

    def body(i, carry):
        s = 3 + 2 * i
        step(s, 1)
        step(s + 1, 0)
        return carry

    lax.fori_loop(0, (n_tiles + 1) // 2, body, 0)
    o_ref[...] = acc_ref[...].T.astype(o_ref.dtype)


def _sb_attention(u, bsz, seq, t):
    nq = seq // t
    tile_f32 = pltpu.VMEM((2, 2, t, t), F32)
    tile_bf16 = pltpu.VMEM((2, 2, t, t), BF16)
    return pl.pallas_call(
        functools.partial(_sb_kernel, t=t, nt=nq),
        grid=(bsz, N_HEADS // 2, nq),
        in_specs=[
            pl.BlockSpec((t, LANES), lambda b, hp, qi: (b * nq + qi, COL_SB_Q + hp)),
            pl.BlockSpec((seq, LANES), lambda b, hp, qi: (b, COL_SB_K + hp)),
            pl.BlockSpec((seq, LANES), lambda b, hp, qi: (b, COL_SB_V + hp)),
        ],
        out_specs=pl.BlockSpec((t, LANES), lambda b, hp, qi: (b * nq + qi, hp)),
        out_shape=jax.ShapeDtypeStruct((bsz * seq, BRANCH_W), BF16),
        scratch_shapes=[
            pltpu.VMEM((2, nq, LANES, t), BF16),
            tile_f32,
            tile_bf16,
            tile_f32,
            tile_bf16,
            pltpu.VMEM((2, 1, t), F32),
            pltpu.VMEM((LANES, t), F32),
        ],
        compiler_params=_cparams(("parallel", "parallel", "arbitrary")),
        name="sb_attention",
    )(u, u, u)


def _moba_kernel(q_ref, k_ref, v_ref, o_ref, vt_ref, km_ref, sel_ref, z_ref, p_ref, alpha_ref, m_ref,
                 acc_ref, *, nb, ke):
    t = MOBA_BLOCK
    qi = pl.program_id(2)
    n_tiles = qi + 1
    ones_row = (HEAD_DIM, 0)

    @pl.when(qi == 0)
    def _():
        rows = lax.broadcasted_iota(jnp.int32, (LANES, t), 0)

        def tr(c, carry):
            kv = pl.ds(pl.multiple_of(c * t, t), t)
            vt = v_ref[kv, :].astype(F32).T
            vt_ref[0, c] = jnp.where(rows < HEAD_DIM, vt,
                                     jnp.where(rows == ones_row[0], 1.0, 0.0)).astype(BF16)
            vt_ref[1, c] = jnp.where(rows >= HEAD_DIM, vt,
                                     jnp.where(rows == ones_row[1], 1.0, 0.0)).astype(BF16)
            km_ref[pl.ds(c, 1), :] = jnp.sum(k_ref[kv, :].astype(F32), axis=0, keepdims=True) * (1.0 / t)
            return carry

        km_ref[...] = jnp.zeros_like(km_ref)
        lax.fori_loop(0, nb, tr, 0)

    lane = lax.broadcasted_iota(jnp.int32, (1, LANES), 1)
    head0 = lane < HEAD_DIM
    q = q_ref[...]
    qs = q * (HEAD_DIM ** -0.5)
    qh = (jnp.where(head0, q, 0), jnp.where(head0, 0, q))
    qsh = (jnp.where(head0, qs, 0), jnp.where(head0, 0, qs))
    nt_dims = (((1,), (1,)), ((), ()))

    km = km_ref[...]
    km_hi = km.astype(BF16)
    km_lo = (km - km_hi.astype(F32)).astype(BF16)
    blk = lax.broadcasted_iota(jnp.int32, (LANES, t), 0).astype(F32)
    qif = qi.astype(F32)
    for h in range(2):
        gate = (lax.dot_general(km_hi, qh[h], nt_dims, preferred_element_type=F32)
                + lax.dot_general(km_lo, qh[h], nt_dims, preferred_element_type=F32))
        g = jnp.where(blk < qif, gate, -jnp.inf)
        sel = jnp.zeros((LANES, t), F32)
        for r in range(ke):
            mx = jnp.max(g, axis=0, keepdims=True)
            idx = jnp.min(jnp.where(g == mx, blk, 1e9), axis=0, keepdims=True)
            hit = blk == idx
            sel = jnp.where(jnp.logical_and(hit, qif > r), 1.0, sel)
            g = jnp.where(hit, -jnp.inf, g)
        sel_ref[h] = sel

    row = lax.broadcasted_iota(jnp.int32, (t, t), 0)
    col = lax.broadcasted_iota(jnp.int32, (t, t), 1)

    def tile_of(p):
        return jnp.maximum(qi - p, 0)

    def score_mm(p, par):
        k = k_ref[pl.ds(pl.multiple_of(tile_of(p) * t, t), t), :]
        for h in range(2):
            z_ref[par, h] = lax.dot_general(k, qsh[h], nt_dims, preferred_element_type=F32)

    def softmax(p, par, own):
        for h in range(2):
            s = z_ref[par, h]
            if own:
                s = jnp.where(row <= col, s, -jnp.inf)
                m_old = jnp.full((1, t), -jnp.inf, F32)
            else:
                chosen = sel_ref[h, pl.ds(tile_of(p), 1), :] * jnp.where(p < n_tiles, 1.0, 0.0)
                s = s + jnp.where(chosen > 0.5, 0.0, -jnp.inf)
                m_old = m_ref[h]
            m_new = jnp.maximum(m_old, jnp.max(s, axis=0, keepdims=True))
            alpha_ref[par, h] = jnp.zeros((1, t), F32) if own else jnp.exp(m_old - m_new)
            p_ref[par, h] = jnp.exp(s - m_new).astype(BF16)
            m_ref[h] = m_new

    def value_mm(p, par):
        tile = tile_of(p)
        for h in range(2):
            acc_ref[h] = (acc_ref[h] * alpha_ref[par, h]
                          + jnp.dot(vt_ref[h, tile], p_ref[par, h], preferred_element_type=F32))

    def step(s, par):
        value_mm(s - 2, par)
        softmax(s - 1, 1 - par, False)
        score_mm(s, par)

    acc_ref[...] = jnp.zeros_like(acc_ref)
    score_mm(0, 0)
    softmax(0, 0, True)
    score_mm(1, 1)

    def body(i, carry):
        s = 2 + 2 * i
        step(s, 0)
        step(s + 1, 1)
        return carry

    lax.fori_loop(0, (n_tiles + 1) // 2, body, 0)
    rows = lax.broadcasted_iota(jnp.int32, (LANES, t), 0)
    a0 = acc_ref[0]
    a1 = acc_ref[1]
    out_t = jnp.where(rows < HEAD_DIM, a0 / a0[ones_row[0]:ones_row[0] + 1, :],
                      a1 / a1[ones_row[1]:ones_row[1] + 1, :])
    o_ref[...] = out_t.T.astype(o_ref.dtype)


def _moba_attention(u, bsz, seq):
    t = MOBA_BLOCK
    nb = seq // t
    assert nb <= LANES
    ke = max(1, min(MOBA_TOPK, nb - 1))
    return pl.pallas_call(
        functools.partial(_moba_kernel, nb=nb, ke=ke),
        grid=(bsz, N_HEADS // 2, nb),
        in_specs=[
            pl.BlockSpec((t, LANES), lambda b, hp, qi: (b * nb + qi, COL_MO_Q + hp)),
            pl.BlockSpec((seq, LANES), lambda b, hp, qi: (b, COL_MO_K + hp)),
            pl.BlockSpec((seq, LANES), lambda b, hp, qi: (b, COL_MO_V + hp)),
        ],
        out_specs=pl.BlockSpec((t, LANES), lambda b, hp, qi: (b * nb + qi, hp)),
        out_shape=jax.ShapeDtypeStruct((bsz * seq, BRANCH_W), BF16),
        scratch_shapes=[
            pltpu.VMEM((2, nb, LANES, t), BF16),
            pltpu.VMEM((LANES, LANES), F32),
            pltpu.VMEM((2, LANES, t), F32),
            pltpu.VMEM((2, 2, t, t), F32),
            pltpu.VMEM((2, 2, t, t), BF16),
            pltpu.VMEM((2, 2, 1, t), F32),
            pltpu.VMEM((2, 1, t), F32),
            pltpu.VMEM((2, LANES, t), F32),
        ],
        compiler_params=_cparams(("parallel", "parallel", "arbitrary")),
        name="moba_attention",
    )(u, u, u)


def _per_head(v, lanes_per_head, width):
    head = lax.broadcasted_iota(jnp.int32, (1, width), 1) // lanes_per_head
    out = jnp.zeros((v.shape[0], width), F32)
    for h in range(N_HEADS):
        out = jnp.where(head == h, v[:, h:h + 1], out)
    return out


def _split3(a):
    hi = a.astype(BF16)
    r1 = a - hi.astype(F32)
    mid = r1.astype(BF16)
    lo = (r1 - mid.astype(F32)).astype(BF16)
    return hi, mid, lo


def _ssd_kernel(ua_ref, z_ref, xbc_ref, dt_ref, cwa_ref, cwc_ref, cbias_ref, dtb_ref, alog_ref,
                dskip_ref, ng_ref, ya_ref, yc_ref, bufa, bufc, hst):
    t = SSM_CHUNK
    w = BRANCH_W

    @pl.when(pl.program_id(1) == 0)
    def _():
        bufa[0:HIST, :] = jnp.zeros((HIST, w), F32)
        bufc[0:HIST, :] = jnp.zeros((HIST, SSM_CONV_DIM), F32)
        hst[...] = jnp.zeros_like(hst)

    ua = ua_ref[...].astype(F32)
    bufa[HIST:, :] = ua[:, 2 * w:] * ua[:, :w]
    conv = cwa_ref[0:1, :] * bufa[pl.ds(HIST - SC_K + 1, t), :]
    for kk in range(1, SC_K):
        conv = conv + cwa_ref[kk:kk + 1, :] * bufa[pl.ds(HIST - SC_K + 1 + kk, t), :]
    ya_ref[...] = (ua[:, w:2 * w] * conv).astype(ya_ref.dtype)
    bufa[0:HIST, :] = bufa[t:t + HIST, :]

    bufc[HIST:, :] = xbc_ref[...].astype(F32)
    xc = cbias_ref[...] + cwc_ref[0:1, :] * bufc[pl.ds(HIST - SSM_CONV_K + 1, t), :]
    for kk in range(1, SSM_CONV_K):
        xc = xc + cwc_ref[kk:kk + 1, :] * bufc[pl.ds(HIST - SSM_CONV_K + 1 + kk, t), :]
    bufc[0:HIST, :] = bufc[t:t + HIST, :]
    xc = xc * _sigmoid(xc)
    xs = xc[:, :w]
    b_in = xc[:, w:w + LANES]
    c_in = xc[:, w + LANES:]

    dtp = dt_ref[...] + dtb_ref[...]
    dt = jnp.maximum(dtp, 0.0) + jnp.log(1.0 + jnp.exp(-jnp.abs(dtp)))
    a = dt * (-jnp.exp(alog_ref[...]))
    row = lax.broadcasted_iota(jnp.int32, (t, t), 0)
    col = lax.broadcasted_iota(jnp.int32, (t, t), 1)
    causal = row >= col
    tri = jnp.where(causal, 1.0, 0.0).astype(BF16)
    a_hi, a_mid, a_lo = _split3(a)
    acs = (jnp.dot(tri, a_hi, preferred_element_type=F32)
           + jnp.dot(tri, a_mid, preferred_element_type=F32)
           + jnp.dot(tri, a_lo, preferred_element_type=F32))
    acs_t = acs.T
    acs_x = _per_head(acs, HEAD_DIM, w)
    last_x = acs_x[t - 1:t, :]
    x_dt = xs * _per_head(dt, HEAD_DIM, w)
    to_end_x = jnp.exp(last_x - acs_x)
    from_start_x = jnp.exp(acs_x)
    chunk_decay_x = jnp.exp(last_x)

    lane = lax.broadcasted_iota(jnp.int32, (1, LANES), 1)
    low = lane < HEAD_DIM
    c_bf = c_in.astype(BF16)
    nt = (((1,), (1,)), ((), ()))
    for g in range(SSM_GROUPS):
        gmask = low if g == 0 else jnp.logical_not(low)
        sl = slice(g * LANES, (g + 1) * LANES)
        b_g = jnp.where(gmask, b_in, 0.0)
        cb = lax.dot_general(c_bf, b_g.astype(BF16), nt, preferred_element_type=F32)
        xg = x_dt[:, sl]
        y = jnp.zeros((t, LANES), F32)
        for e in range(2):
            h = 2 * g + e
            seg = acs[:, h:h + 1] - acs_t[h:h + 1, :]
            decay = jnp.exp(jnp.where(causal, seg, -jnp.inf))
            emask = low if e == 0 else jnp.logical_not(low)
            xe = jnp.where(emask, xg, 0.0).astype(BF16)
            y = y + jnp.dot((cb * decay).astype(BF16), xe, preferred_element_type=F32)
        h_enter = hst[g]
        y = y + jnp.dot(c_bf, h_enter.astype(BF16), preferred_element_type=F32) * from_start_x[:, sl]
        state = jnp.dot(b_g.T.astype(BF16), (xg * to_end_x[:, sl]).astype(BF16),
                        preferred_element_type=F32)
        hst[g] = h_enter * chunk_decay_x[:, sl] + state

        y = y + xs[:, sl] * dskip_ref[:, sl]
        zg = z_ref[:, sl].astype(F32)
        gated = y * (zg * _sigmoid(zg))
        ms = jnp.mean(gated * gated, axis=-1, keepdims=True)
        yc_ref[:, sl] = (gated * lax.rsqrt(ms + RMS_EPS) * ng_ref[:, sl]).astype(yc_ref.dtype)


def _ssd(u, dt, cwa, cwc, cbias, dtb, alog, dskip, ng, bsz, seq):
    t = SSM_CHUNK
    nc = seq // t
    w = BRANCH_W
    small = lambda shape: pl.BlockSpec(shape, lambda b, c: (0, 0))
    return pl.pallas_call(
        _ssd_kernel,
        grid=(bsz, nc),
        in_specs=[
            pl.BlockSpec((t, 3 * w), lambda b, c: (b * nc + c, COL_A)),
            pl.BlockSpec((t, w), lambda b, c: (b * nc + c, COL_Z // 2)),
            pl.BlockSpec((t, SSM_CONV_DIM), lambda b, c: (b * nc + c, COL_XBC * LANES // SSM_CONV_DIM)),
            pl.BlockSpec((t, LANES), lambda b, c: (b * nc + c, 0)),
            small((SC_K, w)), small((SSM_CONV_K, SSM_CONV_DIM)), small((1, SSM_CONV_DIM)),
            small((1, LANES)), small((1, LANES)), small((1, w)), small((1, w)),
        ],
        out_specs=[pl.BlockSpec((t, w), lambda b, c: (b * nc + c, 0)),
                   pl.BlockSpec((t, w), lambda b, c: (b * nc + c, 0))],
        out_shape=[jax.ShapeDtypeStruct((bsz * seq, w), BF16),
                   jax.ShapeDtypeStruct((bsz * seq, w), BF16)],
        scratch_shapes=[
            pltpu.VMEM((t + HIST, w), F32),
            pltpu.VMEM((t + HIST, SSM_CONV_DIM), F32),
            pltpu.VMEM((SSM_GROUPS, LANES, LANES), F32),
        ],
        compiler_params=_cparams(("parallel", "arbitrary")),
        name="conv_ssd",
    )(u, u, u, dt, cwa, cwc, cbias, dtb, alog, dskip, ng)


def _merge_kernel(x_ref, ga_ref, gb_ref, gc_ref, gd_ref, ya_ref, yb_ref, yc_ref, yd_ref,
                  wb_ref, wo_ref, o_ref):
    merged = None
    branches = ((ga_ref, ya_ref), (gb_ref, yb_ref), (gc_ref, yc_ref), (gd_ref, yd_ref))
    for i, (g_ref, y_ref) in enumerate(branches):
        gate = _sigmoid(g_ref[...].astype(F32))
        term = gate * jnp.dot(y_ref[...], wb_ref[i], preferred_element_type=F32)
        merged = term if merged is None else merged + term
    o_ref[...] = x_ref[...] + jnp.dot(merged.astype(BF16), wo_ref[...], preferred_element_type=F32)


def _merge(x, u, ya, yb, yc, yd, wb, wo, tm):
    m = x.shape[0]
    ybr = pl.BlockSpec((tm, BRANCH_W), lambda i: (i, 0))
    g0 = COL_G * LANES // D_MODEL
    gates = [pl.BlockSpec((tm, D_MODEL), functools.partial(lambda i, c: (i, c), c=g0 + br))
             for br in range(N_BRANCH)]
    return pl.pallas_call(
        _merge_kernel,
        grid=(m // tm,),
        in_specs=[
            pl.BlockSpec((tm, D_MODEL), lambda i: (i, 0)),
            *gates,
            ybr, ybr, ybr, ybr,
            pl.BlockSpec((N_BRANCH, BRANCH_W, D_MODEL), lambda i: (0, 0, 0)),
            pl.BlockSpec((D_MODEL, D_MODEL), lambda i: (0, 0)),
        ],
        out_specs=pl.BlockSpec((tm, D_MODEL), lambda i: (i, 0)),
        out_shape=jax.ShapeDtypeStruct((m, D_MODEL), F32),
        compiler_params=_cparams(("parallel",)),
        name="merge",
    )(x, u, u, u, u, ya, yb, yc, yd, wb, wo)


def _ffn_kernel(x_ref, g_ref, wgu_ref, wd_ref, fg_ref, o_ref, h_ref, act_ref, *, th, final_norm):
    x = x_ref[...]
    ms = jnp.mean(x * x, axis=-1, keepdims=True)
    h_ref[...] = (x * lax.rsqrt(ms + RMS_EPS) * g_ref[...]).astype(BF16)
    for c in range(FFN_HIDDEN // th):
        h = h_ref[...]
        gate = jnp.dot(h, wgu_ref[:, c * th:(c + 1) * th], preferred_element_type=F32)
        up = jnp.dot(h, wgu_ref[:, FFN_HIDDEN + c * th:FFN_HIDDEN + (c + 1) * th],
                     preferred_element_type=F32)
        act_ref[:, c * th:(c + 1) * th] = ((gate * _sigmoid(gate)) * up).astype(BF16)
    y = x_ref[...] + jnp.dot(act_ref[...], wd_ref[...], preferred_element_type=F32)
    if final_norm:
        ms = jnp.mean(y * y, axis=-1, keepdims=True)
        y = y * lax.rsqrt(ms + RMS_EPS) * fg_ref[...]
    o_ref[...] = y


def _ffn(x, g, wgu, wd, final_g, tm, th, final_norm):
    m = x.shape[0]
    resident = pl.Buffered(1)
    return pl.pallas_call(
        functools.partial(_ffn_kernel, th=th, final_norm=final_norm),
        grid=(m // tm,),
        in_specs=[
            pl.BlockSpec((tm, D_MODEL), lambda i: (i, 0)),
            pl.BlockSpec((1, D_MODEL), lambda i: (0, 0)),
            pl.BlockSpec((D_MODEL, 2 * FFN_HIDDEN), lambda i: (0, 0), pipeline_mode=resident),
            pl.BlockSpec((FFN_HIDDEN, D_MODEL), lambda i: (0, 0), pipeline_mode=resident),
            pl.BlockSpec((1, D_MODEL), lambda i: (0, 0)),
        ],
        out_specs=pl.BlockSpec((tm, D_MODEL), lambda i: (i, 0)),
        out_shape=jax.ShapeDtypeStruct((m, D_MODEL), F32),
        scratch_shapes=[pltpu.VMEM((tm, D_MODEL), BF16), pltpu.VMEM((tm, FFN_HIDDEN), BF16)],
        compiler_params=_cparams(("parallel",)),
        name="ffn",
    )(x, g, wgu, wd, final_g)


def _row_tile(m, want):
    while m % want:
        want //= 2
    return want


def _pad_lanes(v):
    return jnp.pad(v.astype(F32), (0, LANES - v.shape[0]))[None, :]


def kernel(x, norm1_g, w_in, conv_a_w, ssm_conv_w, ssm_conv_b, ssm_dt_bias, ssm_a_log, ssm_d,
           ssm_norm_g, w_branch, w_o, norm2_g, w_gate_up, w_down, final_g):
    bsz, seq, _ = x.shape
    assert seq % MOBA_BLOCK == 0 and seq % SSM_CHUNK == 0
    depth = w_in.shape[0]
    assert depth >= 1
    m = bsz * seq
    xbc_col = 3 * BRANCH_W + 3 * BRANCH_W + BRANCH_W
    dt_col = xbc_col + SSM_CONV_DIM
    d_col = dt_col + N_HEADS
    h = x.reshape(m, D_MODEL)
    tm_big = _row_tile(m, 1024)
    tm_mid = _row_tile(m, 512)
    for l in range(depth):
        w_main = jnp.concatenate([w_in[l, :, :xbc_col], w_in[l, :, d_col:d_col + BRANCH_W],
                                  w_in[l, :, xbc_col:dt_col], w_in[l, :, d_col + BRANCH_W:]],
                                 axis=1).astype(BF16)
        w_dt = jnp.pad(w_in[l, :, dt_col:dt_col + N_HEADS], ((0, 0), (0, LANES - N_HEADS))).astype(BF16)
        u, dt = _inproj(h, norm1_g[l][None, :], w_main, w_dt, tm_big, 1024)
        y_b = _sb_attention(u, bsz, seq, 256)
        y_d = _moba_attention(u, bsz, seq)
        y_a, y_c = _ssd(u, dt, conv_a_w[l], ssm_conv_w[l], ssm_conv_b[l][None, :],
                        _pad_lanes(ssm_dt_bias[l]), _pad_lanes(ssm_a_log[l]),
                        jnp.repeat(ssm_d[l], HEAD_DIM)[None, :], ssm_norm_g[l][None, :], bsz, seq)
        h = _merge(h, u, y_a, y_b, y_c, y_d, w_branch[l].astype(BF16), w_o[l].astype(BF16), tm_mid)
        h = _ffn(h, norm2_g[l][None, :], w_gate_up[l].astype(BF16), w_down[l].astype(BF16),
                 final_g[None, :], tm_mid, 256, final_norm=(l == depth - 1))
    return h.reshape(bsz, seq, D_MODEL)
```

```python
import functools

import jax
import jax.numpy as jnp
from jax import lax
from jax.experimental import pallas as pl
from jax.experimental.pallas import tpu as pltpu

F32 = jnp.float32
BF16 = jnp.bfloat16

D_MODEL = 1024
HEAD_DIM = 64
BRANCH_W = 256
N_BRANCH = 4
N_HEADS = 4
SC_K = 3
SSM_GROUPS = 2
SSM_STATE = 64
SSM_CONV_K = 4
SSM_CHUNK = 256
SSM_CONV_DIM = BRANCH_W + 2 * SSM_GROUPS * SSM_STATE
MOBA_BLOCK = 256
MOBA_TOPK = 3
FFN_HIDDEN = 2816
RMS_EPS = 1e-6

LANES = 128
HIST = 8
NEG = -1e30

N_PACK = 7168
COL_A = 0
COL_SB_Q, COL_SB_K, COL_SB_V = 6, 8, 10
COL_Z = 12
COL_MO_Q = 14
COL_XBC = 16
COL_MO_K, COL_MO_V = 20, 22
COL_G = 24

VMEM_LIMIT = 56 * 1024 * 1024

def _cparams(sem):
    return pltpu.CompilerParams(dimension_semantics=sem, vmem_limit_bytes=VMEM_LIMIT)


def _sigmoid(x):
    return 1.0 / (1.0 + jnp.exp(-x))


def _inproj_kernel(x_ref, g_ref, w_ref, wdt_ref, u_ref, dt_ref, h_ref, *, tn):
    x = x_ref[...]
    ms = jnp.mean(x * x, axis=-1, keepdims=True)
    h_ref[...] = (x * lax.rsqrt(ms + RMS_EPS) * g_ref[...]).astype(BF16)
    dt_ref[...] = jnp.dot(h_ref[...], wdt_ref[...], preferred_element_type=F32)
    for c in range(N_PACK // tn):
        sl = slice(c * tn, (c + 1) * tn)
        u_ref[:, sl] = jnp.dot(h_ref[...], w_ref[:, sl], preferred_element_type=F32).astype(BF16)


def _inproj(x, g, w, wdt, tm, tn):
    m = x.shape[0]
    resident = pl.Buffered(1)
    return pl.pallas_call(
        functools.partial(_inproj_kernel, tn=tn),
        grid=(m // tm,),
        in_specs=[
            pl.BlockSpec((tm, D_MODEL), lambda i: (i, 0)),
            pl.BlockSpec((1, D_MODEL), lambda i: (0, 0)),
            pl.BlockSpec((D_MODEL, N_PACK), lambda i: (0, 0), pipeline_mode=resident),
            pl.BlockSpec((D_MODEL, LANES), lambda i: (0, 0)),
        ],
        out_specs=[
            pl.BlockSpec((tm, N_PACK), lambda i: (i, 0)),
            pl.BlockSpec((tm, LANES), lambda i: (i, 0)),
        ],
        out_shape=[jax.ShapeDtypeStruct((m, N_PACK), BF16),
                   jax.ShapeDtypeStruct((m, LANES), F32)],
        scratch_shapes=[pltpu.VMEM((tm, D_MODEL), BF16)],
        compiler_params=_cparams(("parallel",)),
        name="inproj",
    )(x, g, w, wdt)


def _neg_abs(x):
    return pltpu.bitcast(pltpu.bitcast(x, jnp.uint32) | jnp.uint32(0x80000000), F32)


def _sb_kernel(q_ref, k_ref, v_ref, o_ref, vt_ref, z_ref, sp_ref, d_ref, w_ref, r_ref, acc_ref,
               *, t, nt):
    qi = pl.program_id(2)

    @pl.when(qi == 0)
    def _():
        head0_rows = lax.broadcasted_iota(jnp.int32, (LANES, t), 0) < HEAD_DIM

        def tr(c, carry):
            vt = v_ref[pl.ds(pl.multiple_of(c * t, t), t), :].astype(F32).T
            vt_ref[0, c] = jnp.where(head0_rows, vt, 0.0).astype(BF16)
            vt_ref[1, c] = jnp.where(head0_rows, 0.0, vt).astype(BF16)
            return carry

        lax.fori_loop(0, nt, tr, 0)

    lane = lax.broadcasted_iota(jnp.int32, (1, LANES), 1)
    head0 = lane < HEAD_DIM
    q = q_ref[...] * (HEAD_DIM ** -0.5)
    qh = (jnp.where(head0, q, 0), jnp.where(head0, 0, q))
    row = lax.broadcasted_iota(jnp.int32, (t, t), 0)
    col = lax.broadcasted_iota(jnp.int32, (t, t), 1)
    later_t = jnp.where(col > row, 1.0, 0.0).astype(BF16)
    nt_dims = (((1,), (1,)), ((), ()))
    n_tiles = qi + 1

    def tile_of(p):
        return jnp.maximum(qi - p, 0)

    def score_mm(p, par):
        k = k_ref[pl.ds(pl.multiple_of(tile_of(p) * t, t), t), :]
        for h in range(2):
            z_ref[par, h] = lax.dot_general(k, qh[h], nt_dims, preferred_element_type=F32)

    def softplus(par, masked):
        for h in range(2):
            z = z_ref[par, h]
            if masked:
                z = jnp.where(row < col, z, NEG)
            sp = jnp.maximum(z, 0.0) + jnp.log(1.0 + jnp.exp(_neg_abs(z)))
            sp_ref[par, h] = sp.astype(BF16)
            d_ref[par, h] = z - sp

    def weights(par):
        for h in range(2):
            sp = sp_ref[par, h]
            between = jnp.dot(later_t, sp, preferred_element_type=F32)
            r = r_ref[h]
            w_ref[par, h] = jnp.exp(d_ref[par, h] - between - r).astype(BF16)
            r_ref[h] = r + between[0:1, :] + sp[0:1, :].astype(F32)

    def value_mm(p, par):
        tile = tile_of(p)
        acc_ref[...] += (jnp.dot(vt_ref[0, tile], w_ref[par, 0], preferred_element_type=F32)
                         + jnp.dot(vt_ref[1, tile], w_ref[par, 1], preferred_element_type=F32))

    def step(s, par):
        value_mm(s - 3, 1 - par)
        weights(par)
        softplus(1 - par, False)
        score_mm(s, par)

    acc_ref[...] = jnp.zeros_like(acc_ref)
    r_ref[...] = jnp.zeros_like(r_ref)
    score_mm(0, 0)
    softplus(0, True)
    score_mm(1, 1)
    weights(0)
    softplus(1, False)
    score_mm(2, 0)

    n_full = jnp.maximum(n_tiles - 3, 0)

    def body(i, carry):
        s = 3 + 2 * i
        step(s, 1)
        step(s + 1, 0)
        return carry

    lax.fori_loop(0, n_full // 2, body, 0)

    @pl.when(n_full % 2 == 1)
    def _():
        step(n_tiles - 1, 1)

    for par_t in range(2):
        @pl.when(n_tiles % 2 == par_t)
        def _():
            @pl.when(n_tiles >= 3)
            def _():
                value_mm(n_tiles - 3, 1 - par_t)
                weights(par_t)
                softplus(1 - par_t, False)

            @pl.when(n_tiles >= 2)
            def _():
                value_mm(n_tiles - 2, par_t)
                weights(1 - par_t)

            value_mm(n_tiles - 1, 1 - par_t)

    o_ref[...] = acc_ref[...].T.astype(o_ref.dtype)


def _sb_attention(u, bsz, seq, t):
    nq = seq // t
    tile_f32 = pltpu.VMEM((2, 2, t, t), F32)
    tile_bf16 = pltpu.VMEM((2, 2, t, t), BF16)
    return pl.pallas_call(
        functools.partial(_sb_kernel, t=t, nt=nq),
        grid=(bsz, N_HEADS // 2, nq),
        in_specs=[
            pl.BlockSpec((t, LANES), lambda b, hp, qi: (b * nq + qi, COL_SB_Q + hp)),
            pl.BlockSpec((seq, LANES), lambda b, hp, qi: (b, COL_SB_K + hp)),
            pl.BlockSpec((seq, LANES), lambda b, hp, qi: (b, COL_SB_V + hp)),
        ],
        out_specs=pl.BlockSpec((t, LANES), lambda b, hp, qi: (b * nq + qi, hp)),
        out_shape=jax.ShapeDtypeStruct((bsz * seq, BRANCH_W), BF16),
        scratch_shapes=[
            pltpu.VMEM((2, nq, LANES, t), BF16),
            tile_f32,
            tile_bf16,
            tile_f32,
            tile_bf16,
            pltpu.VMEM((2, 1, t), F32),
            pltpu.VMEM((LANES, t), F32),
        ],
        compiler_params=_cparams(("parallel", "parallel", "arbitrary")),
        name="sb_attention",
    )(u, u, u)


def _moba_kernel(q_ref, k_ref, v_ref, o_ref, vt_ref, km_ref, sel_ref, z_ref, p_ref, alpha_ref, m_ref,
                 acc_ref, *, nb, ke):
    t = MOBA_BLOCK
    qi = pl.program_id(2)
    n_tiles = qi + 1
    ones_row = (HEAD_DIM, 0)

    @pl.when(qi == 0)
    def _():
        rows = lax.broadcasted_iota(jnp.int32, (LANES, t), 0)

        def tr(c, carry):
            kv = pl.ds(pl.multiple_of(c * t, t), t)
            vt = v_ref[kv, :].astype(F32).T
            vt_ref[0, c] = jnp.where(rows < HEAD_DIM, vt,
                                     jnp.where(rows == ones_row[0], 1.0, 0.0)).astype(BF16)
            vt_ref[1, c] = jnp.where(rows >= HEAD_DIM, vt,
                                     jnp.where(rows == ones_row[1], 1.0, 0.0)).astype(BF16)
            km_ref[pl.ds(c, 1), :] = jnp.sum(k_ref[kv, :].astype(F32), axis=0, keepdims=True) * (1.0 / t)
            return carry

        km_ref[...] = jnp.zeros_like(km_ref)
        lax.fori_loop(0, nb, tr, 0)

    lane = lax.broadcasted_iota(jnp.int32, (1, LANES), 1)
    head0 = lane < HEAD_DIM
    q = q_ref[...]
    qs = q * (HEAD_DIM ** -0.5)
    qh = (jnp.where(head0, q, 0), jnp.where(head0, 0, q))
    qsh = (jnp.where(head0, qs, 0), jnp.where(head0, 0, qs))
    nt_dims = (((1,), (1,)), ((), ()))

    km = km_ref[...]
    km_hi = km.astype(BF16)
    km_lo = (km - km_hi.astype(F32)).astype(BF16)
    blk = lax.broadcasted_iota(jnp.int32, (LANES, t), 0).astype(F32)
    qif = qi.astype(F32)
    for h in range(2):
        gate = (lax.dot_general(km_hi, qh[h], nt_dims, preferred_element_type=F32)
                + lax.dot_general(km_lo, qh[h], nt_dims, preferred_element_type=F32))
        g = jnp.where(blk < qif, gate, -jnp.inf)
        sel = jnp.zeros((LANES, t), F32)
        for r in range(ke):
            mx = jnp.max(g, axis=0, keepdims=True)
            idx = jnp.min(jnp.where(g == mx, blk, 1e9), axis=0, keepdims=True)
            hit = blk == idx
            sel = jnp.where(jnp.logical_and(hit, qif > r), 1.0, sel)
            g = jnp.where(hit, -jnp.inf, g)
        sel_ref[h] = sel

    row = lax.broadcasted_iota(jnp.int32, (t, t), 0)
    col = lax.broadcasted_iota(jnp.int32, (t, t), 1)

    def tile_of(p):
        return jnp.maximum(qi - p, 0)

    def score_mm(p, par):
        k = k_ref[pl.ds(pl.multiple_of(tile_of(p) * t, t), t), :]
        for h in range(2):
            z_ref[par, h] = lax.dot_general(k, qsh[h], nt_dims, preferred_element_type=F32)

    def softmax(p, par, own):
        for h in range(2):
            s = z_ref[par, h]
            if own:
                s = jnp.where(row <= col, s, -jnp.inf)
                m_old = jnp.full((1, t), -jnp.inf, F32)
            else:
                chosen = sel_ref[h, pl.ds(tile_of(p), 1), :]
                s = s + jnp.where(chosen > 0.5, 0.0, -jnp.inf)
                m_old = m_ref[h]
            m_new = jnp.maximum(m_old, jnp.max(s, axis=0, keepdims=True))
            alpha_ref[par, h] = jnp.zeros((1, t), F32) if own else jnp.exp(m_old - m_new)
            p_ref[par, h] = jnp.exp(s - m_new).astype(BF16)
            m_ref[h] = m_new

    def value_mm(p, par):
        tile = tile_of(p)
        for h in range(2):
            acc_ref[h] = (acc_ref[h] * alpha_ref[par, h]
                          + jnp.dot(vt_ref[h, tile], p_ref[par, h], preferred_element_type=F32))

    def step(s, par):
        value_mm(s - 2, par)
        softmax(s - 1, 1 - par, False)
        score_mm(s, par)

    acc_ref[...] = jnp.zeros_like(acc_ref)
    score_mm(0, 0)
    softmax(0, 0, True)
    score_mm(1, 1)

    n_full = jnp.maximum(n_tiles - 2, 0)

    def body(i, carry):
        s = 2 + 2 * i
        step(s, 0)
        step(s + 1, 1)
        return carry

    lax.fori_loop(0, n_full // 2, body, 0)

    @pl.when(n_full % 2 == 1)
    def _():
        step(n_tiles - 1, 0)

    for par_t in range(2):
        @pl.when(n_tiles % 2 == par_t)
        def _():
            @pl.when(n_tiles >= 2)
            def _():
                value_mm(n_tiles - 2, par_t)
                softmax(n_tiles - 1, 1 - par_t, False)

            value_mm(n_tiles - 1, 1 - par_t)

    rows = lax.broadcasted_iota(jnp.int32, (LANES, t), 0)
    a0 = acc_ref[0]
    a1 = acc_ref[1]
    out_t = jnp.where(rows < HEAD_DIM, a0 / a0[ones_row[0]:ones_row[0] + 1, :],
                      a1 / a1[ones_row[1]:ones_row[1] + 1, :])
    o_ref[...] = out_t.T.astype(o_ref.dtype)


def _moba_attention(u, bsz, seq):
    t = MOBA_BLOCK
    nb = seq // t
    assert nb <= LANES
    ke = max(1, min(MOBA_TOPK, nb - 1))
    return pl.pallas_call(
        functools.partial(_moba_kernel, nb=nb, ke=ke),
        grid=(bsz, N_HEADS // 2, nb),
        in_specs=[
            pl.BlockSpec((t, LANES), lambda b, hp, qi: (b * nb + qi, COL_MO_Q + hp)),
            pl.BlockSpec((seq, LANES), lambda b, hp, qi: (b, COL_MO_K + hp)),
            pl.BlockSpec((seq, LANES), lambda b, hp, qi: (b, COL_MO_V + hp)),
        ],
        out_specs=pl.BlockSpec((t, LANES), lambda b, hp, qi: (b * nb + qi, hp)),
        out_shape=jax.ShapeDtypeStruct((bsz * seq, BRANCH_W), BF16),
        scratch_shapes=[
            pltpu.VMEM((2, nb, LANES, t), BF16),
            pltpu.VMEM((LANES, LANES), F32),
            pltpu.VMEM((2, LANES, t), F32),
            pltpu.VMEM((2, 2, t, t), F32),
            pltpu.VMEM((2, 2, t, t), BF16),
            pltpu.VMEM((2, 2, 1, t), F32),
            pltpu.VMEM((2, 1, t), F32),
            pltpu.VMEM((2, LANES, t), F32),
        ],
        compiler_params=_cparams(("parallel", "parallel", "arbitrary")),
        name="moba_attention",
    )(u, u, u)


def _per_head(v, lanes_per_head, width):
    head = lax.broadcasted_iota(jnp.int32, (1, width), 1) // lanes_per_head
    out = jnp.zeros((v.shape[0], width), F32)
    for h in range(N_HEADS):
        out = jnp.where(head == h, v[:, h:h + 1], out)
    return out


def _split3(a):
    hi = a.astype(BF16)
    r1 = a - hi.astype(F32)
    mid = r1.astype(BF16)
    lo = (r1 - mid.astype(F32)).astype(BF16)
    return hi, mid, lo


def _ssd_kernel(ua_ref, z_ref, xbc_ref, dt_ref, cwa_ref, cwc_ref, cbias_ref, dtb_ref, alog_ref,
                dskip_ref, ng_ref, ya_ref, yc_ref, bufa, bufc, hst):
    t = SSM_CHUNK
    w = BRANCH_W

    @pl.when(pl.program_id(1) == 0)
    def _():
        bufa[0:HIST, :] = jnp.zeros((HIST, w), F32)
        bufc[0:HIST, :] = jnp.zeros((HIST, SSM_CONV_DIM), F32)
        hst[...] = jnp.zeros_like(hst)

    ua = ua_ref[...].astype(F32)
    bufa[HIST:, :] = ua[:, 2 * w:] * ua[:, :w]
    conv = cwa_ref[0:1, :] * bufa[pl.ds(HIST - SC_K + 1, t), :]
    for kk in range(1, SC_K):
        conv = conv + cwa_ref[kk:kk + 1, :] * bufa[pl.ds(HIST - SC_K + 1 + kk, t), :]
    ya_ref[...] = (ua[:, w:2 * w] * conv).astype(ya_ref.dtype)
    bufa[0:HIST, :] = bufa[t:t + HIST, :]

    bufc[HIST:, :] = xbc_ref[...].astype(F32)
    xc = cbias_ref[...] + cwc_ref[0:1, :] * bufc[pl.ds(HIST - SSM_CONV_K + 1, t), :]
    for kk in range(1, SSM_CONV_K):
        xc = xc + cwc_ref[kk:kk + 1, :] * bufc[pl.ds(HIST - SSM_CONV_K + 1 + kk, t), :]
    bufc[0:HIST, :] = bufc[t:t + HIST, :]
    xc = xc * _sigmoid(xc)
    xs = xc[:, :w]
    b_in = xc[:, w:w + LANES]
    c_in = xc[:, w + LANES:]

    dtp = dt_ref[...] + dtb_ref[...]
    dt = jnp.maximum(dtp, 0.0) + jnp.log(1.0 + jnp.exp(-jnp.abs(dtp)))
    a = dt * (-jnp.exp(alog_ref[...]))
    row = lax.broadcasted_iota(jnp.int32, (t, t), 0)
    col = lax.broadcasted_iota(jnp.int32, (t, t), 1)
    causal = row >= col
    tri = jnp.where(causal, 1.0, 0.0).astype(BF16)
    a_hi, a_mid, a_lo = _split3(a)
    acs = (jnp.dot(tri, a_hi, preferred_element_type=F32)
           + jnp.dot(tri, a_mid, preferred_element_type=F32)
           + jnp.dot(tri, a_lo, preferred_element_type=F32))
    acs_t = acs.T
    acs_x = _per_head(acs, HEAD_DIM, w)
    last_x = acs_x[t - 1:t, :]
    x_dt = xs * _per_head(dt, HEAD_DIM, w)
    to_end_x = jnp.exp(last_x - acs_x)
    from_start_x = jnp.exp(acs_x)
    chunk_decay_x = jnp.exp(last_x)

    lane = lax.broadcasted_iota(jnp.int32, (1, LANES), 1)
    low = lane < HEAD_DIM
    c_bf = c_in.astype(BF16)
    nt = (((1,), (1,)), ((), ()))
    for g in range(SSM_GROUPS):
        gmask = low if g == 0 else jnp.logical_not(low)
        sl = slice(g * LANES, (g + 1) * LANES)
        b_g = jnp.where(gmask, b_in, 0.0)
        cb = lax.dot_general(c_bf, b_g.astype(BF16), nt, preferred_element_type=F32)
        xg = x_dt[:, sl]
        y = jnp.zeros((t, LANES), F32)
        for e in range(2):
            h = 2 * g + e
            seg = acs[:, h:h + 1] - acs_t[h:h + 1, :]
            decay = jnp.exp(jnp.where(causal, seg, -jnp.inf))
            emask = low if e == 0 else jnp.logical_not(low)
            xe = jnp.where(emask, xg, 0.0).astype(BF16)
            y = y + jnp.dot((cb * decay).astype(BF16), xe, preferred_element_type=F32)
        h_enter = hst[g]
        y = y + jnp.dot(c_bf, h_enter.astype(BF16), preferred_element_type=F32) * from_start_x[:, sl]
        state = jnp.dot(b_g.T.astype(BF16), (xg * to_end_x[:, sl]).astype(BF16),
                        preferred_element_type=F32)
        hst[g] = h_enter * chunk_decay_x[:, sl] + state

        y = y + xs[:, sl] * dskip_ref[:, sl]
        zg = z_ref[:, sl].astype(F32)
        gated = y * (zg * _sigmoid(zg))
        ms = jnp.mean(gated * gated, axis=-1, keepdims=True)
        yc_ref[:, sl] = (gated * lax.rsqrt(ms + RMS_EPS) * ng_ref[:, sl]).astype(yc_ref.dtype)


def _ssd(u, dt, cwa, cwc, cbias, dtb, alog, dskip, ng, bsz, seq):
    t = SSM_CHUNK
    nc = seq // t
    w = BRANCH_W
    small = lambda shape: pl.BlockSpec(shape, lambda b, c: (0, 0))
    return pl.pallas_call(
        _ssd_kernel,
        grid=(bsz, nc),
        in_specs=[
            pl.BlockSpec((t, 3 * w), lambda b, c: (b * nc + c, COL_A)),
            pl.BlockSpec((t, w), lambda b, c: (b * nc + c, COL_Z // 2)),
            pl.BlockSpec((t, SSM_CONV_DIM), lambda b, c: (b * nc + c, COL_XBC * LANES // SSM_CONV_DIM)),
            pl.BlockSpec((t, LANES), lambda b, c: (b * nc + c, 0)),
            small((SC_K, w)), small((SSM_CONV_K, SSM_CONV_DIM)), small((1, SSM_CONV_DIM)),
            small((1, LANES)), small((1, LANES)), small((1, w)), small((1, w)),
        ],
        out_specs=[pl.BlockSpec((t, w), lambda b, c: (b * nc + c, 0)),
                   pl.BlockSpec((t, w), lambda b, c: (b * nc + c, 0))],
        out_shape=[jax.ShapeDtypeStruct((bsz * seq, w), BF16),
                   jax.ShapeDtypeStruct((bsz * seq, w), BF16)],
        scratch_shapes=[
            pltpu.VMEM((t + HIST, w), F32),
            pltpu.VMEM((t + HIST, SSM_CONV_DIM), F32),
            pltpu.VMEM((SSM_GROUPS, LANES, LANES), F32),
        ],
        compiler_params=_cparams(("parallel", "arbitrary")),
        name="conv_ssd",
    )(u, u, u, dt, cwa, cwc, cbias, dtb, alog, dskip, ng)


def _merge_kernel(x_ref, ga_ref, gb_ref, gc_ref, gd_ref, ya_ref, yb_ref, yc_ref, yd_ref,
                  wb_ref, wo_ref, o_ref):
    merged = None
    branches = ((ga_ref, ya_ref), (gb_ref, yb_ref), (gc_ref, yc_ref), (gd_ref, yd_ref))
    for i, (g_ref, y_ref) in enumerate(branches):
        gate = _sigmoid(g_ref[...].astype(F32))
        term = gate * jnp.dot(y_ref[...], wb_ref[i], preferred_element_type=F32)
        merged = term if merged is None else merged + term
    o_ref[...] = x_ref[...] + jnp.dot(merged.astype(BF16), wo_ref[...], preferred_element_type=F32)


def _merge(x, u, ya, yb, yc, yd, wb, wo, tm):
    m = x.shape[0]
    ybr = pl.BlockSpec((tm, BRANCH_W), lambda i: (i, 0))
    g0 = COL_G * LANES // D_MODEL
    gates = [pl.BlockSpec((tm, D_MODEL), functools.partial(lambda i, c: (i, c), c=g0 + br))
             for br in range(N_BRANCH)]
    return pl.pallas_call(
        _merge_kernel,
        grid=(m // tm,),
        in_specs=[
            pl.BlockSpec((tm, D_MODEL), lambda i: (i, 0)),
            *gates,
            ybr, ybr, ybr, ybr,
            pl.BlockSpec((N_BRANCH, BRANCH_W, D_MODEL), lambda i: (0, 0, 0)),
            pl.BlockSpec((D_MODEL, D_MODEL), lambda i: (0, 0)),
        ],
        out_specs=pl.BlockSpec((tm, D_MODEL), lambda i: (i, 0)),
        out_shape=jax.ShapeDtypeStruct((m, D_MODEL), F32),
        compiler_params=_cparams(("parallel",)),
        name="merge",
    )(x, u, u, u, u, ya, yb, yc, yd, wb, wo)


def _ffn_kernel(x_ref, g_ref, wgu_ref, wd_ref, fg_ref, o_ref, h_ref, act_ref, *, th, final_norm):
    x = x_ref[...]
    ms = jnp.mean(x * x, axis=-1, keepdims=True)
    h_ref[...] = (x * lax.rsqrt(ms + RMS_EPS) * g_ref[...]).astype(BF16)
    for c in range(FFN_HIDDEN // th):
        h = h_ref[...]
        gate = jnp.dot(h, wgu_ref[:, c * th:(c + 1) * th], preferred_element_type=F32)
        up = jnp.dot(h, wgu_ref[:, FFN_HIDDEN + c * th:FFN_HIDDEN + (c + 1) * th],
                     preferred_element_type=F32)
        act_ref[:, c * th:(c + 1) * th] = ((gate * _sigmoid(gate)) * up).astype(BF16)
    y = x_ref[...] + jnp.dot(act_ref[...], wd_ref[...], preferred_element_type=F32)
    if final_norm:
        ms = jnp.mean(y * y, axis=-1, keepdims=True)
        y = y * lax.rsqrt(ms + RMS_EPS) * fg_ref[...]
    o_ref[...] = y


def _ffn(x, g, wgu, wd, final_g, tm, th, final_norm):
    m = x.shape[0]
    resident = pl.Buffered(1)
    return pl.pallas_call(
        functools.partial(_ffn_kernel, th=th, final_norm=final_norm),
        grid=(m // tm,),
        in_specs=[
            pl.BlockSpec((tm, D_MODEL), lambda i: (i, 0)),
            pl.BlockSpec((1, D_MODEL), lambda i: (0, 0)),
            pl.BlockSpec((D_MODEL, 2 * FFN_HIDDEN), lambda i: (0, 0), pipeline_mode=resident),
            pl.BlockSpec((FFN_HIDDEN, D_MODEL), lambda i: (0, 0), pipeline_mode=resident),
            pl.BlockSpec((1, D_MODEL), lambda i: (0, 0)),
        ],
        out_specs=pl.BlockSpec((tm, D_MODEL), lambda i: (i, 0)),
        out_shape=jax.ShapeDtypeStruct((m, D_MODEL), F32),
        scratch_shapes=[pltpu.VMEM((tm, D_MODEL), BF16), pltpu.VMEM((tm, FFN_HIDDEN), BF16)],
        compiler_params=_cparams(("parallel",)),
        name="ffn",
    )(x, g, wgu, wd, final_g)


def _row_tile(m, want):
    while m % want:
        want //= 2
    return want


def _pad_lanes(v):
    return jnp.pad(v.astype(F32), (0, LANES - v.shape[0]))[None, :]


def kernel(x, norm1_g, w_in, conv_a_w, ssm_conv_w, ssm_conv_b, ssm_dt_bias, ssm_a_log, ssm_d,
           ssm_norm_g, w_branch, w_o, norm2_g, w_gate_up, w_down, final_g):
    bsz, seq, _ = x.shape
    assert seq % MOBA_BLOCK == 0 and seq % SSM_CHUNK == 0
    depth = w_in.shape[0]
    assert depth >= 1
    m = bsz * seq
    xbc_col = 3 * BRANCH_W + 3 * BRANCH_W + BRANCH_W
    dt_col = xbc_col + SSM_CONV_DIM
    d_col = dt_col + N_HEADS
    h = x.reshape(m, D_MODEL)
    tm_big = _row_tile(m, 1024)
    tm_mid = _row_tile(m, 512)
    for l in range(depth):
        w_main = jnp.concatenate([w_in[l, :, :xbc_col], w_in[l, :, d_col:d_col + BRANCH_W],
                                  w_in[l, :, xbc_col:dt_col], w_in[l, :, d_col + BRANCH_W:]],
                                 axis=1).astype(BF16)
        w_dt = jnp.pad(w_in[l, :, dt_col:dt_col + N_HEADS], ((0, 0), (0, LANES - N_HEADS))).astype(BF16)
        u, dt = _inproj(h, norm1_g[l][None, :], w_main, w_dt, tm_mid, 1024)
        y_b = _sb_attention(u, bsz, seq, 256)
        y_d = _moba_attention(u, bsz, seq)
        y_a, y_c = _ssd(u, dt, conv_a_w[l], ssm_conv_w[l], ssm_conv_b[l][None, :],
                        _pad_lanes(ssm_dt_bias[l]), _pad_lanes(ssm_a_log[l]),
                        jnp.repeat(ssm_d[l], HEAD_DIM)[None, :], ssm_norm_g[l][None, :], bsz, seq)
        h = _merge(h, u, y_a, y_b, y_c, y_d, w_branch[l].astype(BF16), w_o[l].astype(BF16), tm_mid)
        h = _ffn(h, norm2_g[l][None, :], w_gate_up[l].astype(BF16), w_down[l].astype(BF16),
                 final_g[None, :], tm_mid, 256, final_norm=(l == depth - 1))
    return h.reshape(bsz, seq, D_MODEL)
```

```python
import functools

import jax
import jax.numpy as jnp
from jax import lax
from jax.experimental import pallas as pl
from jax.experimental.pallas import tpu as pltpu

F32 = jnp.float32
BF16 = jnp.bfloat16

D_MODEL = 1024
HEAD_DIM = 64
BRANCH_W = 256
N_BRANCH = 4
N_HEADS = 4
SC_K = 3
SSM_GROUPS = 2
SSM_STATE = 64
SSM_CONV_K = 4
SSM_CHUNK = 256
SSM_CONV_DIM = BRANCH_W + 2 * SSM_GROUPS * SSM_STATE
MOBA_BLOCK = 256
MOBA_TOPK = 3
FFN_HIDDEN = 2816
RMS_EPS = 1e-6

LANES = 128
HIST = 8
NEG = -1e30

N_PACK = 7168
COL_A = 0
COL_SB_Q, COL_SB_K, COL_SB_V = 6, 8, 10
COL_Z = 12
COL_MO_Q = 14
COL_XBC = 16
COL_MO_K, COL_MO_V = 20, 22
COL_G = 24

VMEM_LIMIT = 56 * 1024 * 1024

def _cparams(sem):
    return pltpu.CompilerParams(dimension_semantics=sem, vmem_limit_bytes=VMEM_LIMIT)


def _sigmoid(x):
    return 1.0 / (1.0 + jnp.exp(-x))


def _inproj_kernel(x_ref, g_ref, w_ref, wdt_ref, u_ref, dt_ref, h_ref, *, tn):
    x = x_ref[...]
    ms = jnp.mean(x * x, axis=-1, keepdims=True)
    h_ref[...] = (x * lax.rsqrt(ms + RMS_EPS) * g_ref[...]).astype(BF16)
    dt_ref[...] = jnp.dot(h_ref[...], wdt_ref[...], preferred_element_type=F32)
    for c in range(N_PACK // tn):
        sl = slice(c * tn, (c + 1) * tn)
        u_ref[:, sl] = jnp.dot(h_ref[...], w_ref[:, sl], preferred_element_type=F32).astype(BF16)


def _inproj(x, g, w, wdt, tm, tn):
    m = x.shape[0]
    resident = pl.Buffered(1)
    return pl.pallas_call(
        functools.partial(_inproj_kernel, tn=tn),
        grid=(m // tm,),
        in_specs=[
            pl.BlockSpec((tm, D_MODEL), lambda i: (i, 0)),
            pl.BlockSpec((1, D_MODEL), lambda i: (0, 0)),
            pl.BlockSpec((D_MODEL, N_PACK), lambda i: (0, 0), pipeline_mode=resident),
            pl.BlockSpec((D_MODEL, LANES), lambda i: (0, 0)),
        ],
        out_specs=[
            pl.BlockSpec((tm, N_PACK), lambda i: (i, 0)),
            pl.BlockSpec((tm, LANES), lambda i: (i, 0)),
        ],
        out_shape=[jax.ShapeDtypeStruct((m, N_PACK), BF16),
                   jax.ShapeDtypeStruct((m, LANES), F32)],
        scratch_shapes=[pltpu.VMEM((tm, D_MODEL), BF16)],
        compiler_params=_cparams(("parallel",)),
        name="inproj",
    )(x, g, w, wdt)


def _neg_abs(x):
    return pltpu.bitcast(pltpu.bitcast(x, jnp.uint32) | jnp.uint32(0x80000000), F32)


def _sb_kernel(q_ref, k_ref, v_ref, o_ref, vt_ref, z_ref, sp_ref, d_ref, w_ref, r_ref, acc_ref,
               *, t, nt):
    qi = pl.program_id(2)

    @pl.when(qi == 0)
    def _():
        head0_rows = lax.broadcasted_iota(jnp.int32, (LANES, t), 0) < HEAD_DIM

        def tr(c, carry):
            vt = v_ref[pl.ds(pl.multiple_of(c * t, t), t), :].astype(F32).T
            vt_ref[0, c] = jnp.where(head0_rows, vt, 0.0).astype(BF16)
            vt_ref[1, c] = jnp.where(head0_rows, 0.0, vt).astype(BF16)
            return carry

        lax.fori_loop(0, nt, tr, 0)

    q_t = (q_ref[...].astype(F32) * (HEAD_DIM ** -0.5)).T
    head0_rows = lax.broadcasted_iota(jnp.int32, (LANES, t), 0) < HEAD_DIM
    qt = (jnp.where(head0_rows, q_t, 0.0).astype(BF16), jnp.where(head0_rows, 0.0, q_t).astype(BF16))
    row = lax.broadcasted_iota(jnp.int32, (t, t), 0)
    col = lax.broadcasted_iota(jnp.int32, (t, t), 1)
    later_t = jnp.where(col > row, 1.0, 0.0).astype(BF16)
    n_tiles = qi + 1

    def tile_of(p):
        return jnp.maximum(qi - p, 0)

    def score_mm(p, par):
        k = k_ref[pl.ds(pl.multiple_of(tile_of(p) * t, t), t), :]
        for h in range(2):
            z_ref[par, h] = jnp.dot(k, qt[h], preferred_element_type=F32)

    def softplus(par, masked):
        for h in range(2):
            z = z_ref[par, h]
            if masked:
                z = jnp.where(row < col, z, NEG)
            sp = jnp.maximum(z, 0.0) + jnp.log(1.0 + jnp.exp(_neg_abs(z)))
            sp_ref[par, h] = sp.astype(BF16)
            d_ref[par, h] = z - sp

    def weights(par):
        for h in range(2):
            sp = sp_ref[par, h]
            between = jnp.dot(later_t, sp, preferred_element_type=F32)
            r = r_ref[h]
            w_ref[par, h] = jnp.exp(d_ref[par, h] - between - r).astype(BF16)
            r_ref[h] = r + between[0:1, :] + sp[0:1, :].astype(F32)

    def value_mm(p, par):
        tile = tile_of(p)
        acc_ref[...] += (jnp.dot(vt_ref[0, tile], w_ref[par, 0], preferred_element_type=F32)
                         + jnp.dot(vt_ref[1, tile], w_ref[par, 1], preferred_element_type=F32))

    def step(s, par):
        value_mm(s - 3, 1 - par)
        weights(par)
        softplus(1 - par, False)
        score_mm(s, par)

    acc_ref[...] = jnp.zeros_like(acc_ref)
    r_ref[...] = jnp.zeros_like(r_ref)
    score_mm(0, 0)
    softplus(0, True)
    score_mm(1, 1)
    weights(0)
    softplus(1, False)
    score_mm(2, 0)

    n_full = jnp.maximum(n_tiles - 3, 0)

    def body(i, carry):
        s = 3 + 2 * i
        step(s, 1)
        step(s + 1, 0)
        return carry

    lax.fori_loop(0, n_full // 2, body, 0)

    @pl.when(n_full % 2 == 1)
    def _():
        step(n_tiles - 1, 1)

    for par_t in range(2):
        @pl.when(n_tiles % 2 == par_t)
        def _():
            @pl.when(n_tiles >= 3)
            def _():
                value_mm(n_tiles - 3, 1 - par_t)
                weights(par_t)
                softplus(1 - par_t, False)

            @pl.when(n_tiles >= 2)
            def _():
                value_mm(n_tiles - 2, par_t)
                weights(1 - par_t)

            value_mm(n_tiles - 1, 1 - par_t)

    o_ref[...] = acc_ref[...].T.astype(o_ref.dtype)


def _sb_attention(u, bsz, seq, t):
    nq = seq // t
    tile_f32 = pltpu.VMEM((2, 2, t, t), F32)
    tile_bf16 = pltpu.VMEM((2, 2, t, t), BF16)
    return pl.pallas_call(
        functools.partial(_sb_kernel, t=t, nt=nq),
        grid=(bsz, N_HEADS // 2, nq),
        in_specs=[
            pl.BlockSpec((t, LANES), lambda b, hp, qi: (b * nq + qi, COL_SB_Q + hp)),
            pl.BlockSpec((seq, LANES), lambda b, hp, qi: (b, COL_SB_K + hp)),
            pl.BlockSpec((seq, LANES), lambda b, hp, qi: (b, COL_SB_V + hp)),
        ],
        out_specs=pl.BlockSpec((t, LANES), lambda b, hp, qi: (b * nq + qi, hp)),
        out_shape=jax.ShapeDtypeStruct((bsz * seq, BRANCH_W), BF16),
        scratch_shapes=[
            pltpu.VMEM((2, nq, LANES, t), BF16),
            tile_f32,
            tile_bf16,
            tile_f32,
            tile_bf16,
            pltpu.VMEM((2, 1, t), F32),
            pltpu.VMEM((LANES, t), F32),
        ],
        compiler_params=_cparams(("parallel", "parallel", "arbitrary")),
        name="sb_attention",
    )(u, u, u)


def _moba_kernel(q_ref, k_ref, v_ref, o_ref, vt_ref, km_ref, sel_ref, z_ref, p_ref, alpha_ref, m_ref,
                 acc_ref, *, nb, ke):
    t = MOBA_BLOCK
    qi = pl.program_id(2)
    n_tiles = qi + 1
    ones_row = (HEAD_DIM, 0)

    @pl.when(qi == 0)
    def _():
        rows = lax.broadcasted_iota(jnp.int32, (LANES, t), 0)

        def tr(c, carry):
            kv = pl.ds(pl.multiple_of(c * t, t), t)
            vt = v_ref[kv, :].astype(F32).T
            vt_ref[0, c] = jnp.where(rows < HEAD_DIM, vt,
                                     jnp.where(rows == ones_row[0], 1.0, 0.0)).astype(BF16)
            vt_ref[1, c] = jnp.where(rows >= HEAD_DIM, vt,
                                     jnp.where(rows == ones_row[1], 1.0, 0.0)).astype(BF16)
            km_ref[pl.ds(c, 1), :] = jnp.sum(k_ref[kv, :].astype(F32), axis=0, keepdims=True) * (1.0 / t)
            return carry

        km_ref[...] = jnp.zeros_like(km_ref)
        lax.fori_loop(0, nb, tr, 0)

    q_t = q_ref[...].astype(F32).T
    head0_rows = lax.broadcasted_iota(jnp.int32, (LANES, t), 0) < HEAD_DIM
    qt = (jnp.where(head0_rows, q_t, 0.0).astype(BF16), jnp.where(head0_rows, 0.0, q_t).astype(BF16))
    qst = tuple(x * (HEAD_DIM ** -0.5) for x in qt)

    km = km_ref[...]
    km_hi = km.astype(BF16)
    km_lo = (km - km_hi.astype(F32)).astype(BF16)
    blk = lax.broadcasted_iota(jnp.int32, (LANES, t), 0).astype(F32)
    qif = qi.astype(F32)
    for h in range(2):
        gate = (jnp.dot(km_hi, qt[h], preferred_element_type=F32)
                + jnp.dot(km_lo, qt[h], preferred_element_type=F32))
        g = jnp.where(blk < qif, gate, -jnp.inf)
        sel = jnp.zeros((LANES, t), F32)
        for r in range(ke):
            mx = jnp.max(g, axis=0, keepdims=True)
            idx = jnp.min(jnp.where(g == mx, blk, 1e9), axis=0, keepdims=True)
            hit = blk == idx
            sel = jnp.where(jnp.logical_and(hit, qif > r), 1.0, sel)
            g = jnp.where(hit, -jnp.inf, g)
        sel_ref[h] = sel

    row = lax.broadcasted_iota(jnp.int32, (t, t), 0)
    col = lax.broadcasted_iota(jnp.int32, (t, t), 1)

    def tile_of(p):
        return jnp.maximum(qi - p, 0)

    def score_mm(p, par):
        k = k_ref[pl.ds(pl.multiple_of(tile_of(p) * t, t), t), :]
        for h in range(2):
            z_ref[par, h] = jnp.dot(k, qst[h], preferred_element_type=F32)

    def softmax(p, par, own):
        for h in range(2):
            s = z_ref[par, h]
            if own:
                s = jnp.where(row <= col, s, -jnp.inf)
                m_old = jnp.full((1, t), -jnp.inf, F32)
            else:
                chosen = sel_ref[h, pl.ds(tile_of(p), 1), :]
                s = s + jnp.where(chosen > 0.5, 0.0, -jnp.inf)
                m_old = m_ref[h]
            m_new = jnp.maximum(m_old, jnp.max(s, axis=0, keepdims=True))
            alpha_ref[par, h] = jnp.zeros((1, t), F32) if own else jnp.exp(m_old - m_new)
            p_ref[par, h] = jnp.exp(s - m_new).astype(BF16)
            m_ref[h] = m_new

    def value_mm(p, par):
        tile = tile_of(p)
        for h in range(2):
            acc_ref[h] = (acc_ref[h] * alpha_ref[par, h]
                          + jnp.dot(vt_ref[h, tile], p_ref[par, h], preferred_element_type=F32))

    def step(s, par):
        value_mm(s - 2, par)
        softmax(s - 1, 1 - par, False)
        score_mm(s, par)

    acc_ref[...] = jnp.zeros_like(acc_ref)
    score_mm(0, 0)
    softmax(0, 0, True)
    score_mm(1, 1)

    n_full = jnp.maximum(n_tiles - 2, 0)

    def body(i, carry):
        s = 2 + 2 * i
        step(s, 0)
        step(s + 1, 1)
        return carry

    lax.fori_loop(0, n_full // 2, body, 0)

    @pl.when(n_full % 2 == 1)
    def _():
        step(n_tiles - 1, 0)

    for par_t in range(2):
        @pl.when(n_tiles % 2 == par_t)
        def _():
            @pl.when(n_tiles >= 2)
            def _():
                value_mm(n_tiles - 2, par_t)
                softmax(n_tiles - 1, 1 - par_t, False)

            value_mm(n_tiles - 1, 1 - par_t)

    rows = lax.broadcasted_iota(jnp.int32, (LANES, t), 0)
    a0 = acc_ref[0]
    a1 = acc_ref[1]
    out_t = jnp.where(rows < HEAD_DIM, a0 / a0[ones_row[0]:ones_row[0] + 1, :],
                      a1 / a1[ones_row[1]:ones_row[1] + 1, :])
    o_ref[...] = out_t.T.astype(o_ref.dtype)


def _moba_attention(u, bsz, seq):
    t = MOBA_BLOCK
    nb = seq // t
    assert nb <= LANES
    ke = max(1, min(MOBA_TOPK, nb - 1))
    return pl.pallas_call(
        functools.partial(_moba_kernel, nb=nb, ke=ke),
        grid=(bsz, N_HEADS // 2, nb),
        in_specs=[
            pl.BlockSpec((t, LANES), lambda b, hp, qi: (b * nb + qi, COL_MO_Q + hp)),
            pl.BlockSpec((seq, LANES), lambda b, hp, qi: (b, COL_MO_K + hp)),
            pl.BlockSpec((seq, LANES), lambda b, hp, qi: (b, COL_MO_V + hp)),
        ],
        out_specs=pl.BlockSpec((t, LANES), lambda b, hp, qi: (b * nb + qi, hp)),
        out_shape=jax.ShapeDtypeStruct((bsz * seq, BRANCH_W), BF16),
        scratch_shapes=[
            pltpu.VMEM((2, nb, LANES, t), BF16),
            pltpu.VMEM((LANES, LANES), F32),
            pltpu.VMEM((2, LANES, t), F32),
            pltpu.VMEM((2, 2, t, t), F32),
            pltpu.VMEM((2, 2, t, t), BF16),
            pltpu.VMEM((2, 2, 1, t), F32),
            pltpu.VMEM((2, 1, t), F32),
            pltpu.VMEM((2, LANES, t), F32),
        ],
        compiler_params=_cparams(("parallel", "parallel", "arbitrary")),
        name="moba_attention",
    )(u, u, u)


def _per_head(v, lanes_per_head, width):
    head = lax.broadcasted_iota(jnp.int32, (1, width), 1) // lanes_per_head
    out = jnp.zeros((v.shape[0], width), F32)
    for h in range(N_HEADS):
        out = jnp.where(head == h, v[:, h:h + 1], out)
    return out


def _split3(a):
    hi = a.astype(BF16)
    r1 = a - hi.astype(F32)
    mid = r1.astype(BF16)
    lo = (r1 - mid.astype(F32)).astype(BF16)
    return hi, mid, lo


def _ssd_kernel(ua_ref, z_ref, xbc_ref, dt_ref, cwa_ref, cwc_ref, cbias_ref, dtb_ref, alog_ref,
                dskip_ref, ng_ref, ya_ref, yc_ref, bufa, bufc, hst):
    t = SSM_CHUNK
    w = BRANCH_W

    @pl.when(pl.program_id(1) == 0)
    def _():
        bufa[0:HIST, :] = jnp.zeros((HIST, w), F32)
        bufc[0:HIST, :] = jnp.zeros((HIST, SSM_CONV_DIM), F32)
        hst[...] = jnp.zeros_like(hst)

    ua = ua_ref[...].astype(F32)
    bufa[HIST:, :] = ua[:, 2 * w:] * ua[:, :w]
    conv = cwa_ref[0:1, :] * bufa[pl.ds(HIST - SC_K + 1, t), :]
    for kk in range(1, SC_K):
        conv = conv + cwa_ref[kk:kk + 1, :] * bufa[pl.ds(HIST - SC_K + 1 + kk, t), :]
    ya_ref[...] = (ua[:, w:2 * w] * conv).astype(ya_ref.dtype)
    bufa[0:HIST, :] = bufa[t:t + HIST, :]

    bufc[HIST:, :] = xbc_ref[...].astype(F32)
    xc = cbias_ref[...] + cwc_ref[0:1, :] * bufc[pl.ds(HIST - SSM_CONV_K + 1, t), :]
    for kk in range(1, SSM_CONV_K):
        xc = xc + cwc_ref[kk:kk + 1, :] * bufc[pl.ds(HIST - SSM_CONV_K + 1 + kk, t), :]
    bufc[0:HIST, :] = bufc[t:t + HIST, :]
    xc = xc * _sigmoid(xc)
    xs = xc[:, :w]
    b_in = xc[:, w:w + LANES]
    c_in = xc[:, w + LANES:]

    dtp = dt_ref[...] + dtb_ref[...]
    dt = jnp.maximum(dtp, 0.0) + jnp.log(1.0 + jnp.exp(-jnp.abs(dtp)))
    a = dt * (-jnp.exp(alog_ref[...]))
    row = lax.broadcasted_iota(jnp.int32, (t, t), 0)
    col = lax.broadcasted_iota(jnp.int32, (t, t), 1)
    causal = row >= col
    tri = jnp.where(causal, 1.0, 0.0).astype(BF16)
    a_hi, a_mid, a_lo = _split3(a)
    acs = (jnp.dot(tri, a_hi, preferred_element_type=F32)
           + jnp.dot(tri, a_mid, preferred_element_type=F32)
           + jnp.dot(tri, a_lo, preferred_element_type=F32))
    acs_t = acs.T
    acs_x = _per_head(acs, HEAD_DIM, w)
    last_x = acs_x[t - 1:t, :]
    x_dt = xs * _per_head(dt, HEAD_DIM, w)
    to_end_x = jnp.exp(last_x - acs_x)
    from_start_x = jnp.exp(acs_x)
    chunk_decay_x = jnp.exp(last_x)

    lane = lax.broadcasted_iota(jnp.int32, (1, LANES), 1)
    low = lane < HEAD_DIM
    c_bf = c_in.astype(BF16)
    nt = (((1,), (1,)), ((), ()))
    for g in range(SSM_GROUPS):
        gmask = low if g == 0 else jnp.logical_not(low)
        sl = slice(g * LANES, (g + 1) * LANES)
        b_g = jnp.where(gmask, b_in, 0.0)
        cb = lax.dot_general(c_bf, b_g.astype(BF16), nt, preferred_element_type=F32)
        xg = x_dt[:, sl]
        y = jnp.zeros((t, LANES), F32)
        for e in range(2):
            h = 2 * g + e
            seg = acs[:, h:h + 1] - acs_t[h:h + 1, :]
            decay = jnp.exp(jnp.where(causal, seg, -jnp.inf))
            emask = low if e == 0 else jnp.logical_not(low)
            xe = jnp.where(emask, xg, 0.0).astype(BF16)
            y = y + jnp.dot((cb * decay).astype(BF16), xe, preferred_element_type=F32)
        h_enter = hst[g]
        y = y + jnp.dot(c_bf, h_enter.astype(BF16), preferred_element_type=F32) * from_start_x[:, sl]
        state = jnp.dot(b_g.T.astype(BF16), (xg * to_end_x[:, sl]).astype(BF16),
                        preferred_element_type=F32)
        hst[g] = h_enter * chunk_decay_x[:, sl] + state

        y = y + xs[:, sl] * dskip_ref[:, sl]
        zg = z_ref[:, sl].astype(F32)
        gated = y * (zg * _sigmoid(zg))
        ms = jnp.mean(gated * gated, axis=-1, keepdims=True)
        yc_ref[:, sl] = (gated * lax.rsqrt(ms + RMS_EPS) * ng_ref[:, sl]).astype(yc_ref.dtype)


def _ssd(u, dt, cwa, cwc, cbias, dtb, alog, dskip, ng, bsz, seq):
    t = SSM_CHUNK
    nc = seq // t
    w = BRANCH_W
    small = lambda shape: pl.BlockSpec(shape, lambda b, c: (0, 0))
    return pl.pallas_call(
        _ssd_kernel,
        grid=(bsz, nc),
        in_specs=[
            pl.BlockSpec((t, 3 * w), lambda b, c: (b * nc + c, COL_A)),
            pl.BlockSpec((t, w), lambda b, c: (b * nc + c, COL_Z // 2)),
            pl.BlockSpec((t, SSM_CONV_DIM), lambda b, c: (b * nc + c, COL_XBC * LANES // SSM_CONV_DIM)),
            pl.BlockSpec((t, LANES), lambda b, c: (b * nc + c, 0)),
            small((SC_K, w)), small((SSM_CONV_K, SSM_CONV_DIM)), small((1, SSM_CONV_DIM)),
            small((1, LANES)), small((1, LANES)), small((1, w)), small((1, w)),
        ],
        out_specs=[pl.BlockSpec((t, w), lambda b, c: (b * nc + c, 0)),
                   pl.BlockSpec((t, w), lambda b, c: (b * nc + c, 0))],
        out_shape=[jax.ShapeDtypeStruct((bsz * seq, w), BF16),
                   jax.ShapeDtypeStruct((bsz * seq, w), BF16)],
        scratch_shapes=[
            pltpu.VMEM((t + HIST, w), F32),
            pltpu.VMEM((t + HIST, SSM_CONV_DIM), F32),
            pltpu.VMEM((SSM_GROUPS, LANES, LANES), F32),
        ],
        compiler_params=_cparams(("parallel", "arbitrary")),
        name="conv_ssd",
    )(u, u, u, dt, cwa, cwc, cbias, dtb, alog, dskip, ng)


def _merge_kernel(x_ref, ga_ref, gb_ref, gc_ref, gd_ref, ya_ref, yb_ref, yc_ref, yd_ref,
                  wb_ref, wo_ref, o_ref):
    merged = None
    branches = ((ga_ref, ya_ref), (gb_ref, yb_ref), (gc_ref, yc_ref), (gd_ref, yd_ref))
    for i, (g_ref, y_ref) in enumerate(branches):
        twice_gate = 1.0 + jnp.tanh(0.5 * g_ref[...].astype(F32))
        term = twice_gate * jnp.dot(y_ref[...], wb_ref[i], preferred_element_type=F32)
        merged = term if merged is None else merged + term
    merged = (0.5 * merged).astype(BF16)
    o_ref[...] = x_ref[...] + jnp.dot(merged, wo_ref[...], preferred_element_type=F32)


def _merge(x, u, ya, yb, yc, yd, wb, wo, tm):
    m = x.shape[0]
    ybr = pl.BlockSpec((tm, BRANCH_W), lambda i: (i, 0))
    g0 = COL_G * LANES // D_MODEL
    gates = [pl.BlockSpec((tm, D_MODEL), functools.partial(lambda i, c: (i, c), c=g0 + br))
             for br in range(N_BRANCH)]
    return pl.pallas_call(
        _merge_kernel,
        grid=(m // tm,),
        in_specs=[
            pl.BlockSpec((tm, D_MODEL), lambda i: (i, 0)),
            *gates,
            ybr, ybr, ybr, ybr,
            pl.BlockSpec((N_BRANCH, BRANCH_W, D_MODEL), lambda i: (0, 0, 0)),
            pl.BlockSpec((D_MODEL, D_MODEL), lambda i: (0, 0)),
        ],
        out_specs=pl.BlockSpec((tm, D_MODEL), lambda i: (i, 0)),
        out_shape=jax.ShapeDtypeStruct((m, D_MODEL), F32),
        compiler_params=_cparams(("parallel",)),
        name="merge",
    )(x, u, u, u, u, ya, yb, yc, yd, wb, wo)


def _ffn_kernel(x_ref, g_ref, wgu_ref, wd_ref, fg_ref, o_ref, h_ref, act_ref, *, th, final_norm):
    x = x_ref[...]
    ms = jnp.mean(x * x, axis=-1, keepdims=True)
    h_ref[...] = (x * lax.rsqrt(ms + RMS_EPS) * g_ref[...]).astype(BF16)
    for c in range(FFN_HIDDEN // th):
        h = h_ref[...]
        gate = jnp.dot(h, wgu_ref[:, c * th:(c + 1) * th], preferred_element_type=F32)
        up = jnp.dot(h, wgu_ref[:, FFN_HIDDEN + c * th:FFN_HIDDEN + (c + 1) * th],
                     preferred_element_type=F32)
        act_ref[:, c * th:(c + 1) * th] = ((gate * _sigmoid(gate)) * up).astype(BF16)
    y = x_ref[...] + jnp.dot(act_ref[...], wd_ref[...], preferred_element_type=F32)
    if final_norm:
        ms = jnp.mean(y * y, axis=-1, keepdims=True)
        y = y * lax.rsqrt(ms + RMS_EPS) * fg_ref[...]
    o_ref[...] = y


def _ffn(x, g, wgu, wd, final_g, tm, th, final_norm):
    m = x.shape[0]
    resident = pl.Buffered(1)
    return pl.pallas_call(
        functools.partial(_ffn_kernel, th=th, final_norm=final_norm),
        grid=(m // tm,),
        in_specs=[
            pl.BlockSpec((tm, D_MODEL), lambda i: (i, 0)),
            pl.BlockSpec((1, D_MODEL), lambda i: (0, 0)),
            pl.BlockSpec((D_MODEL, 2 * FFN_HIDDEN), lambda i: (0, 0), pipeline_mode=resident),
            pl.BlockSpec((FFN_HIDDEN, D_MODEL), lambda i: (0, 0), pipeline_mode=resident),
            pl.BlockSpec((1, D_MODEL), lambda i: (0, 0)),
        ],
        out_specs=pl.BlockSpec((tm, D_MODEL), lambda i: (i, 0)),
        out_shape=jax.ShapeDtypeStruct((m, D_MODEL), F32),
        scratch_shapes=[pltpu.VMEM((tm, D_MODEL), BF16), pltpu.VMEM((tm, FFN_HIDDEN), BF16)],
        compiler_params=_cparams(("parallel",)),
        name="ffn",
    )(x, g, wgu, wd, final_g)


def _row_tile(m, want):
    while m % want:
        want //= 2
    return want


def _pad_lanes(v):
    return jnp.pad(v.astype(F32), (0, LANES - v.shape[0]))[None, :]


def kernel(x, norm1_g, w_in, conv_a_w, ssm_conv_w, ssm_conv_b, ssm_dt_bias, ssm_a_log, ssm_d,
           ssm_norm_g, w_branch, w_o, norm2_g, w_gate_up, w_down, final_g):
    bsz, seq, _ = x.shape
    assert seq % MOBA_BLOCK == 0 and seq % SSM_CHUNK == 0
    depth = w_in.shape[0]
    assert depth >= 1
    m = bsz * seq
    xbc_col = 3 * BRANCH_W + 3 * BRANCH_W + BRANCH_W
    dt_col = xbc_col + SSM_CONV_DIM
    d_col = dt_col + N_HEADS
    h = x.reshape(m, D_MODEL)
    tm_big = _row_tile(m, 1024)
    tm_mid = _row_tile(m, 512)
    for l in range(depth):
        w_main = jnp.concatenate([w_in[l, :, :xbc_col], w_in[l, :, d_col:d_col + BRANCH_W],
                                  w_in[l, :, xbc_col:dt_col], w_in[l, :, d_col + BRANCH_W:]],
                                 axis=1).astype(BF16)
        w_dt = jnp.pad(w_in[l, :, dt_col:dt_col + N_HEADS], ((0, 0), (0, LANES - N_HEADS))).astype(BF16)
        u, dt = _inproj(h, norm1_g[l][None, :], w_main, w_dt, tm_mid, 1024)
        y_b = _sb_attention(u, bsz, seq, 256)
        y_d = _moba_attention(u, bsz, seq)
        y_a, y_c = _ssd(u, dt, conv_a_w[l], ssm_conv_w[l], ssm_conv_b[l][None, :],
                        _pad_lanes(ssm_dt_bias[l]), _pad_lanes(ssm_a_log[l]),
                        jnp.repeat(ssm_d[l], HEAD_DIM)[None, :], ssm_norm_g[l][None, :], bsz, seq)
        h = _merge(h, u, y_a, y_b, y_c, y_d, w_branch[l].astype(BF16), w_o[l].astype(BF16), tm_mid)
        h = _ffn(h, norm2_g[l][None, :], w_gate_up[l].astype(BF16), w_down[l].astype(BF16),
                 final_g[None, :], tm_mid, 256, final_norm=(l == depth - 1))
    return h.reshape(bsz, seq, D_MODEL)
```

```python
import functools

import jax
import jax.numpy as jnp
from jax import lax
from jax.experimental import pallas as pl
from jax.experimental.pallas import tpu as pltpu

F32 = jnp.float32
BF16 = jnp.bfloat16

D_MODEL = 1024
HEAD_DIM = 64
BRANCH_W = 256
N_BRANCH = 4
N_HEADS = 4
SC_K = 3
SSM_GROUPS = 2
SSM_STATE = 64
SSM_CONV_K = 4
SSM_CHUNK = 256
SSM_CONV_DIM = BRANCH_W + 2 * SSM_GROUPS * SSM_STATE
MOBA_BLOCK = 256
MOBA_TOPK = 3
FFN_HIDDEN = 2816
RMS_EPS = 1e-6

LANES = 128
HIST = 8
BF16_ROWS = 16
NEG = -1e30

N_PACK = 7168
COL_A = 0
COL_SB_Q, COL_SB_K, COL_SB_V = 6, 8, 10
COL_Z = 12
COL_MO_Q = 14
COL_XBC = 16
COL_MO_K, COL_MO_V = 20, 22
COL_G = 24

VMEM_LIMIT = 56 * 1024 * 1024

def _cparams(sem):
    return pltpu.CompilerParams(dimension_semantics=sem, vmem_limit_bytes=VMEM_LIMIT)


def _sigmoid(x):
    return 1.0 / (1.0 + jnp.exp(-x))


def _inproj_kernel(x_ref, g_ref, w_ref, wdt_ref, u_ref, dt_ref, h_ref, *, tn):
    x = x_ref[...]
    ms = jnp.mean(x * x, axis=-1, keepdims=True)
    h_ref[...] = (x * lax.rsqrt(ms + RMS_EPS) * g_ref[...]).astype(BF16)
    dt_ref[...] = jnp.dot(h_ref[...], wdt_ref[...], preferred_element_type=F32)
    assert (COL_G * LANES) % tn == 0
    for c in range(N_PACK // tn):
        sl = slice(c * tn, (c + 1) * tn)
        acc = jnp.dot(h_ref[...], w_ref[:, sl], preferred_element_type=F32)
        if c * tn >= COL_G * LANES:
            acc = 1.0 + jnp.tanh(0.5 * acc)
        u_ref[:, sl] = acc.astype(BF16)


def _inproj(x, g, w, wdt, tm, tn):
    m = x.shape[0]
    resident = pl.Buffered(1)
    return pl.pallas_call(
        functools.partial(_inproj_kernel, tn=tn),
        grid=(m // tm,),
        in_specs=[
            pl.BlockSpec((tm, D_MODEL), lambda i: (i, 0)),
            pl.BlockSpec((1, D_MODEL), lambda i: (0, 0)),
            pl.BlockSpec((D_MODEL, N_PACK), lambda i: (0, 0), pipeline_mode=resident),
            pl.BlockSpec((D_MODEL, LANES), lambda i: (0, 0)),
        ],
        out_specs=[
            pl.BlockSpec((tm, N_PACK), lambda i: (i, 0)),
            pl.BlockSpec((tm, LANES), lambda i: (i, 0)),
        ],
        out_shape=[jax.ShapeDtypeStruct((m, N_PACK), BF16),
                   jax.ShapeDtypeStruct((m, LANES), F32)],
        scratch_shapes=[pltpu.VMEM((tm, D_MODEL), BF16)],
        compiler_params=_cparams(("parallel",)),
        name="inproj",
    )(x, g, w, wdt)


def _neg_abs(x):
    return pltpu.bitcast(pltpu.bitcast(x, jnp.uint32) | jnp.uint32(0x80000000), F32)


def _sb_kernel(q_ref, k_ref, v_ref, o_ref, vt_ref, z_ref, sp_ref, d_ref, w_ref, r_ref, acc_ref,
               *, t, nt):
    qi = pl.program_id(2)

    @pl.when(qi == 0)
    def _():
        head0_rows = lax.broadcasted_iota(jnp.int32, (LANES, t), 0) < HEAD_DIM

        def tr(c, carry):
            vt = v_ref[pl.ds(pl.multiple_of(c * t, t), t), :].astype(F32).T
            vt_ref[0, c] = jnp.where(head0_rows, vt, 0.0).astype(BF16)
            vt_ref[1, c] = jnp.where(head0_rows, 0.0, vt).astype(BF16)
            return carry

        lax.fori_loop(0, nt, tr, 0)

    q_t = (q_ref[...].astype(F32) * (HEAD_DIM ** -0.5)).T
    head0_rows = lax.broadcasted_iota(jnp.int32, (LANES, t), 0) < HEAD_DIM
    qt = (jnp.where(head0_rows, q_t, 0.0).astype(BF16), jnp.where(head0_rows, 0.0, q_t).astype(BF16))
    row = lax.broadcasted_iota(jnp.int32, (t, t), 0)
    col = lax.broadcasted_iota(jnp.int32, (t, t), 1)
    later_t = jnp.where(col > row, 1.0, 0.0).astype(BF16)
    n_tiles = qi + 1

    def tile_of(p):
        return jnp.maximum(qi - p, 0)

    def score_mm(p, par):
        k = k_ref[pl.ds(pl.multiple_of(tile_of(p) * t, t), t), :]
        for h in range(2):
            z_ref[par, h] = jnp.dot(k, qt[h], preferred_element_type=F32)

    def softplus(par, masked):
        for h in range(2):
            z = z_ref[par, h]
            if masked:
                z = jnp.where(row < col, z, NEG)
            sp = jnp.maximum(z, 0.0) + jnp.log(1.0 + jnp.exp(_neg_abs(z)))
            sp_ref[par, h] = sp.astype(BF16)
            d_ref[par, h] = z - sp

    def weights(par):
        for h in range(2):
            sp = sp_ref[par, h]
            between = jnp.dot(later_t, sp, preferred_element_type=F32)
            r = r_ref[h]
            w_ref[par, h] = jnp.exp(d_ref[par, h] - between - r).astype(BF16)
            r_ref[h] = r + between[0:1, :] + sp[0:1, :].astype(F32)

    def value_mm(p, par):
        tile = tile_of(p)
        acc_ref[...] += (jnp.dot(vt_ref[0, tile], w_ref[par, 0], preferred_element_type=F32)
                         + jnp.dot(vt_ref[1, tile], w_ref[par, 1], preferred_element_type=F32))

    def step(s, par):
        value_mm(s - 3, 1 - par)
        weights(par)
        softplus(1 - par, False)
        score_mm(s, par)

    acc_ref[...] = jnp.zeros_like(acc_ref)
    r_ref[...] = jnp.zeros_like(r_ref)
    score_mm(0, 0)
    softplus(0, True)
    score_mm(1, 1)
    weights(0)
    softplus(1, False)
    score_mm(2, 0)

    n_full = jnp.maximum(n_tiles - 3, 0)

    def body(i, carry):
        s = 3 + 2 * i
        step(s, 1)
        step(s + 1, 0)
        return carry

    lax.fori_loop(0, n_full // 2, body, 0)

    @pl.when(n_full % 2 == 1)
    def _():
        step(n_tiles - 1, 1)

    for par_t in range(2):
        @pl.when(n_tiles % 2 == par_t)
        def _():
            @pl.when(n_tiles >= 3)
            def _():
                value_mm(n_tiles - 3, 1 - par_t)
                weights(par_t)
                softplus(1 - par_t, False)

            @pl.when(n_tiles >= 2)
            def _():
                value_mm(n_tiles - 2, par_t)
                weights(1 - par_t)

            value_mm(n_tiles - 1, 1 - par_t)

    o_ref[...] = acc_ref[...].T.astype(o_ref.dtype)


def _sb_attention(u, bsz, seq, t):
    nq = seq // t
    tile_f32 = pltpu.VMEM((2, 2, t, t), F32)
    tile_bf16 = pltpu.VMEM((2, 2, t, t), BF16)
    return pl.pallas_call(
        functools.partial(_sb_kernel, t=t, nt=nq),
        grid=(bsz, N_HEADS // 2, nq),
        in_specs=[
            pl.BlockSpec((t, LANES), lambda b, hp, qi: (b * nq + qi, COL_SB_Q + hp)),
            pl.BlockSpec((seq, LANES), lambda b, hp, qi: (b, COL_SB_K + hp)),
            pl.BlockSpec((seq, LANES), lambda b, hp, qi: (b, COL_SB_V + hp)),
        ],
        out_specs=pl.BlockSpec((t, LANES), lambda b, hp, qi: (b * nq + qi, hp)),
        out_shape=jax.ShapeDtypeStruct((bsz * seq, BRANCH_W), BF16),
        scratch_shapes=[
            pltpu.VMEM((2, nq, LANES, t), BF16),
            tile_f32,
            tile_bf16,
            tile_f32,
            tile_bf16,
            pltpu.VMEM((2, 1, t), F32),
            pltpu.VMEM((LANES, t), F32),
        ],
        compiler_params=_cparams(("parallel", "parallel", "arbitrary")),
        name="sb_attention",
    )(u, u, u)


def _moba_kernel(q_ref, k_ref, v_ref, o_ref, vt_ref, km_ref, sel_ref, z_ref, p_ref, alpha_ref, m_ref,
                 acc_ref, *, nb, ke):
    t = MOBA_BLOCK
    qi = pl.program_id(2)
    n_tiles = qi + 1
    ones_row = (HEAD_DIM, 0)

    @pl.when(qi == 0)
    def _():
        rows = lax.broadcasted_iota(jnp.int32, (LANES, t), 0)

        def tr(c, carry):
            kv = pl.ds(pl.multiple_of(c * t, t), t)
            vt = v_ref[kv, :].astype(F32).T
            vt_ref[0, c] = jnp.where(rows < HEAD_DIM, vt,
                                     jnp.where(rows == ones_row[0], 1.0, 0.0)).astype(BF16)
            vt_ref[1, c] = jnp.where(rows >= HEAD_DIM, vt,
                                     jnp.where(rows == ones_row[1], 1.0, 0.0)).astype(BF16)
            km_ref[pl.ds(c, 1), :] = jnp.sum(k_ref[kv, :].astype(F32), axis=0, keepdims=True) * (1.0 / t)
            return carry

        km_ref[...] = jnp.zeros_like(km_ref)
        lax.fori_loop(0, nb, tr, 0)

    q_t = q_ref[...].astype(F32).T
    head0_rows = lax.broadcasted_iota(jnp.int32, (LANES, t), 0) < HEAD_DIM
    qt = (jnp.where(head0_rows, q_t, 0.0).astype(BF16), jnp.where(head0_rows, 0.0, q_t).astype(BF16))
    qst = tuple(x * (HEAD_DIM ** -0.5) for x in qt)

    km = km_ref[...]
    km_hi = km.astype(BF16)
    km_lo = (km - km_hi.astype(F32)).astype(BF16)
    blk = lax.broadcasted_iota(jnp.int32, (km_ref.shape[0], t), 0).astype(F32)
    qif = qi.astype(F32)
    for h in range(2):
        gate = (jnp.dot(km_hi, qt[h], preferred_element_type=F32)
                + jnp.dot(km_lo, qt[h], preferred_element_type=F32))
        g = jnp.where(blk < qif, gate, -jnp.inf)
        sel = jnp.zeros_like(gate)
        for r in range(ke):
            mx = jnp.max(g, axis=0, keepdims=True)
            idx = jnp.min(jnp.where(g == mx, blk, 1e9), axis=0, keepdims=True)
            hit = blk == idx
            sel = jnp.where(jnp.logical_and(hit, qif > r), 1.0, sel)
            g = jnp.where(hit, -jnp.inf, g)
        sel_ref[h] = sel

    row = lax.broadcasted_iota(jnp.int32, (t, t), 0)
    col = lax.broadcasted_iota(jnp.int32, (t, t), 1)

    def tile_of(p):
        return jnp.maximum(qi - p, 0)

    def score_mm(p, par):
        k = k_ref[pl.ds(pl.multiple_of(tile_of(p) * t, t), t), :]
        for h in range(2):
            z_ref[par, h] = jnp.dot(k, qst[h], preferred_element_type=F32)

    def softmax(p, par, own):
        for h in range(2):
            s = z_ref[par, h]
            if own:
                s = jnp.where(row <= col, s, -jnp.inf)
                m_new = jnp.max(s, axis=0, keepdims=True)
                shift = m_new
                alpha_ref[par, h] = jnp.zeros((1, t), F32)
            else:
                chosen = sel_ref[h, pl.ds(tile_of(p), 1), :] > 0.5
                m_old = m_ref[h]
                m_tile = jnp.max(s, axis=0, keepdims=True)
                m_new = jnp.where(chosen, jnp.maximum(m_old, m_tile), m_old)
                shift = jnp.where(chosen, m_new, jnp.inf)
                alpha_ref[par, h] = jnp.exp(m_old - m_new)
            p_ref[par, h] = jnp.exp(s - shift).astype(BF16)
            m_ref[h] = m_new

    def value_mm(p, par):
        tile = tile_of(p)
        for h in range(2):
            acc_ref[h] = (acc_ref[h] * alpha_ref[par, h]
                          + jnp.dot(vt_ref[h, tile], p_ref[par, h], preferred_element_type=F32))

    def step(s, par):
        value_mm(s - 2, par)
        softmax(s - 1, 1 - par, False)
        score_mm(s, par)

    acc_ref[...] = jnp.zeros_like(acc_ref)
    score_mm(0, 0)
    softmax(0, 0, True)
    score_mm(1, 1)

    n_full = jnp.maximum(n_tiles - 2, 0)

    def body(i, carry):
        s = 2 + 2 * i
        step(s, 0)
        step(s + 1, 1)
        return carry

    lax.fori_loop(0, n_full // 2, body, 0)

    @pl.when(n_full % 2 == 1)
    def _():
        step(n_tiles - 1, 0)

    for par_t in range(2):
        @pl.when(n_tiles % 2 == par_t)
        def _():
            @pl.when(n_tiles >= 2)
            def _():
                value_mm(n_tiles - 2, par_t)
                softmax(n_tiles - 1, 1 - par_t, False)

            value_mm(n_tiles - 1, 1 - par_t)

    rows = lax.broadcasted_iota(jnp.int32, (LANES, t), 0)
    a0 = acc_ref[0]
    a1 = acc_ref[1]
    out_t = jnp.where(rows < HEAD_DIM, a0 / a0[ones_row[0]:ones_row[0] + 1, :],
                      a1 / a1[ones_row[1]:ones_row[1] + 1, :])
    o_ref[...] = out_t.T.astype(o_ref.dtype)


def _moba_attention(u, bsz, seq):
    t = MOBA_BLOCK
    nb = seq // t
    nb_rows = -(-nb // BF16_ROWS) * BF16_ROWS
    ke = max(1, min(MOBA_TOPK, nb - 1))
    return pl.pallas_call(
        functools.partial(_moba_kernel, nb=nb, ke=ke),
        grid=(bsz, N_HEADS // 2, nb),
        in_specs=[
            pl.BlockSpec((t, LANES), lambda b, hp, qi: (b * nb + qi, COL_MO_Q + hp)),
            pl.BlockSpec((seq, LANES), lambda b, hp, qi: (b, COL_MO_K + hp)),
            pl.BlockSpec((seq, LANES), lambda b, hp, qi: (b, COL_MO_V + hp)),
        ],
        out_specs=pl.BlockSpec((t, LANES), lambda b, hp, qi: (b * nb + qi, hp)),
        out_shape=jax.ShapeDtypeStruct((bsz * seq, BRANCH_W), BF16),
        scratch_shapes=[
            pltpu.VMEM((2, nb, LANES, t), BF16),
            pltpu.VMEM((nb_rows, LANES), F32),
            pltpu.VMEM((2, nb_rows, t), F32),
            pltpu.VMEM((2, 2, t, t), F32),
            pltpu.VMEM((2, 2, t, t), BF16),
            pltpu.VMEM((2, 2, 1, t), F32),
            pltpu.VMEM((2, 1, t), F32),
            pltpu.VMEM((2, LANES, t), F32),
        ],
        compiler_params=_cparams(("parallel", "parallel", "arbitrary")),
        name="moba_attention",
    )(u, u, u)


def _per_head(v, lanes_per_head, width):
    head = lax.broadcasted_iota(jnp.int32, (1, width), 1) // lanes_per_head
    out = jnp.zeros((v.shape[0], width), F32)
    for h in range(N_HEADS):
        out = jnp.where(head == h, v[:, h:h + 1], out)
    return out


def _split3(a):
    hi = a.astype(BF16)
    r1 = a - hi.astype(F32)
    mid = r1.astype(BF16)
    lo = (r1 - mid.astype(F32)).astype(BF16)
    return hi, mid, lo


def _ssd_kernel(ua_ref, z_ref, xbc_ref, dt_ref, cwa_ref, cwc_ref, cbias_ref, dtb_ref, alog_ref,
                dskip_ref, ng_ref, ya_ref, yc_ref, bufa, bufc, hst):
    t = SSM_CHUNK
    w = BRANCH_W

    @pl.when(pl.program_id(1) == 0)
    def _():
        bufa[0:HIST, :] = jnp.zeros((HIST, w), F32)
        bufc[0:HIST, :] = jnp.zeros((HIST, SSM_CONV_DIM), F32)
        hst[...] = jnp.zeros_like(hst)

    ua = ua_ref[...].astype(F32)
    bufa[HIST:, :] = ua[:, 2 * w:] * ua[:, :w]
    conv = cwa_ref[0:1, :] * bufa[pl.ds(HIST - SC_K + 1, t), :]
    for kk in range(1, SC_K):
        conv = conv + cwa_ref[kk:kk + 1, :] * bufa[pl.ds(HIST - SC_K + 1 + kk, t), :]
    ya_ref[...] = (ua[:, w:2 * w] * conv).astype(ya_ref.dtype)
    bufa[0:HIST, :] = bufa[t:t + HIST, :]

    bufc[HIST:, :] = xbc_ref[...].astype(F32)
    xc = cbias_ref[...] + cwc_ref[0:1, :] * bufc[pl.ds(HIST - SSM_CONV_K + 1, t), :]
    for kk in range(1, SSM_CONV_K):
        xc = xc + cwc_ref[kk:kk + 1, :] * bufc[pl.ds(HIST - SSM_CONV_K + 1 + kk, t), :]
    bufc[0:HIST, :] = bufc[t:t + HIST, :]
    xc = xc * _sigmoid(xc)
    xs = xc[:, :w]
    b_in = xc[:, w:w + LANES]
    c_in = xc[:, w + LANES:]

    dtp = dt_ref[...] + dtb_ref[...]
    dt = jnp.maximum(dtp, 0.0) + jnp.log(1.0 + jnp.exp(-jnp.abs(dtp)))
    a = dt * (-jnp.exp(alog_ref[...]))
    row = lax.broadcasted_iota(jnp.int32, (t, t), 0)
    col = lax.broadcasted_iota(jnp.int32, (t, t), 1)
    causal = row >= col
    tri = jnp.where(causal, 1.0, 0.0).astype(BF16)
    a_hi, a_mid, a_lo = _split3(a)
    acs = (jnp.dot(tri, a_hi, preferred_element_type=F32)
           + jnp.dot(tri, a_mid, preferred_element_type=F32)
           + jnp.dot(tri, a_lo, preferred_element_type=F32))
    acs_t = acs.T
    acs_x = _per_head(acs, HEAD_DIM, w)
    last_x = acs_x[t - 1:t, :]
    x_dt = xs * _per_head(dt, HEAD_DIM, w)
    to_end_x = jnp.exp(last_x - acs_x)
    from_start_x = jnp.exp(acs_x)
    chunk_decay_x = jnp.exp(last_x)

    lane = lax.broadcasted_iota(jnp.int32, (1, LANES), 1)
    low = lane < HEAD_DIM
    c_bf = c_in.astype(BF16)
    nt = (((1,), (1,)), ((), ()))
    for g in range(SSM_GROUPS):
        gmask = low if g == 0 else jnp.logical_not(low)
        sl = slice(g * LANES, (g + 1) * LANES)
        b_g = jnp.where(gmask, b_in, 0.0)
        cb = lax.dot_general(c_bf, b_g.astype(BF16), nt, preferred_element_type=F32)
        xg = x_dt[:, sl]
        y = jnp.zeros((t, LANES), F32)
        for e in range(2):
            h = 2 * g + e
            seg = acs[:, h:h + 1] - acs_t[h:h + 1, :]
            decay = jnp.exp(jnp.where(causal, seg, -jnp.inf))
            emask = low if e == 0 else jnp.logical_not(low)
            xe = jnp.where(emask, xg, 0.0).astype(BF16)
            y = y + jnp.dot((cb * decay).astype(BF16), xe, preferred_element_type=F32)
        h_enter = hst[g]
        y = y + jnp.dot(c_bf, h_enter.astype(BF16), preferred_element_type=F32) * from_start_x[:, sl]
        state = jnp.dot(b_g.T.astype(BF16), (xg * to_end_x[:, sl]).astype(BF16),
                        preferred_element_type=F32)
        hst[g] = h_enter * chunk_decay_x[:, sl] + state

        y = y + xs[:, sl] * dskip_ref[:, sl]
        zg = z_ref[:, sl].astype(F32)
        gated = y * (zg * _sigmoid(zg))
        ms = jnp.mean(gated * gated, axis=-1, keepdims=True)
        yc_ref[:, sl] = (gated * lax.rsqrt(ms + RMS_EPS) * ng_ref[:, sl]).astype(yc_ref.dtype)


def _ssd(u, dt, cwa, cwc, cbias, dtb, alog, dskip, ng, bsz, seq):
    t = SSM_CHUNK
    nc = seq // t
    w = BRANCH_W
    small = lambda shape: pl.BlockSpec(shape, lambda b, c: (0, 0))
    return pl.pallas_call(
        _ssd_kernel,
        grid=(bsz, nc),
        in_specs=[
            pl.BlockSpec((t, 3 * w), lambda b, c: (b * nc + c, COL_A)),
            pl.BlockSpec((t, w), lambda b, c: (b * nc + c, COL_Z // 2)),
            pl.BlockSpec((t, SSM_CONV_DIM), lambda b, c: (b * nc + c, COL_XBC * LANES // SSM_CONV_DIM)),
            pl.BlockSpec((t, LANES), lambda b, c: (b * nc + c, 0)),
            small((SC_K, w)), small((SSM_CONV_K, SSM_CONV_DIM)), small((1, SSM_CONV_DIM)),
            small((1, LANES)), small((1, LANES)), small((1, w)), small((1, w)),
        ],
        out_specs=[pl.BlockSpec((t, w), lambda b, c: (b * nc + c, 0)),
                   pl.BlockSpec((t, w), lambda b, c: (b * nc + c, 0))],
        out_shape=[jax.ShapeDtypeStruct((bsz * seq, w), BF16),
                   jax.ShapeDtypeStruct((bsz * seq, w), BF16)],
        scratch_shapes=[
            pltpu.VMEM((t + HIST, w), F32),
            pltpu.VMEM((t + HIST, SSM_CONV_DIM), F32),
            pltpu.VMEM((SSM_GROUPS, LANES, LANES), F32),
        ],
        compiler_params=_cparams(("parallel", "arbitrary")),
        name="conv_ssd",
    )(u, u, u, dt, cwa, cwc, cbias, dtb, alog, dskip, ng)


def _merge_kernel(x_ref, ga_ref, gb_ref, gc_ref, gd_ref, ya_ref, yb_ref, yc_ref, yd_ref,
                  wb_ref, wo_ref, o_ref):
    merged = None
    branches = ((ga_ref, ya_ref), (gb_ref, yb_ref), (gc_ref, yc_ref), (gd_ref, yd_ref))
    for i, (g_ref, y_ref) in enumerate(branches):
        term = g_ref[...].astype(F32) * jnp.dot(y_ref[...], wb_ref[i], preferred_element_type=F32)
        merged = term if merged is None else merged + term
    merged = (0.5 * merged).astype(BF16)
    o_ref[...] = x_ref[...] + jnp.dot(merged, wo_ref[...], preferred_element_type=F32)


def _merge(x, u, ya, yb, yc, yd, wb, wo, tm):
    m = x.shape[0]
    ybr = pl.BlockSpec((tm, BRANCH_W), lambda i: (i, 0))
    g0 = COL_G * LANES // D_MODEL
    gates = [pl.BlockSpec((tm, D_MODEL), functools.partial(lambda i, c: (i, c), c=g0 + br))
             for br in range(N_BRANCH)]
    return pl.pallas_call(
        _merge_kernel,
        grid=(m // tm,),
        in_specs=[
            pl.BlockSpec((tm, D_MODEL), lambda i: (i, 0)),
            *gates,
            ybr, ybr, ybr, ybr,
            pl.BlockSpec((N_BRANCH, BRANCH_W, D_MODEL), lambda i: (0, 0, 0)),
            pl.BlockSpec((D_MODEL, D_MODEL), lambda i: (0, 0)),
        ],
        out_specs=pl.BlockSpec((tm, D_MODEL), lambda i: (i, 0)),
        out_shape=jax.ShapeDtypeStruct((m, D_MODEL), F32),
        compiler_params=_cparams(("parallel",)),
        name="merge",
    )(x, u, u, u, u, ya, yb, yc, yd, wb, wo)


def _ffn_kernel(x_ref, g_ref, wgu_ref, wd_ref, fg_ref, o_ref, h_ref, act_ref, *, th, final_norm):
    x = x_ref[...]
    ms = jnp.mean(x * x, axis=-1, keepdims=True)
    h_ref[...] = (x * lax.rsqrt(ms + RMS_EPS) * g_ref[...]).astype(BF16)
    for c in range(FFN_HIDDEN // th):
        h = h_ref[...]
        gate = jnp.dot(h, wgu_ref[:, c * th:(c + 1) * th], preferred_element_type=F32)
        up = jnp.dot(h, wgu_ref[:, FFN_HIDDEN + c * th:FFN_HIDDEN + (c + 1) * th],
                     preferred_element_type=F32)
        act_ref[:, c * th:(c + 1) * th] = ((gate * _sigmoid(gate)) * up).astype(BF16)
    y = x_ref[...] + jnp.dot(act_ref[...], wd_ref[...], preferred_element_type=F32)
    if final_norm:
        ms = jnp.mean(y * y, axis=-1, keepdims=True)
        y = y * lax.rsqrt(ms + RMS_EPS) * fg_ref[...]
    o_ref[...] = y


def _ffn(x, g, wgu, wd, final_g, tm, th, final_norm):
    m = x.shape[0]
    resident = pl.Buffered(1)
    return pl.pallas_call(
        functools.partial(_ffn_kernel, th=th, final_norm=final_norm),
        grid=(m // tm,),
        in_specs=[
            pl.BlockSpec((tm, D_MODEL), lambda i: (i, 0)),
            pl.BlockSpec((1, D_MODEL), lambda i: (0, 0)),
            pl.BlockSpec((D_MODEL, 2 * FFN_HIDDEN), lambda i: (0, 0), pipeline_mode=resident),
            pl.BlockSpec((FFN_HIDDEN, D_MODEL), lambda i: (0, 0), pipeline_mode=resident),
            pl.BlockSpec((1, D_MODEL), lambda i: (0, 0)),
        ],
        out_specs=pl.BlockSpec((tm, D_MODEL), lambda i: (i, 0)),
        out_shape=jax.ShapeDtypeStruct((m, D_MODEL), F32),
        scratch_shapes=[pltpu.VMEM((tm, D_MODEL), BF16), pltpu.VMEM((tm, FFN_HIDDEN), BF16)],
        compiler_params=_cparams(("parallel",)),
        name="ffn",
    )(x, g, wgu, wd, final_g)


def _row_tile(m, want):
    while m % want:
        want //= 2
    return want


def _pad_lanes(v):
    return jnp.pad(v.astype(F32), (0, LANES - v.shape[0]))[None, :]


def kernel(x, norm1_g, w_in, conv_a_w, ssm_conv_w, ssm_conv_b, ssm_dt_bias, ssm_a_log, ssm_d,
           ssm_norm_g, w_branch, w_o, norm2_g, w_gate_up, w_down, final_g):
    bsz, seq, _ = x.shape
    assert seq % MOBA_BLOCK == 0 and seq % SSM_CHUNK == 0
    depth = w_in.shape[0]
    assert depth >= 1
    m = bsz * seq
    xbc_col = 3 * BRANCH_W + 3 * BRANCH_W + BRANCH_W
    dt_col = xbc_col + SSM_CONV_DIM
    d_col = dt_col + N_HEADS
    h = x.reshape(m, D_MODEL)
    tm_big = _row_tile(m, 1024)
    tm_mid = _row_tile(m, 512)
    for l in range(depth):
        w_main = jnp.concatenate([w_in[l, :, :xbc_col], w_in[l, :, d_col:d_col + BRANCH_W],
                                  w_in[l, :, xbc_col:dt_col], w_in[l, :, d_col + BRANCH_W:]],
                                 axis=1).astype(BF16)
        w_dt = jnp.pad(w_in[l, :, dt_col:dt_col + N_HEADS], ((0, 0), (0, LANES - N_HEADS))).astype(BF16)
        u, dt = _inproj(h, norm1_g[l][None, :], w_main, w_dt, tm_mid, 1024)
        y_b = _sb_attention(u, bsz, seq, 256)
        y_d = _moba_attention(u, bsz, seq)
        y_a, y_c = _ssd(u, dt, conv_a_w[l], ssm_conv_w[l], ssm_conv_b[l][None, :],
                        _pad_lanes(ssm_dt_bias[l]), _pad_lanes(ssm_a_log[l]),
                        jnp.repeat(ssm_d[l], HEAD_DIM)[None, :], ssm_norm_g[l][None, :], bsz, seq)
        h = _merge(h, u, y_a, y_b, y_c, y_d, w_branch[l].astype(BF16), w_o[l].astype(BF16), tm_mid)
        h = _ffn(h, norm2_g[l][None, :], w_gate_up[l].astype(BF16), w_down[l].astype(BF16),
                 final_g[None, :], tm_mid, 256, final_norm=(l == depth - 1))
    return h.reshape(bsz, seq, D_MODEL)
```

```python
import functools

import jax
import jax.numpy as jnp
from jax import lax
from jax.experimental import pallas as pl
from jax.experimental.pallas import tpu as pltpu

F32 = jnp.float32
BF16 = jnp.bfloat16

D_MODEL = 1024
HEAD_DIM = 64
BRANCH_W = 256
N_BRANCH = 4
N_HEADS = 4
SC_K = 3
SSM_GROUPS = 2
SSM_STATE = 64
SSM_CONV_K = 4
SSM_CHUNK = 256
SSM_CONV_DIM = BRANCH_W + 2 * SSM_GROUPS * SSM_STATE
MOBA_BLOCK = 256
MOBA_TOPK = 3
FFN_HIDDEN = 2816
RMS_EPS = 1e-6

LANES = 128
HIST = 8
BF16_ROWS = 16
NEG = -1e30

N_PACK = 7168
COL_A = 0
COL_SB_Q, COL_SB_K, COL_SB_V = 6, 8, 10
COL_Z = 12
COL_MO_Q = 14
COL_XBC = 16
COL_MO_K, COL_MO_V = 20, 22
COL_G = 24

VMEM_LIMIT = 56 * 1024 * 1024


def _cparams(sem):
    return pltpu.CompilerParams(dimension_semantics=sem, vmem_limit_bytes=VMEM_LIMIT)


def _sigmoid(x):
    return 1.0 / (1.0 + jnp.exp(-x))


def _inproj_kernel(x_ref, g_ref, w_ref, wdt_ref, u_ref, dt_ref, h_ref, *, tn):
    x = x_ref[...]
    ms = jnp.mean(x * x, axis=-1, keepdims=True)
    h_ref[...] = (x * lax.rsqrt(ms + RMS_EPS) * g_ref[...]).astype(BF16)
    dt_ref[...] = jnp.dot(h_ref[...], wdt_ref[...], preferred_element_type=F32)
    assert (COL_G * LANES) % tn == 0
    for c in range(N_PACK // tn):
        sl = slice(c * tn, (c + 1) * tn)
        acc = jnp.dot(h_ref[...], w_ref[:, sl], preferred_element_type=F32)
        if c * tn >= COL_G * LANES:
            acc = 1.0 + jnp.tanh(0.5 * acc)
        u_ref[:, sl] = acc.astype(BF16)


def _inproj(x, g, w, wdt, tm, tn):
    m = x.shape[0]
    resident = pl.Buffered(1)
    return pl.pallas_call(
        functools.partial(_inproj_kernel, tn=tn),
        grid=(m // tm,),
        in_specs=[
            pl.BlockSpec((tm, D_MODEL), lambda i: (i, 0)),
            pl.BlockSpec((1, D_MODEL), lambda i: (0, 0)),
            pl.BlockSpec((D_MODEL, N_PACK), lambda i: (0, 0), pipeline_mode=resident),
            pl.BlockSpec((D_MODEL, LANES), lambda i: (0, 0)),
        ],
        out_specs=[
            pl.BlockSpec((tm, N_PACK), lambda i: (i, 0)),
            pl.BlockSpec((tm, LANES), lambda i: (i, 0)),
        ],
        out_shape=[jax.ShapeDtypeStruct((m, N_PACK), BF16),
                   jax.ShapeDtypeStruct((m, LANES), F32)],
        scratch_shapes=[pltpu.VMEM((tm, D_MODEL), BF16)],
        compiler_params=_cparams(("parallel",)),
        name="inproj",
    )(x, g, w, wdt)


def _neg_abs(x):
    return pltpu.bitcast(pltpu.bitcast(x, jnp.uint32) | jnp.uint32(0x80000000), F32)


def _attn_kernel(qs_ref, ks_ref, vs_ref, qm_ref, km_ref, vm_ref, os_ref, om_ref,
                 s_vt, s_z, s_sp, s_d, s_w, s_r, s_acc,
                 m_vt, m_mean, m_sel, m_z, m_p, m_alpha, m_max, m_acc, *, nb, ke):
    t = MOBA_BLOCK
    qi = pl.program_id(2)
    n_tiles = qi + 1
    ones_row = (HEAD_DIM, 0)
    rows = lax.broadcasted_iota(jnp.int32, (LANES, t), 0)
    head0_rows = rows < HEAD_DIM

    @pl.when(qi == 0)
    def _():
        def tr(c, carry):
            kv = pl.ds(pl.multiple_of(c * t, t), t)
            vt = vs_ref[kv, :].astype(F32).T
            s_vt[0, c] = jnp.where(head0_rows, vt, 0.0).astype(BF16)
            s_vt[1, c] = jnp.where(head0_rows, 0.0, vt).astype(BF16)
            vt = vm_ref[kv, :].astype(F32).T
            m_vt[0, c] = jnp.where(head0_rows, vt, jnp.where(rows == ones_row[0], 1.0, 0.0)).astype(BF16)
            m_vt[1, c] = jnp.where(head0_rows, jnp.where(rows == ones_row[1], 1.0, 0.0), vt).astype(BF16)
            m_mean[pl.ds(c, 1), :] = jnp.sum(km_ref[kv, :].astype(F32), axis=0, keepdims=True) * (1.0 / t)
            return carry

        m_mean[...] = jnp.zeros_like(m_mean)
        lax.fori_loop(0, nb, tr, 0)

    def split_heads(q_t):
        return (jnp.where(head0_rows, q_t, 0.0).astype(BF16), jnp.where(head0_rows, 0.0, q_t).astype(BF16))

    s_qt = split_heads((qs_ref[...].astype(F32) * (HEAD_DIM ** -0.5)).T)
    m_qt = split_heads(qm_ref[...].astype(F32).T)
    m_qst = tuple(x * (HEAD_DIM ** -0.5) for x in m_qt)

    mean = m_mean[...]
    mean_hi = mean.astype(BF16)
    mean_lo = (mean - mean_hi.astype(F32)).astype(BF16)
    blk = lax.broadcasted_iota(jnp.int32, (m_mean.shape[0], t), 0).astype(F32)
    qif = qi.astype(F32)
    for h in range(2):
        gate = (jnp.dot(mean_hi, m_qt[h], preferred_element_type=F32)
                + jnp.dot(mean_lo, m_qt[h], preferred_element_type=F32))
        g = jnp.where(blk < qif, gate, -jnp.inf)
        sel = jnp.zeros_like(gate)
        for r in range(ke):
            mx = jnp.max(g, axis=0, keepdims=True)
            idx = jnp.min(jnp.where(g == mx, blk, 1e9), axis=0, keepdims=True)
            hit = blk == idx
            sel = jnp.where(jnp.logical_and(hit, qif > r), 1.0, sel)
            g = jnp.where(hit, -jnp.inf, g)
        m_sel[h] = sel

    row = lax.broadcasted_iota(jnp.int32, (t, t), 0)
    col = lax.broadcasted_iota(jnp.int32, (t, t), 1)
    later_t = jnp.where(col > row, 1.0, 0.0).astype(BF16)

    def tile_of(p):
        return jnp.maximum(qi - p, 0)

    def key_rows(p):
        return pl.ds(pl.multiple_of(tile_of(p) * t, t), t)

    def s_score(p, par):
        k = ks_ref[key_rows(p), :]
        for h in range(2):
            s_z[par, h] = jnp.dot(k, s_qt[h], preferred_element_type=F32)

    def s_softplus(par, masked):
        for h in range(2):
            z = s_z[par, h]
            if masked:
                z = jnp.where(row < col, z, NEG)
            sp = jnp.maximum(z, 0.0) + jnp.log(1.0 + jnp.exp(_neg_abs(z)))
            s_sp[par, h] = sp.astype(BF16)
            s_d[par, h] = z - sp

    def s_weights(par):
        for h in range(2):
            sp = s_sp[par, h]
            between = jnp.dot(later_t, sp, preferred_element_type=F32)
            r = s_r[h]
            s_w[par, h] = jnp.exp(s_d[par, h] - between - r).astype(BF16)
            s_r[h] = r + between[0:1, :] + sp[0:1, :].astype(F32)

    def s_value(p, par):
        tile = tile_of(p)
        s_acc[...] += (jnp.dot(s_vt[0, tile], s_w[par, 0], preferred_element_type=F32)
                       + jnp.dot(s_vt[1, tile], s_w[par, 1], preferred_element_type=F32))

    def m_score(p, par):
        k = km_ref[key_rows(p), :]
        for h in range(2):
            m_z[par, h] = jnp.dot(k, m_qst[h], preferred_element_type=F32)

    def m_softmax(p, par, own):
        for h in range(2):
            s = m_z[par, h]
            if own:
                s = jnp.where(row <= col, s, -jnp.inf)
                m_new = jnp.max(s, axis=0, keepdims=True)
                shift = m_new
                m_alpha[par, h] = jnp.zeros((1, t), F32)
            else:
                chosen = m_sel[h, pl.ds(tile_of(p), 1), :] > 0.5
                m_old = m_max[h]
                m_tile = jnp.max(s, axis=0, keepdims=True)
                m_new = jnp.where(chosen, jnp.maximum(m_old, m_tile), m_old)
                shift = jnp.where(chosen, m_new, jnp.inf)
                m_alpha[par, h] = jnp.exp(m_old - m_new)
            m_p[par, h] = jnp.exp(s - shift).astype(BF16)
            m_max[h] = m_new

    def m_value(p, par):
        tile = tile_of(p)
        for h in range(2):
            m_acc[h] = (m_acc[h] * m_alpha[par, h]
                        + jnp.dot(m_vt[h, tile], m_p[par, h], preferred_element_type=F32))

    def step(s, par):
        s_value(s - 3, 1 - par)
        m_value(s - 3, 1 - par)
        s_weights(par)
        m_softmax(s - 2, par, False)
        s_softplus(1 - par, False)
        m_score(s - 1, 1 - par)
        s_score(s, par)

    s_acc[...] = jnp.zeros_like(s_acc)
    s_r[...] = jnp.zeros_like(s_r)
    m_acc[...] = jnp.zeros_like(m_acc)
    s_score(0, 0)
    s_softplus(0, True)
    m_score(0, 0)
    s_score(1, 1)
    s_weights(0)
    m_softmax(0, 0, True)
    s_softplus(1, False)
    m_score(1, 1)
    s_score(2, 0)

    n_full = jnp.maximum(n_tiles - 3, 0)

    def body(i, carry):
        s = 3 + 2 * i
        step(s, 1)
        step(s + 1, 0)
        return carry

    lax.fori_loop(0, n_full // 2, body, 0)

    @pl.when(n_full % 2 == 1)
    def _():
        step(n_tiles - 1, 1)

    for par_t in range(2):
        @pl.when(n_tiles % 2 == par_t)
        def _():
            @pl.when(n_tiles >= 3)
            def _():
                s_value(n_tiles - 3, 1 - par_t)
                m_value(n_tiles - 3, 1 - par_t)
                s_weights(par_t)
                m_softmax(n_tiles - 2, par_t, False)
                s_softplus(1 - par_t, False)
                m_score(n_tiles - 1, 1 - par_t)

            @pl.when(n_tiles >= 2)
            def _():
                s_value(n_tiles - 2, par_t)
                m_value(n_tiles - 2, par_t)
                s_weights(1 - par_t)
                m_softmax(n_tiles - 1, 1 - par_t, False)

            s_value(n_tiles - 1, 1 - par_t)
            m_value(n_tiles - 1, 1 - par_t)

    os_ref[...] = s_acc[...].T.astype(os_ref.dtype)
    a0 = m_acc[0]
    a1 = m_acc[1]
    out_t = jnp.where(head0_rows, a0 / a0[ones_row[0]:ones_row[0] + 1, :],
                      a1 / a1[ones_row[1]:ones_row[1] + 1, :])
    om_ref[...] = out_t.T.astype(om_ref.dtype)


def _attention(u, bsz, seq):
    t = MOBA_BLOCK
    nb = seq // t
    nb_rows = -(-nb // BF16_ROWS) * BF16_ROWS
    ke = max(1, min(MOBA_TOPK, nb - 1))
    tile_f32 = pltpu.VMEM((2, 2, t, t), F32)
    tile_bf16 = pltpu.VMEM((2, 2, t, t), BF16)

    def q_spec(col):
        return pl.BlockSpec((t, LANES), lambda b, hp, qi: (b * nb + qi, col + hp))

    def kv_spec(col):
        return pl.BlockSpec((seq, LANES), lambda b, hp, qi: (b, col + hp))

    out_spec = pl.BlockSpec((t, LANES), lambda b, hp, qi: (b * nb + qi, hp))
    out_shape = jax.ShapeDtypeStruct((bsz * seq, BRANCH_W), BF16)
    return pl.pallas_call(
        functools.partial(_attn_kernel, nb=nb, ke=ke),
        grid=(bsz, N_HEADS // 2, nb),
        in_specs=[q_spec(COL_SB_Q), kv_spec(COL_SB_K), kv_spec(COL_SB_V),
                  q_spec(COL_MO_Q), kv_spec(COL_MO_K), kv_spec(COL_MO_V)],
        out_specs=[out_spec, out_spec],
        out_shape=[out_shape, out_shape],
        scratch_shapes=[
            pltpu.VMEM((2, nb, LANES, t), BF16),
            tile_f32,
            tile_bf16,
            tile_f32,
            tile_bf16,
            pltpu.VMEM((2, 1, t), F32),
            pltpu.VMEM((LANES, t), F32),
            pltpu.VMEM((2, nb, LANES, t), BF16),
            pltpu.VMEM((nb_rows, LANES), F32),
            pltpu.VMEM((2, nb_rows, t), F32),
            tile_f32,
            tile_bf16,
            pltpu.VMEM((2, 2, 1, t), F32),
            pltpu.VMEM((2, 1, t), F32),
            pltpu.VMEM((2, LANES, t), F32),
        ],
        compiler_params=_cparams(("parallel", "parallel", "arbitrary")),
        name="attention",
    )(u, u, u, u, u, u)


def _per_head(v, lanes_per_head, width):
    head = lax.broadcasted_iota(jnp.int32, (1, width), 1) // lanes_per_head
    out = jnp.zeros((v.shape[0], width), F32)
    for h in range(N_HEADS):
        out = jnp.where(head == h, v[:, h:h + 1], out)
    return out


def _split3(a):
    hi = a.astype(BF16)
    r1 = a - hi.astype(F32)
    mid = r1.astype(BF16)
    lo = (r1 - mid.astype(F32)).astype(BF16)
    return hi, mid, lo


def _ssd_kernel(ua_ref, z_ref, xbc_ref, dt_ref, cwa_ref, cwc_ref, cbias_ref, dtb_ref, alog_ref,
                dskip_ref, ng_ref, ya_ref, yc_ref, bufa, bufc, hst):
    t = SSM_CHUNK
    w = BRANCH_W

    @pl.when(pl.program_id(1) == 0)
    def _():
        bufa[0:HIST, :] = jnp.zeros((HIST, w), F32)
        bufc[0:HIST, :] = jnp.zeros((HIST, SSM_CONV_DIM), F32)
        hst[...] = jnp.zeros_like(hst)

    ua = ua_ref[...].astype(F32)
    bufa[HIST:, :] = ua[:, 2 * w:] * ua[:, :w]
    conv = cwa_ref[0:1, :] * bufa[pl.ds(HIST - SC_K + 1, t), :]
    for kk in range(1, SC_K):
        conv = conv + cwa_ref[kk:kk + 1, :] * bufa[pl.ds(HIST - SC_K + 1 + kk, t), :]
    ya_ref[...] = (ua[:, w:2 * w] * conv).astype(ya_ref.dtype)
    bufa[0:HIST, :] = bufa[t:t + HIST, :]

    bufc[HIST:, :] = xbc_ref[...].astype(F32)
    xc = cbias_ref[...] + cwc_ref[0:1, :] * bufc[pl.ds(HIST - SSM_CONV_K + 1, t), :]
    for kk in range(1, SSM_CONV_K):
        xc = xc + cwc_ref[kk:kk + 1, :] * bufc[pl.ds(HIST - SSM_CONV_K + 1 + kk, t), :]
    bufc[0:HIST, :] = bufc[t:t + HIST, :]
    xc = xc * _sigmoid(xc)
    xs = xc[:, :w]
    b_in = xc[:, w:w + LANES]
    c_in = xc[:, w + LANES:]

    dtp = dt_ref[...] + dtb_ref[...]
    dt = jnp.maximum(dtp, 0.0) + jnp.log(1.0 + jnp.exp(-jnp.abs(dtp)))
    a = dt * (-jnp.exp(alog_ref[...]))
    row = lax.broadcasted_iota(jnp.int32, (t, t), 0)
    col = lax.broadcasted_iota(jnp.int32, (t, t), 1)
    causal = row >= col
    tri = jnp.where(causal, 1.0, 0.0).astype(BF16)
    a_hi, a_mid, a_lo = _split3(a)
    acs = (jnp.dot(tri, a_hi, preferred_element_type=F32)
           + jnp.dot(tri, a_mid, preferred_element_type=F32)
           + jnp.dot(tri, a_lo, preferred_element_type=F32))
    acs_t = acs.T
    acs_x = _per_head(acs, HEAD_DIM, w)
    last_x = acs_x[t - 1:t, :]
    x_dt = xs * _per_head(dt, HEAD_DIM, w)
    to_end_x = jnp.exp(last_x - acs_x)
    from_start_x = jnp.exp(acs_x)
    chunk_decay_x = jnp.exp(last_x)

    lane = lax.broadcasted_iota(jnp.int32, (1, LANES), 1)
    low = lane < HEAD_DIM
    c_bf = c_in.astype(BF16)
    nt = (((1,), (1,)), ((), ()))
    for g in range(SSM_GROUPS):
        gmask = low if g == 0 else jnp.logical_not(low)
        sl = slice(g * LANES, (g + 1) * LANES)
        b_g = jnp.where(gmask, b_in, 0.0)
        cb = lax.dot_general(c_bf, b_g.astype(BF16), nt, preferred_element_type=F32)
        xg = x_dt[:, sl]
        y = jnp.zeros((t, LANES), F32)
        for e in range(2):
            h = 2 * g + e
            seg = acs[:, h:h + 1] - acs_t[h:h + 1, :]
            decay = jnp.exp(jnp.where(causal, seg, -jnp.inf))
            emask = low if e == 0 else jnp.logical_not(low)
            xe = jnp.where(emask, xg, 0.0).astype(BF16)
            y = y + jnp.dot((cb * decay).astype(BF16), xe, preferred_element_type=F32)
        h_enter = hst[g]
        y = y + jnp.dot(c_bf, h_enter.astype(BF16), preferred_element_type=F32) * from_start_x[:, sl]
        state = jnp.dot(b_g.T.astype(BF16), (xg * to_end_x[:, sl]).astype(BF16),
                        preferred_element_type=F32)
        hst[g] = h_enter * chunk_decay_x[:, sl] + state

        y = y + xs[:, sl] * dskip_ref[:, sl]
        zg = z_ref[:, sl].astype(F32)
        gated = y * (zg * _sigmoid(zg))
        ms = jnp.mean(gated * gated, axis=-1, keepdims=True)
        yc_ref[:, sl] = (gated * lax.rsqrt(ms + RMS_EPS) * ng_ref[:, sl]).astype(yc_ref.dtype)


def _ssd(u, dt, cwa, cwc, cbias, dtb, alog, dskip, ng, bsz, seq):
    t = SSM_CHUNK
    nc = seq // t
    w = BRANCH_W
    small = lambda shape: pl.BlockSpec(shape, lambda b, c: (0, 0))
    return pl.pallas_call(
        _ssd_kernel,
        grid=(bsz, nc),
        in_specs=[
            pl.BlockSpec((t, 3 * w), lambda b, c: (b * nc + c, COL_A)),
            pl.BlockSpec((t, w), lambda b, c: (b * nc + c, COL_Z // 2)),
            pl.BlockSpec((t, SSM_CONV_DIM), lambda b, c: (b * nc + c, COL_XBC * LANES // SSM_CONV_DIM)),
            pl.BlockSpec((t, LANES), lambda b, c: (b * nc + c, 0)),
            small((SC_K, w)), small((SSM_CONV_K, SSM_CONV_DIM)), small((1, SSM_CONV_DIM)),
            small((1, LANES)), small((1, LANES)), small((1, w)), small((1, w)),
        ],
        out_specs=[pl.BlockSpec((t, w), lambda b, c: (b * nc + c, 0)),
                   pl.BlockSpec((t, w), lambda b, c: (b * nc + c, 0))],
        out_shape=[jax.ShapeDtypeStruct((bsz * seq, w), BF16),
                   jax.ShapeDtypeStruct((bsz * seq, w), BF16)],
        scratch_shapes=[
            pltpu.VMEM((t + HIST, w), F32),
            pltpu.VMEM((t + HIST, SSM_CONV_DIM), F32),
            pltpu.VMEM((SSM_GROUPS, LANES, LANES), F32),
        ],
        compiler_params=_cparams(("parallel", "arbitrary")),
        name="conv_ssd",
    )(u, u, u, dt, cwa, cwc, cbias, dtb, alog, dskip, ng)


def _merge_kernel(x_ref, ga_ref, gb_ref, gc_ref, gd_ref, ya_ref, yb_ref, yc_ref, yd_ref,
                  wb_ref, wo_ref, o_ref):
    merged = None
    branches = ((ga_ref, ya_ref), (gb_ref, yb_ref), (gc_ref, yc_ref), (gd_ref, yd_ref))
    for i, (g_ref, y_ref) in enumerate(branches):
        term = g_ref[...].astype(F32) * jnp.dot(y_ref[...], wb_ref[i], preferred_element_type=F32)
        merged = term if merged is None else merged + term
    merged = (0.5 * merged).astype(BF16)
    o_ref[...] = x_ref[...] + jnp.dot(merged, wo_ref[...], preferred_element_type=F32)


def _merge(x, u, ya, yb, yc, yd, wb, wo, tm):
    m = x.shape[0]
    ybr = pl.BlockSpec((tm, BRANCH_W), lambda i: (i, 0))
    g0 = COL_G * LANES // D_MODEL
    gates = [pl.BlockSpec((tm, D_MODEL), functools.partial(lambda i, c: (i, c), c=g0 + br))
             for br in range(N_BRANCH)]
    return pl.pallas_call(
        _merge_kernel,
        grid=(m // tm,),
        in_specs=[
            pl.BlockSpec((tm, D_MODEL), lambda i: (i, 0)),
            *gates,
            ybr, ybr, ybr, ybr,
            pl.BlockSpec((N_BRANCH, BRANCH_W, D_MODEL), lambda i: (0, 0, 0)),
            pl.BlockSpec((D_MODEL, D_MODEL), lambda i: (0, 0)),
        ],
        out_specs=pl.BlockSpec((tm, D_MODEL), lambda i: (i, 0)),
        out_shape=jax.ShapeDtypeStruct((m, D_MODEL), F32),
        compiler_params=_cparams(("parallel",)),
        name="merge",
    )(x, u, u, u, u, ya, yb, yc, yd, wb, wo)


def _ffn_kernel(x_ref, g_ref, wgu_ref, wd_ref, fg_ref, o_ref, h_ref, act_ref, *, th, final_norm):
    x = x_ref[...]
    ms = jnp.mean(x * x, axis=-1, keepdims=True)
    h_ref[...] = (x * lax.rsqrt(ms + RMS_EPS) * g_ref[...]).astype(BF16)
    for c in range(FFN_HIDDEN // th):
        h = h_ref[...]
        gate = jnp.dot(h, wgu_ref[:, c * th:(c + 1) * th], preferred_element_type=F32)
        up = jnp.dot(h, wgu_ref[:, FFN_HIDDEN + c * th:FFN_HIDDEN + (c + 1) * th],
                     preferred_element_type=F32)
        act_ref[:, c * th:(c + 1) * th] = ((gate * _sigmoid(gate)) * up).astype(BF16)
    y = x_ref[...] + jnp.dot(act_ref[...], wd_ref[...], preferred_element_type=F32)
    if final_norm:
        ms = jnp.mean(y * y, axis=-1, keepdims=True)
        y = y * lax.rsqrt(ms + RMS_EPS) * fg_ref[...]
    o_ref[...] = y


def _ffn(x, g, wgu, wd, final_g, tm, th, final_norm):
    m = x.shape[0]
    resident = pl.Buffered(1)
    return pl.pallas_call(
        functools.partial(_ffn_kernel, th=th, final_norm=final_norm),
        grid=(m // tm,),
        in_specs=[
            pl.BlockSpec((tm, D_MODEL), lambda i: (i, 0)),
            pl.BlockSpec((1, D_MODEL), lambda i: (0, 0)),
            pl.BlockSpec((D_MODEL, 2 * FFN_HIDDEN), lambda i: (0, 0), pipeline_mode=resident),
            pl.BlockSpec((FFN_HIDDEN, D_MODEL), lambda i: (0, 0), pipeline_mode=resident),
            pl.BlockSpec((1, D_MODEL), lambda i: (0, 0)),
        ],
        out_specs=pl.BlockSpec((tm, D_MODEL), lambda i: (i, 0)),
        out_shape=jax.ShapeDtypeStruct((m, D_MODEL), F32),
        scratch_shapes=[pltpu.VMEM((tm, D_MODEL), BF16), pltpu.VMEM((tm, FFN_HIDDEN), BF16)],
        compiler_params=_cparams(("parallel",)),
        name="ffn",
    )(x, g, wgu, wd, final_g)


def _row_tile(m, want):
    while m % want:
        want //= 2
    return want


def _pad_lanes(v):
    return jnp.pad(v.astype(F32), (0, LANES - v.shape[0]))[None, :]


def kernel(x, norm1_g, w_in, conv_a_w, ssm_conv_w, ssm_conv_b, ssm_dt_bias, ssm_a_log, ssm_d,
           ssm_norm_g, w_branch, w_o, norm2_g, w_gate_up, w_down, final_g):
    bsz, seq, _ = x.shape
    assert seq % MOBA_BLOCK == 0 and seq % SSM_CHUNK == 0
    depth = w_in.shape[0]
    assert depth >= 1
    m = bsz * seq
    xbc_col = 3 * BRANCH_W + 3 * BRANCH_W + BRANCH_W
    dt_col = xbc_col + SSM_CONV_DIM
    d_col = dt_col + N_HEADS
    h = x.reshape(m, D_MODEL)
    tm = _row_tile(m, 512)
    for l in range(depth):
        w_main = jnp.concatenate([w_in[l, :, :xbc_col], w_in[l, :, d_col:d_col + BRANCH_W],
                                  w_in[l, :, xbc_col:dt_col], w_in[l, :, d_col + BRANCH_W:]],
                                 axis=1).astype(BF16)
        w_dt = jnp.pad(w_in[l, :, dt_col:dt_col + N_HEADS], ((0, 0), (0, LANES - N_HEADS))).astype(BF16)
        u, dt = _inproj(h, norm1_g[l][None, :], w_main, w_dt, tm, 1024)
        y_b, y_d = _attention(u, bsz, seq)
        y_a, y_c = _ssd(u, dt, conv_a_w[l], ssm_conv_w[l], ssm_conv_b[l][None, :],
                        _pad_lanes(ssm_dt_bias[l]), _pad_lanes(ssm_a_log[l]),
                        jnp.repeat(ssm_d[l], HEAD_DIM)[None, :], ssm_norm_g[l][None, :], bsz, seq)
        h = _merge(h, u, y_a, y_b, y_c, y_d, w_branch[l].astype(BF16), w_o[l].astype(BF16), tm)
        h = _ffn(h, norm2_g[l][None, :], w_gate_up[l].astype(BF16), w_down[l].astype(BF16),
                 final_g[None, :], tm, 256, final_norm=(l == depth - 1))
    return h.reshape(bsz, seq, D_MODEL)
```

```python
import functools

import jax
import jax.numpy as jnp
from jax import lax
from jax.experimental import pallas as pl
from jax.experimental.pallas import tpu as pltpu

F32 = jnp.float32
BF16 = jnp.bfloat16

D_MODEL = 1024
HEAD_DIM = 64
BRANCH_W = 256
N_BRANCH = 4
N_HEADS = 4
SC_K = 3
SSM_GROUPS = 2
SSM_STATE = 64
SSM_CONV_K = 4
SSM_CHUNK = 256
SSM_CONV_DIM = BRANCH_W + 2 * SSM_GROUPS * SSM_STATE
MOBA_BLOCK = 256
MOBA_TOPK = 3
FFN_HIDDEN = 2816
RMS_EPS = 1e-6

LANES = 128
HIST = 8
BF16_ROWS = 16
NEG = -1e30

N_PACK = 7168
COL_A = 0
COL_SB_Q, COL_SB_K, COL_SB_V = 6, 8, 10
COL_Z = 12
COL_MO_Q = 14
COL_XBC = 16
COL_MO_K, COL_MO_V = 20, 22
COL_G = 24

VMEM_LIMIT = 56 * 1024 * 1024


def _cparams(sem):
    return pltpu.CompilerParams(dimension_semantics=sem, vmem_limit_bytes=VMEM_LIMIT)


def _sigmoid(x):
    return 1.0 / (1.0 + jnp.exp(-x))


def _inproj_kernel(x_ref, g_ref, w_ref, wdt_ref, u_ref, dt_ref, h_ref, *, tn):
    x = x_ref[...]
    ms = jnp.mean(x * x, axis=-1, keepdims=True)
    h_ref[...] = (x * lax.rsqrt(ms + RMS_EPS) * g_ref[...]).astype(BF16)
    dt_ref[...] = jnp.dot(h_ref[...], wdt_ref[...], preferred_element_type=F32)
    assert (COL_G * LANES) % tn == 0
    for c in range(N_PACK // tn):
        sl = slice(c * tn, (c + 1) * tn)
        acc = jnp.dot(h_ref[...], w_ref[:, sl], preferred_element_type=F32)
        if c * tn >= COL_G * LANES:
            acc = 1.0 + jnp.tanh(0.5 * acc)
        u_ref[:, sl] = acc.astype(BF16)


def _inproj(x, g, w, wdt, tm, tn):
    m = x.shape[0]
    resident = pl.Buffered(1)
    return pl.pallas_call(
        functools.partial(_inproj_kernel, tn=tn),
        grid=(m // tm,),
        in_specs=[
            pl.BlockSpec((tm, D_MODEL), lambda i: (i, 0)),
            pl.BlockSpec((1, D_MODEL), lambda i: (0, 0)),
            pl.BlockSpec((D_MODEL, N_PACK), lambda i: (0, 0), pipeline_mode=resident),
            pl.BlockSpec((D_MODEL, LANES), lambda i: (0, 0)),
        ],
        out_specs=[
            pl.BlockSpec((tm, N_PACK), lambda i: (i, 0)),
            pl.BlockSpec((tm, LANES), lambda i: (i, 0)),
        ],
        out_shape=[jax.ShapeDtypeStruct((m, N_PACK), BF16),
                   jax.ShapeDtypeStruct((m, LANES), F32)],
        scratch_shapes=[pltpu.VMEM((tm, D_MODEL), BF16)],
        compiler_params=_cparams(("parallel",)),
        name="inproj",
    )(x, g, w, wdt)


def _neg_abs(x):
    return pltpu.bitcast(pltpu.bitcast(x, jnp.uint32) | jnp.uint32(0x80000000), F32)


def _attn_kernel(qs_ref, ks_ref, vs_ref, qm_ref, km_ref, vm_ref, os_ref, om_ref,
                 s_vt, s_z, s_sp, s_d, s_w, s_r, s_acc,
                 m_vt, m_mean, m_sel, m_z, m_p, m_alpha, m_max, m_acc, *, nb, ke):
    t = MOBA_BLOCK
    qi = pl.program_id(1)
    n_tiles = qi + 1
    ones_row = (HEAD_DIM, 0)
    rows = lax.broadcasted_iota(jnp.int32, (LANES, t), 0)
    head0_rows = rows < HEAD_DIM
    heads = range(N_HEADS)

    def pair_lanes(h):
        return slice((h // 2) * LANES, (h // 2 + 1) * LANES)

    def own_rows(h, x, other):
        return jnp.where(head0_rows, x, other) if h % 2 == 0 else jnp.where(head0_rows, other, x)

    @pl.when(qi == 0)
    def _():
        def tr(c, carry):
            kv = pl.ds(pl.multiple_of(c * t, t), t)
            for h in heads:
                vt = vs_ref[kv, pair_lanes(h)].astype(F32).T
                s_vt[h, c] = own_rows(h, vt, 0.0).astype(BF16)
                vt = vm_ref[kv, pair_lanes(h)].astype(F32).T
                m_vt[h, c] = own_rows(h, vt, jnp.where(rows == ones_row[h % 2], 1.0, 0.0)).astype(BF16)
            m_mean[pl.ds(c, 1), :] = jnp.sum(km_ref[kv, :].astype(F32), axis=0, keepdims=True) * (1.0 / t)
            return carry

        m_mean[...] = jnp.zeros_like(m_mean)
        lax.fori_loop(0, nb, tr, 0)

    def split_heads(q_t):
        return tuple(own_rows(h, q_t[(h // 2) * LANES:(h // 2 + 1) * LANES, :], 0.0).astype(BF16) for h in heads)

    s_qt = split_heads((qs_ref[...].astype(F32) * (HEAD_DIM ** -0.5)).T)
    m_qt = split_heads(qm_ref[...].astype(F32).T)
    m_qst = tuple(x * (HEAD_DIM ** -0.5) for x in m_qt)

    mean = m_mean[...]
    mean_hi = mean.astype(BF16)
    mean_lo = (mean - mean_hi.astype(F32)).astype(BF16)
    blk = lax.broadcasted_iota(jnp.int32, (m_mean.shape[0], t), 0).astype(F32)
    qif = qi.astype(F32)
    for h in heads:
        gate = (jnp.dot(mean_hi[:, pair_lanes(h)], m_qt[h], preferred_element_type=F32)
                + jnp.dot(mean_lo[:, pair_lanes(h)], m_qt[h], preferred_element_type=F32))
        g = jnp.where(blk < qif, gate, -jnp.inf)
        sel = jnp.zeros_like(gate)
        for r in range(ke):
            mx = jnp.max(g, axis=0, keepdims=True)
            idx = jnp.min(jnp.where(g == mx, blk, 1e9), axis=0, keepdims=True)
            hit = blk == idx
            sel = jnp.where(jnp.logical_and(hit, qif > r), 1.0, sel)
            g = jnp.where(hit, -jnp.inf, g)
        m_sel[h] = sel

    row = lax.broadcasted_iota(jnp.int32, (t, t), 0)
    col = lax.broadcasted_iota(jnp.int32, (t, t), 1)
    later_t = jnp.where(col > row, 1.0, 0.0).astype(BF16)

    def tile_of(p):
        return jnp.maximum(qi - p, 0)

    def key_rows(p):
        return pl.ds(pl.multiple_of(tile_of(p) * t, t), t)

    def s_score(p, par):
        k = ks_ref[key_rows(p), :]
        for h in heads:
            s_z[par, h] = jnp.dot(k[:, pair_lanes(h)], s_qt[h], preferred_element_type=F32)

    def s_softplus(par, masked):
        for h in heads:
            z = s_z[par, h]
            if masked:
                z = jnp.where(row < col, z, NEG)
            sp = jnp.maximum(z, 0.0) + jnp.log(1.0 + jnp.exp(_neg_abs(z)))
            s_sp[par, h] = sp.astype(BF16)
            s_d[par, h] = z - sp

    def s_weights(par):
        for h in heads:
            sp = s_sp[par, h]
            between = jnp.dot(later_t, sp, preferred_element_type=F32)
            r = s_r[h]
            s_w[par, h] = jnp.exp(s_d[par, h] - between - r).astype(BF16)
            s_r[h] = r + between[0:1, :] + sp[0:1, :].astype(F32)

    def s_value(p, par):
        tile = tile_of(p)
        for i in range(N_HEADS // 2):
            s_acc[i] += (jnp.dot(s_vt[2 * i, tile], s_w[par, 2 * i], preferred_element_type=F32)
                         + jnp.dot(s_vt[2 * i + 1, tile], s_w[par, 2 * i + 1], preferred_element_type=F32))

    def m_score(p, par):
        k = km_ref[key_rows(p), :]
        for h in heads:
            m_z[par, h] = jnp.dot(k[:, pair_lanes(h)], m_qst[h], preferred_element_type=F32)

    def m_softmax(p, par, own):
        for h in heads:
            s = m_z[par, h]
            if own:
                s = jnp.where(row <= col, s, -jnp.inf)
                m_new = jnp.max(s, axis=0, keepdims=True)
                shift = m_new
                m_alpha[par, h] = jnp.zeros((1, t), F32)
            else:
                chosen = m_sel[h, pl.ds(tile_of(p), 1), :] > 0.5
                m_old = m_max[h]
                m_tile = jnp.max(s, axis=0, keepdims=True)
                m_new = jnp.where(chosen, jnp.maximum(m_old, m_tile), m_old)
                shift = jnp.where(chosen, m_new, jnp.inf)
                m_alpha[par, h] = jnp.exp(m_old - m_new)
            m_p[par, h] = jnp.exp(s - shift).astype(BF16)
            m_max[h] = m_new

    def m_value(p, par):
        tile = tile_of(p)
        for h in heads:
            m_acc[h] = (m_acc[h] * m_alpha[par, h]
                        + jnp.dot(m_vt[h, tile], m_p[par, h], preferred_element_type=F32))

    def step(s, par):
        s_value(s - 3, 1 - par)
        m_value(s - 3, 1 - par)
        s_weights(par)
        m_softmax(s - 2, par, False)
        s_softplus(1 - par, False)
        m_score(s - 1, 1 - par)
        s_score(s, par)

    s_acc[...] = jnp.zeros_like(s_acc)
    s_r[...] = jnp.zeros_like(s_r)
    m_acc[...] = jnp.zeros_like(m_acc)
    s_score(0, 0)
    s_softplus(0, True)
    m_score(0, 0)
    s_score(1, 1)
    s_weights(0)
    m_softmax(0, 0, True)
    s_softplus(1, False)
    m_score(1, 1)
    s_score(2, 0)

    n_full = jnp.maximum(n_tiles - 3, 0)

    def body(i, carry):
        s = 3 + 2 * i
        step(s, 1)
        step(s + 1, 0)
        return carry

    lax.fori_loop(0, n_full // 2, body, 0)

    @pl.when(n_full % 2 == 1)
    def _():
        step(n_tiles - 1, 1)

    for par_t in range(2):
        @pl.when(n_tiles % 2 == par_t)
        def _():
            @pl.when(n_tiles >= 3)
            def _():
                s_value(n_tiles - 3, 1 - par_t)
                m_value(n_tiles - 3, 1 - par_t)
                s_weights(par_t)
                m_softmax(n_tiles - 2, par_t, False)
                s_softplus(1 - par_t, False)
                m_score(n_tiles - 1, 1 - par_t)

            @pl.when(n_tiles >= 2)
            def _():
                s_value(n_tiles - 2, par_t)
                m_value(n_tiles - 2, par_t)
                s_weights(1 - par_t)
                m_softmax(n_tiles - 1, 1 - par_t, False)

            s_value(n_tiles - 1, 1 - par_t)
            m_value(n_tiles - 1, 1 - par_t)

    for i in range(N_HEADS // 2):
        sl = slice(i * LANES, (i + 1) * LANES)
        os_ref[:, sl] = s_acc[i].T.astype(os_ref.dtype)
        a0 = m_acc[2 * i]
        a1 = m_acc[2 * i + 1]
        out_t = jnp.where(head0_rows, a0 / a0[ones_row[0]:ones_row[0] + 1, :],
                          a1 / a1[ones_row[1]:ones_row[1] + 1, :])
        om_ref[:, sl] = out_t.T.astype(om_ref.dtype)


def _attention(u, bsz, seq):
    t = MOBA_BLOCK
    w = BRANCH_W
    nb = seq // t
    nb_rows = -(-nb // BF16_ROWS) * BF16_ROWS
    ke = max(1, min(MOBA_TOPK, nb - 1))
    tile_f32 = pltpu.VMEM((2, N_HEADS, t, t), F32)
    tile_bf16 = pltpu.VMEM((2, N_HEADS, t, t), BF16)
    vt_tiles = pltpu.VMEM((N_HEADS, nb, LANES, t), BF16)
    resident = pl.Buffered(1)

    def q_spec(col):
        return pl.BlockSpec((t, w), lambda b, qi: (b * nb + qi, col * LANES // w))

    def kv_spec(col):
        return pl.BlockSpec((seq, w), lambda b, qi: (b, col * LANES // w), pipeline_mode=resident)

    out_spec = pl.BlockSpec((t, w), lambda b, qi: (b * nb + qi, 0))
    out_shape = jax.ShapeDtypeStruct((bsz * seq, w), BF16)
    return pl.pallas_call(
        functools.partial(_attn_kernel, nb=nb, ke=ke),
        grid=(bsz, nb),
        in_specs=[q_spec(COL_SB_Q), kv_spec(COL_SB_K), kv_spec(COL_SB_V),
                  q_spec(COL_MO_Q), kv_spec(COL_MO_K), kv_spec(COL_MO_V)],
        out_specs=[out_spec, out_spec],
        out_shape=[out_shape, out_shape],
        scratch_shapes=[
            vt_tiles,
            tile_f32,
            tile_bf16,
            tile_f32,
            tile_bf16,
            pltpu.VMEM((N_HEADS, 1, t), F32),
            pltpu.VMEM((N_HEADS // 2, LANES, t), F32),
            vt_tiles,
            pltpu.VMEM((nb_rows, w), F32),
            pltpu.VMEM((N_HEADS, nb_rows, t), F32),
            tile_f32,
            tile_bf16,
            pltpu.VMEM((2, N_HEADS, 1, t), F32),
            pltpu.VMEM((N_HEADS, 1, t), F32),
            pltpu.VMEM((N_HEADS, LANES, t), F32),
        ],
        compiler_params=_cparams(("parallel", "arbitrary")),
        name="attention",
    )(u, u, u, u, u, u)


def _per_head(v, lanes_per_head, width):
    head = lax.broadcasted_iota(jnp.int32, (1, width), 1) // lanes_per_head
    out = jnp.zeros((v.shape[0], width), F32)
    for h in range(N_HEADS):
        out = jnp.where(head == h, v[:, h:h + 1], out)
    return out


def _split3(a):
    hi = a.astype(BF16)
    r1 = a - hi.astype(F32)
    mid = r1.astype(BF16)
    lo = (r1 - mid.astype(F32)).astype(BF16)
    return hi, mid, lo


def _ssd_kernel(ua_ref, z_ref, xbc_ref, dt_ref, cwa_ref, cwc_ref, cbias_ref, dtb_ref, alog_ref,
                dskip_ref, ng_ref, ya_ref, yc_ref, bufa, bufc, hst):
    t = SSM_CHUNK
    w = BRANCH_W

    @pl.when(pl.program_id(1) == 0)
    def _():
        bufa[0:HIST, :] = jnp.zeros((HIST, w), F32)
        bufc[0:HIST, :] = jnp.zeros((HIST, SSM_CONV_DIM), F32)
        hst[...] = jnp.zeros_like(hst)

    ua = ua_ref[...].astype(F32)
    bufa[HIST:, :] = ua[:, 2 * w:] * ua[:, :w]
    conv = cwa_ref[0:1, :] * bufa[pl.ds(HIST - SC_K + 1, t), :]
    for kk in range(1, SC_K):
        conv = conv + cwa_ref[kk:kk + 1, :] * bufa[pl.ds(HIST - SC_K + 1 + kk, t), :]
    ya_ref[...] = (ua[:, w:2 * w] * conv).astype(ya_ref.dtype)
    bufa[0:HIST, :] = bufa[t:t + HIST, :]

    bufc[HIST:, :] = xbc_ref[...].astype(F32)
    xc = cbias_ref[...] + cwc_ref[0:1, :] * bufc[pl.ds(HIST - SSM_CONV_K + 1, t), :]
    for kk in range(1, SSM_CONV_K):
        xc = xc + cwc_ref[kk:kk + 1, :] * bufc[pl.ds(HIST - SSM_CONV_K + 1 + kk, t), :]
    bufc[0:HIST, :] = bufc[t:t + HIST, :]
    xc = xc * _sigmoid(xc)
    xs = xc[:, :w]
    b_in = xc[:, w:w + LANES]
    c_in = xc[:, w + LANES:]

    dtp = dt_ref[...] + dtb_ref[...]
    dt = jnp.maximum(dtp, 0.0) + jnp.log(1.0 + jnp.exp(-jnp.abs(dtp)))
    a = dt * (-jnp.exp(alog_ref[...]))
    row = lax.broadcasted_iota(jnp.int32, (t, t), 0)
    col = lax.broadcasted_iota(jnp.int32, (t, t), 1)
    causal = row >= col
    tri = jnp.where(causal, 1.0, 0.0).astype(BF16)
    a_hi, a_mid, a_lo = _split3(a)
    acs = (jnp.dot(tri, a_hi, preferred_element_type=F32)
           + jnp.dot(tri, a_mid, preferred_element_type=F32)
           + jnp.dot(tri, a_lo, preferred_element_type=F32))
    acs_t = acs.T
    acs_x = _per_head(acs, HEAD_DIM, w)
    last_x = acs_x[t - 1:t, :]
    x_dt = xs * _per_head(dt, HEAD_DIM, w)
    to_end_x = jnp.exp(last_x - acs_x)
    from_start_x = jnp.exp(acs_x)
    chunk_decay_x = jnp.exp(last_x)

    lane = lax.broadcasted_iota(jnp.int32, (1, LANES), 1)
    low = lane < HEAD_DIM
    c_bf = c_in.astype(BF16)
    nt = (((1,), (1,)), ((), ()))
    for g in range(SSM_GROUPS):
        gmask = low if g == 0 else jnp.logical_not(low)
        sl = slice(g * LANES, (g + 1) * LANES)
        b_g = jnp.where(gmask, b_in, 0.0)
        cb = lax.dot_general(c_bf, b_g.astype(BF16), nt, preferred_element_type=F32)
        xg = x_dt[:, sl]
        y = jnp.zeros((t, LANES), F32)
        for e in range(2):
            h = 2 * g + e
            seg = acs[:, h:h + 1] - acs_t[h:h + 1, :]
            decay = jnp.exp(jnp.where(causal, seg, -jnp.inf))
            emask = low if e == 0 else jnp.logical_not(low)
            xe = jnp.where(emask, xg, 0.0).astype(BF16)
            y = y + jnp.dot((cb * decay).astype(BF16), xe, preferred_element_type=F32)
        h_enter = hst[g]
        y = y + jnp.dot(c_bf, h_enter.astype(BF16), preferred_element_type=F32) * from_start_x[:, sl]
        state = jnp.dot(b_g.T.astype(BF16), (xg * to_end_x[:, sl]).astype(BF16),
                        preferred_element_type=F32)
        hst[g] = h_enter * chunk_decay_x[:, sl] + state

        y = y + xs[:, sl] * dskip_ref[:, sl]
        zg = z_ref[:, sl].astype(F32)
        gated = y * (zg * _sigmoid(zg))
        ms = jnp.mean(gated * gated, axis=-1, keepdims=True)
        yc_ref[:, sl] = (gated * lax.rsqrt(ms + RMS_EPS) * ng_ref[:, sl]).astype(yc_ref.dtype)


def _ssd(u, dt, cwa, cwc, cbias, dtb, alog, dskip, ng, bsz, seq):
    t = SSM_CHUNK
    nc = seq // t
    w = BRANCH_W
    small = lambda shape: pl.BlockSpec(shape, lambda b, c: (0, 0))
    return pl.pallas_call(
        _ssd_kernel,
        grid=(bsz, nc),
        in_specs=[
            pl.BlockSpec((t, 3 * w), lambda b, c: (b * nc + c, COL_A)),
            pl.BlockSpec((t, w), lambda b, c: (b * nc + c, COL_Z // 2)),
            pl.BlockSpec((t, SSM_CONV_DIM), lambda b, c: (b * nc + c, COL_XBC * LANES // SSM_CONV_DIM)),
            pl.BlockSpec((t, LANES), lambda b, c: (b * nc + c, 0)),
            small((SC_K, w)), small((SSM_CONV_K, SSM_CONV_DIM)), small((1, SSM_CONV_DIM)),
            small((1, LANES)), small((1, LANES)), small((1, w)), small((1, w)),
        ],
        out_specs=[pl.BlockSpec((t, w), lambda b, c: (b * nc + c, 0)),
                   pl.BlockSpec((t, w), lambda b, c: (b * nc + c, 0))],
        out_shape=[jax.ShapeDtypeStruct((bsz * seq, w), BF16),
                   jax.ShapeDtypeStruct((bsz * seq, w), BF16)],
        scratch_shapes=[
            pltpu.VMEM((t + HIST, w), F32),
            pltpu.VMEM((t + HIST, SSM_CONV_DIM), F32),
            pltpu.VMEM((SSM_GROUPS, LANES, LANES), F32),
        ],
        compiler_params=_cparams(("parallel", "arbitrary")),
        name="conv_ssd",
    )(u, u, u, dt, cwa, cwc, cbias, dtb, alog, dskip, ng)


def _merge_kernel(x_ref, ga_ref, gb_ref, gc_ref, gd_ref, ya_ref, yb_ref, yc_ref, yd_ref,
                  wb_ref, wo_ref, o_ref):
    merged = None
    branches = ((ga_ref, ya_ref), (gb_ref, yb_ref), (gc_ref, yc_ref), (gd_ref, yd_ref))
    for i, (g_ref, y_ref) in enumerate(branches):
        term = g_ref[...].astype(F32) * jnp.dot(y_ref[...], wb_ref[i], preferred_element_type=F32)
        merged = term if merged is None else merged + term
    merged = (0.5 * merged).astype(BF16)
    o_ref[...] = x_ref[...] + jnp.dot(merged, wo_ref[...], preferred_element_type=F32)


def _merge(x, u, ya, yb, yc, yd, wb, wo, tm):
    m = x.shape[0]
    ybr = pl.BlockSpec((tm, BRANCH_W), lambda i: (i, 0))
    g0 = COL_G * LANES // D_MODEL
    gates = [pl.BlockSpec((tm, D_MODEL), functools.partial(lambda i, c: (i, c), c=g0 + br))
             for br in range(N_BRANCH)]
    return pl.pallas_call(
        _merge_kernel,
        grid=(m // tm,),
        in_specs=[
            pl.BlockSpec((tm, D_MODEL), lambda i: (i, 0)),
            *gates,
            ybr, ybr, ybr, ybr,
            pl.BlockSpec((N_BRANCH, BRANCH_W, D_MODEL), lambda i: (0, 0, 0)),
            pl.BlockSpec((D_MODEL, D_MODEL), lambda i: (0, 0)),
        ],
        out_specs=pl.BlockSpec((tm, D_MODEL), lambda i: (i, 0)),
        out_shape=jax.ShapeDtypeStruct((m, D_MODEL), F32),
        compiler_params=_cparams(("parallel",)),
        name="merge",
    )(x, u, u, u, u, ya, yb, yc, yd, wb, wo)


def _ffn_kernel(x_ref, g_ref, wgu_ref, wd_ref, fg_ref, o_ref, h_ref, act_ref, *, th, final_norm):
    x = x_ref[...]
    ms = jnp.mean(x * x, axis=-1, keepdims=True)
    h_ref[...] = (x * lax.rsqrt(ms + RMS_EPS) * g_ref[...]).astype(BF16)
    for c in range(FFN_HIDDEN // th):
        h = h_ref[...]
        gate = jnp.dot(h, wgu_ref[:, c * th:(c + 1) * th], preferred_element_type=F32)
        up = jnp.dot(h, wgu_ref[:, FFN_HIDDEN + c * th:FFN_HIDDEN + (c + 1) * th],
                     preferred_element_type=F32)
        act_ref[:, c * th:(c + 1) * th] = ((gate * _sigmoid(gate)) * up).astype(BF16)
    y = x_ref[...] + jnp.dot(act_ref[...], wd_ref[...], preferred_element_type=F32)
    if final_norm:
        ms = jnp.mean(y * y, axis=-1, keepdims=True)
        y = y * lax.rsqrt(ms + RMS_EPS) * fg_ref[...]
    o_ref[...] = y


def _ffn(x, g, wgu, wd, final_g, tm, th, final_norm):
    m = x.shape[0]
    resident = pl.Buffered(1)
    return pl.pallas_call(
        functools.partial(_ffn_kernel, th=th, final_norm=final_norm),
        grid=(m // tm,),
        in_specs=[
            pl.BlockSpec((tm, D_MODEL), lambda i: (i, 0)),
            pl.BlockSpec((1, D_MODEL), lambda i: (0, 0)),
            pl.BlockSpec((D_MODEL, 2 * FFN_HIDDEN), lambda i: (0, 0), pipeline_mode=resident),
            pl.BlockSpec((FFN_HIDDEN, D_MODEL), lambda i: (0, 0), pipeline_mode=resident),
            pl.BlockSpec((1, D_MODEL), lambda i: (0, 0)),
        ],
        out_specs=pl.BlockSpec((tm, D_MODEL), lambda i: (i, 0)),
        out_shape=jax.ShapeDtypeStruct((m, D_MODEL), F32),
        scratch_shapes=[pltpu.VMEM((tm, D_MODEL), BF16), pltpu.VMEM((tm, FFN_HIDDEN), BF16)],
        compiler_params=_cparams(("parallel",)),
        name="ffn",
    )(x, g, wgu, wd, final_g)


def _row_tile(m, want):
    while m % want:
        want //= 2
    return want


def _pad_lanes(v):
    return jnp.pad(v.astype(F32), (0, LANES - v.shape[0]))[None, :]


def kernel(x, norm1_g, w_in, conv_a_w, ssm_conv_w, ssm_conv_b, ssm_dt_bias, ssm_a_log, ssm_d,
           ssm_norm_g, w_branch, w_o, norm2_g, w_gate_up, w_down, final_g):
    bsz, seq, _ = x.shape
    assert seq % MOBA_BLOCK == 0 and seq % SSM_CHUNK == 0
    depth = w_in.shape[0]
    assert depth >= 1
    m = bsz * seq
    xbc_col = 3 * BRANCH_W + 3 * BRANCH_W + BRANCH_W
    dt_col = xbc_col + SSM_CONV_DIM
    d_col = dt_col + N_HEADS
    h = x.reshape(m, D_MODEL)
    tm = _row_tile(m, 512)
    for l in range(depth):
        w_main = jnp.concatenate([w_in[l, :, :xbc_col], w_in[l, :, d_col:d_col + BRANCH_W],
                                  w_in[l, :, xbc_col:dt_col], w_in[l, :, d_col + BRANCH_W:]],
                                 axis=1).astype(BF16)
        w_dt = jnp.pad(w_in[l, :, dt_col:dt_col + N_HEADS], ((0, 0), (0, LANES - N_HEADS))).astype(BF16)
        u, dt = _inproj(h, norm1_g[l][None, :], w_main, w_dt, tm, 1024)
        y_b, y_d = _attention(u, bsz, seq)
        y_a, y_c = _ssd(u, dt, conv_a_w[l], ssm_conv_w[l], ssm_conv_b[l][None, :],
                        _pad_lanes(ssm_dt_bias[l]), _pad_lanes(ssm_a_log[l]),
                        jnp.repeat(ssm_d[l], HEAD_DIM)[None, :], ssm_norm_g[l][None, :], bsz, seq)
        h = _merge(h, u, y_a, y_b, y_c, y_d, w_branch[l].astype(BF16), w_o[l].astype(BF16), tm)
        h = _ffn(h, norm2_g[l][None, :], w_gate_up[l].astype(BF16), w_down[l].astype(BF16),
                 final_g[None, :], tm, 256, final_norm=(l == depth - 1))
    return h.reshape(bsz, seq, D_MODEL)
```

```python
import functools

import jax
import jax.numpy as jnp
from jax import lax
from jax.experimental import pallas as pl
from jax.experimental.pallas import tpu as pltpu

F32 = jnp.float32
BF16 = jnp.bfloat16

D_MODEL = 1024
HEAD_DIM = 64
BRANCH_W = 256
N_BRANCH = 4
N_HEADS = 4
SC_K = 3
SSM_GROUPS = 2
SSM_STATE = 64
SSM_CONV_K = 4
SSM_CHUNK = 256
SSM_CONV_DIM = BRANCH_W + 2 * SSM_GROUPS * SSM_STATE
MOBA_BLOCK = 256
MOBA_TOPK = 3
FFN_HIDDEN = 2816
RMS_EPS = 1e-6

LANES = 128
HIST = 8
BF16_ROWS = 16
NEG = -1e30

N_PACK = 3072
COL_A = 0
COL_SB_Q, COL_SB_K, COL_SB_V = 6, 8, 10
COL_Z = 12
COL_MO_Q = 14
COL_XBC = 16
COL_MO_K, COL_MO_V = 20, 22

VMEM_LIMIT = 56 * 1024 * 1024


def _cparams(sem):
    return pltpu.CompilerParams(dimension_semantics=sem, vmem_limit_bytes=VMEM_LIMIT)


def _sigmoid(x):
    return 1.0 / (1.0 + jnp.exp(-x))


def _inproj_kernel(x_ref, g_ref, w_ref, wdt_ref, u_ref, dt_ref, h_ref, *, tn):
    x = x_ref[...]
    ms = jnp.mean(x * x, axis=-1, keepdims=True)
    h_ref[...] = (x * lax.rsqrt(ms + RMS_EPS) * g_ref[...]).astype(BF16)
    dt_ref[...] = jnp.dot(h_ref[...], wdt_ref[...], preferred_element_type=F32)
    for c in range(N_PACK // tn):
        sl = slice(c * tn, (c + 1) * tn)
        u_ref[:, sl] = jnp.dot(h_ref[...], w_ref[:, sl], preferred_element_type=F32).astype(BF16)


def _inproj(x, g, w, wdt, tm, tn):
    m = x.shape[0]
    resident = pl.Buffered(1)
    return pl.pallas_call(
        functools.partial(_inproj_kernel, tn=tn),
        grid=(m // tm,),
        in_specs=[
            pl.BlockSpec((tm, D_MODEL), lambda i: (i, 0)),
            pl.BlockSpec((1, D_MODEL), lambda i: (0, 0)),
            pl.BlockSpec((D_MODEL, N_PACK), lambda i: (0, 0), pipeline_mode=resident),
            pl.BlockSpec((D_MODEL, LANES), lambda i: (0, 0)),
        ],
        out_specs=[
            pl.BlockSpec((tm, N_PACK), lambda i: (i, 0)),
            pl.BlockSpec((tm, LANES), lambda i: (i, 0)),
        ],
        out_shape=[jax.ShapeDtypeStruct((m, N_PACK), BF16),
                   jax.ShapeDtypeStruct((m, LANES), F32)],
        scratch_shapes=[pltpu.VMEM((tm, D_MODEL), BF16)],
        compiler_params=_cparams(("parallel",)),
        name="inproj",
    )(x, g, w, wdt)


def _neg_abs(x):
    return pltpu.bitcast(pltpu.bitcast(x, jnp.uint32) | jnp.uint32(0x80000000), F32)


def _attn_kernel(qs_ref, ks_ref, vs_ref, qm_ref, km_ref, vm_ref, os_ref, om_ref,
                 s_vt, s_z, s_sp, s_d, s_w, s_r, s_acc,
                 m_vt, m_mean, m_sel, m_z, m_p, m_alpha, m_max, m_acc, *, nb, ke):
    t = MOBA_BLOCK
    qi = pl.program_id(1)
    n_tiles = qi + 1
    ones_row = (HEAD_DIM, 0)
    rows = lax.broadcasted_iota(jnp.int32, (LANES, t), 0)
    head0_rows = rows < HEAD_DIM
    heads = range(N_HEADS)

    def pair_lanes(h):
        return slice((h // 2) * LANES, (h // 2 + 1) * LANES)

    def own_rows(h, x, other):
        return jnp.where(head0_rows, x, other) if h % 2 == 0 else jnp.where(head0_rows, other, x)

    @pl.when(qi == 0)
    def _():
        def tr(c, carry):
            kv = pl.ds(pl.multiple_of(c * t, t), t)
            for h in heads:
                vt = vs_ref[kv, pair_lanes(h)].astype(F32).T
                s_vt[h, c] = own_rows(h, vt, 0.0).astype(BF16)
                vt = vm_ref[kv, pair_lanes(h)].astype(F32).T
                m_vt[h, c] = own_rows(h, vt, jnp.where(rows == ones_row[h % 2], 1.0, 0.0)).astype(BF16)
            m_mean[pl.ds(c, 1), :] = jnp.sum(km_ref[kv, :].astype(F32), axis=0, keepdims=True) * (1.0 / t)
            return carry

        m_mean[...] = jnp.zeros_like(m_mean)
        lax.fori_loop(0, nb, tr, 0)

    def split_heads(q_t):
        return tuple(own_rows(h, q_t[(h // 2) * LANES:(h // 2 + 1) * LANES, :], 0.0).astype(BF16) for h in heads)

    s_qt = split_heads((qs_ref[...].astype(F32) * (HEAD_DIM ** -0.5)).T)
    m_qt = split_heads(qm_ref[...].astype(F32).T)
    m_qst = tuple(x * (HEAD_DIM ** -0.5) for x in m_qt)

    mean = m_mean[...]
    mean_hi = mean.astype(BF16)
    mean_lo = (mean - mean_hi.astype(F32)).astype(BF16)
    blk = lax.broadcasted_iota(jnp.int32, (m_mean.shape[0], t), 0).astype(F32)
    qif = qi.astype(F32)
    for h in heads:
        gate = (jnp.dot(mean_hi[:, pair_lanes(h)], m_qt[h], preferred_element_type=F32)
                + jnp.dot(mean_lo[:, pair_lanes(h)], m_qt[h], preferred_element_type=F32))
        g = jnp.where(blk < qif, gate, -jnp.inf)
        sel = jnp.zeros_like(gate)
        for r in range(ke):
            mx = jnp.max(g, axis=0, keepdims=True)
            idx = jnp.min(jnp.where(g == mx, blk, 1e9), axis=0, keepdims=True)
            hit = blk == idx
            sel = jnp.where(jnp.logical_and(hit, qif > r), 1.0, sel)
            g = jnp.where(hit, -jnp.inf, g)
        m_sel[h] = sel

    row = lax.broadcasted_iota(jnp.int32, (t, t), 0)
    col = lax.broadcasted_iota(jnp.int32, (t, t), 1)
    later_t = jnp.where(col > row, 1.0, 0.0).astype(BF16)

    def tile_of(p):
        return jnp.maximum(qi - p, 0)

    def key_rows(p):
        return pl.ds(pl.multiple_of(tile_of(p) * t, t), t)

    def s_score(p, par):
        k = ks_ref[key_rows(p), :]
        for h in heads:
            s_z[par, h] = jnp.dot(k[:, pair_lanes(h)], s_qt[h], preferred_element_type=F32)

    def s_softplus(par, masked):
        for h in heads:
            z = s_z[par, h]
            if masked:
                z = jnp.where(row < col, z, NEG)
            sp = jnp.maximum(z, 0.0) + jnp.log(1.0 + jnp.exp(_neg_abs(z)))
            s_sp[par, h] = sp.astype(BF16)
            s_d[par, h] = z - sp

    def s_weights(par):
        for h in heads:
            sp = s_sp[par, h]
            between = jnp.dot(later_t, sp, preferred_element_type=F32)
            r = s_r[h]
            s_w[par, h] = jnp.exp(s_d[par, h] - between - r).astype(BF16)
            s_r[h] = r + between[0:1, :] + sp[0:1, :].astype(F32)

    def s_value(p, par):
        tile = tile_of(p)
        for i in range(N_HEADS // 2):
            s_acc[i] += (jnp.dot(s_vt[2 * i, tile], s_w[par, 2 * i], preferred_element_type=F32)
                         + jnp.dot(s_vt[2 * i + 1, tile], s_w[par, 2 * i + 1], preferred_element_type=F32))

    def m_score(p, par):
        k = km_ref[key_rows(p), :]
        for h in heads:
            m_z[par, h] = jnp.dot(k[:, pair_lanes(h)], m_qst[h], preferred_element_type=F32)

    def m_softmax(p, par, own):
        for h in heads:
            s = m_z[par, h]
            if own:
                s = jnp.where(row <= col, s, -jnp.inf)
                m_new = jnp.max(s, axis=0, keepdims=True)
                shift = m_new
                m_alpha[par, h] = jnp.zeros((1, t), F32)
            else:
                chosen = m_sel[h, pl.ds(tile_of(p), 1), :] > 0.5
                m_old = m_max[h]
                m_tile = jnp.max(s, axis=0, keepdims=True)
                m_new = jnp.where(chosen, jnp.maximum(m_old, m_tile), m_old)
                shift = jnp.where(chosen, m_new, jnp.inf)
                m_alpha[par, h] = jnp.exp(m_old - m_new)
            m_p[par, h] = jnp.exp(s - shift).astype(BF16)
            m_max[h] = m_new

    def m_value(p, par):
        tile = tile_of(p)
        for h in heads:
            m_acc[h] = (m_acc[h] * m_alpha[par, h]
                        + jnp.dot(m_vt[h, tile], m_p[par, h], preferred_element_type=F32))

    def step(s, par):
        s_value(s - 3, 1 - par)
        m_value(s - 3, 1 - par)
        s_weights(par)
        m_softmax(s - 2, par, False)
        s_softplus(1 - par, False)
        m_score(s - 1, 1 - par)
        s_score(s, par)

    s_acc[...] = jnp.zeros_like(s_acc)
    s_r[...] = jnp.zeros_like(s_r)
    m_acc[...] = jnp.zeros_like(m_acc)
    s_score(0, 0)
    s_softplus(0, True)
    m_score(0, 0)
    s_score(1, 1)
    s_weights(0)
    m_softmax(0, 0, True)
    s_softplus(1, False)
    m_score(1, 1)
    s_score(2, 0)

    n_full = jnp.maximum(n_tiles - 3, 0)

    def body(i, carry):
        s = 3 + 2 * i
        step(s, 1)
        step(s + 1, 0)
        return carry

    lax.fori_loop(0, n_full // 2, body, 0)

    @pl.when(n_full % 2 == 1)
    def _():
        step(n_tiles - 1, 1)

    def drain(par_t, first):
        if first <= 0:
            s_value(n_tiles - 3, 1 - par_t)
            m_value(n_tiles - 3, 1 - par_t)
            s_weights(par_t)
            m_softmax(n_tiles - 2, par_t, False)
            s_softplus(1 - par_t, False)
            m_score(n_tiles - 1, 1 - par_t)
        if first <= 1:
            s_value(n_tiles - 2, par_t)
            m_value(n_tiles - 2, par_t)
            s_weights(1 - par_t)
            m_softmax(n_tiles - 1, 1 - par_t, False)
        s_value(n_tiles - 1, 1 - par_t)
        m_value(n_tiles - 1, 1 - par_t)

    for par_t in range(2):
        @pl.when(jnp.logical_and(n_tiles >= 3, n_tiles % 2 == par_t))
        def _():
            drain(par_t, 0)

    @pl.when(n_tiles == 2)
    def _():
        drain(0, 1)

    @pl.when(n_tiles == 1)
    def _():
        drain(1, 2)

    for i in range(N_HEADS // 2):
        sl = slice(i * LANES, (i + 1) * LANES)
        os_ref[:, sl] = s_acc[i].T.astype(os_ref.dtype)
        a0 = m_acc[2 * i]
        a1 = m_acc[2 * i + 1]
        out_t = jnp.where(head0_rows, a0 / a0[ones_row[0]:ones_row[0] + 1, :],
                          a1 / a1[ones_row[1]:ones_row[1] + 1, :])
        om_ref[:, sl] = out_t.T.astype(om_ref.dtype)


def _attention(u, bsz, seq):
    t = MOBA_BLOCK
    w = BRANCH_W
    nb = seq // t
    nb_rows = -(-nb // BF16_ROWS) * BF16_ROWS
    ke = max(1, min(MOBA_TOPK, nb - 1))
    tile_f32 = pltpu.VMEM((2, N_HEADS, t, t), F32)
    tile_bf16 = pltpu.VMEM((2, N_HEADS, t, t), BF16)
    vt_tiles = pltpu.VMEM((N_HEADS, nb, LANES, t), BF16)
    resident = pl.Buffered(1)

    def q_spec(col):
        return pl.BlockSpec((t, w), lambda b, qi: (b * nb + qi, col * LANES // w))

    def kv_spec(col):
        return pl.BlockSpec((seq, w), lambda b, qi: (b, col * LANES // w), pipeline_mode=resident)

    out_spec = pl.BlockSpec((t, w), lambda b, qi: (b * nb + qi, 0))
    out_shape = jax.ShapeDtypeStruct((bsz * seq, w), BF16)
    return pl.pallas_call(
        functools.partial(_attn_kernel, nb=nb, ke=ke),
        grid=(bsz, nb),
        in_specs=[q_spec(COL_SB_Q), kv_spec(COL_SB_K), kv_spec(COL_SB_V),
                  q_spec(COL_MO_Q), kv_spec(COL_MO_K), kv_spec(COL_MO_V)],
        out_specs=[out_spec, out_spec],
        out_shape=[out_shape, out_shape],
        scratch_shapes=[
            vt_tiles,
            tile_f32,
            tile_bf16,
            tile_f32,
            tile_bf16,
            pltpu.VMEM((N_HEADS, 1, t), F32),
            pltpu.VMEM((N_HEADS // 2, LANES, t), F32),
            vt_tiles,
            pltpu.VMEM((nb_rows, w), F32),
            pltpu.VMEM((N_HEADS, nb_rows, t), F32),
            tile_f32,
            tile_bf16,
            pltpu.VMEM((2, N_HEADS, 1, t), F32),
            pltpu.VMEM((N_HEADS, 1, t), F32),
            pltpu.VMEM((N_HEADS, LANES, t), F32),
        ],
        compiler_params=_cparams(("parallel", "arbitrary")),
        name="attention",
    )(u, u, u, u, u, u)


def _per_head(v, lanes_per_head, width):
    head = lax.broadcasted_iota(jnp.int32, (1, width), 1) // lanes_per_head
    out = jnp.zeros((v.shape[0], width), F32)
    for h in range(N_HEADS):
        out = jnp.where(head == h, v[:, h:h + 1], out)
    return out


def _split3(a):
    hi = a.astype(BF16)
    r1 = a - hi.astype(F32)
    mid = r1.astype(BF16)
    lo = (r1 - mid.astype(F32)).astype(BF16)
    return hi, mid, lo


def _ssd_kernel(ua_ref, z_ref, xbc_ref, dt_ref, cwa_ref, cwc_ref, cbias_ref, dtb_ref, alog_ref,
                dskip_ref, ng_ref, ya_ref, yc_ref, bufa, bufc, hst):
    t = SSM_CHUNK
    w = BRANCH_W

    @pl.when(pl.program_id(1) == 0)
    def _():
        bufa[0:HIST, :] = jnp.zeros((HIST, w), F32)
        bufc[0:HIST, :] = jnp.zeros((HIST, SSM_CONV_DIM), F32)
        hst[...] = jnp.zeros_like(hst)

    ua = ua_ref[...].astype(F32)
    bufa[HIST:, :] = ua[:, 2 * w:] * ua[:, :w]
    conv = cwa_ref[0:1, :] * bufa[pl.ds(HIST - SC_K + 1, t), :]
    for kk in range(1, SC_K):
        conv = conv + cwa_ref[kk:kk + 1, :] * bufa[pl.ds(HIST - SC_K + 1 + kk, t), :]
    ya_ref[...] = (ua[:, w:2 * w] * conv).astype(ya_ref.dtype)
    bufa[0:HIST, :] = bufa[t:t + HIST, :]

    bufc[HIST:, :] = xbc_ref[...].astype(F32)
    xc = cbias_ref[...] + cwc_ref[0:1, :] * bufc[pl.ds(HIST - SSM_CONV_K + 1, t), :]
    for kk in range(1, SSM_CONV_K):
        xc = xc + cwc_ref[kk:kk + 1, :] * bufc[pl.ds(HIST - SSM_CONV_K + 1 + kk, t), :]
    bufc[0:HIST, :] = bufc[t:t + HIST, :]
    xc = xc * _sigmoid(xc)
    xs = xc[:, :w]
    b_in = xc[:, w:w + LANES]
    c_in = xc[:, w + LANES:]

    dtp = dt_ref[...] + dtb_ref[...]
    dt = jnp.maximum(dtp, 0.0) + jnp.log(1.0 + jnp.exp(-jnp.abs(dtp)))
    a = dt * (-jnp.exp(alog_ref[...]))
    row = lax.broadcasted_iota(jnp.int32, (t, t), 0)
    col = lax.broadcasted_iota(jnp.int32, (t, t), 1)
    causal = row >= col
    tri = jnp.where(causal, 1.0, 0.0).astype(BF16)
    a_hi, a_mid, a_lo = _split3(a)
    acs = (jnp.dot(tri, a_hi, preferred_element_type=F32)
           + jnp.dot(tri, a_mid, preferred_element_type=F32)
           + jnp.dot(tri, a_lo, preferred_element_type=F32))
    acs_t = acs.T
    acs_x = _per_head(acs, HEAD_DIM, w)
    last_x = acs_x[t - 1:t, :]
    x_dt = xs * _per_head(dt, HEAD_DIM, w)
    to_end_x = jnp.exp(last_x - acs_x)
    from_start_x = jnp.exp(acs_x)
    chunk_decay_x = jnp.exp(last_x)

    lane = lax.broadcasted_iota(jnp.int32, (1, LANES), 1)
    low = lane < HEAD_DIM
    c_bf = c_in.astype(BF16)
    nt = (((1,), (1,)), ((), ()))
    for g in range(SSM_GROUPS):
        gmask = low if g == 0 else jnp.logical_not(low)
        sl = slice(g * LANES, (g + 1) * LANES)
        b_g = jnp.where(gmask, b_in, 0.0)
        cb = lax.dot_general(c_bf, b_g.astype(BF16), nt, preferred_element_type=F32)
        xg = x_dt[:, sl]
        y = jnp.zeros((t, LANES), F32)
        for e in range(2):
            h = 2 * g + e
            seg = acs[:, h:h + 1] - acs_t[h:h + 1, :]
            decay = jnp.exp(jnp.where(causal, seg, -jnp.inf))
            emask = low if e == 0 else jnp.logical_not(low)
            xe = jnp.where(emask, xg, 0.0).astype(BF16)
            y = y + jnp.dot((cb * decay).astype(BF16), xe, preferred_element_type=F32)
        h_enter = hst[g]
        y = y + jnp.dot(c_bf, h_enter.astype(BF16), preferred_element_type=F32) * from_start_x[:, sl]
        state = jnp.dot(b_g.T.astype(BF16), (xg * to_end_x[:, sl]).astype(BF16),
                        preferred_element_type=F32)
        hst[g] = h_enter * chunk_decay_x[:, sl] + state

        y = y + xs[:, sl] * dskip_ref[:, sl]
        zg = z_ref[:, sl].astype(F32)
        gated = y * (zg * _sigmoid(zg))
        ms = jnp.mean(gated * gated, axis=-1, keepdims=True)
        yc_ref[:, sl] = (gated * lax.rsqrt(ms + RMS_EPS) * ng_ref[:, sl]).astype(yc_ref.dtype)


def _ssd(u, dt, cwa, cwc, cbias, dtb, alog, dskip, ng, bsz, seq):
    t = SSM_CHUNK
    nc = seq // t
    w = BRANCH_W
    small = lambda shape: pl.BlockSpec(shape, lambda b, c: (0, 0))
    return pl.pallas_call(
        _ssd_kernel,
        grid=(bsz, nc),
        in_specs=[
            pl.BlockSpec((t, 3 * w), lambda b, c: (b * nc + c, COL_A)),
            pl.BlockSpec((t, w), lambda b, c: (b * nc + c, COL_Z // 2)),
            pl.BlockSpec((t, SSM_CONV_DIM), lambda b, c: (b * nc + c, COL_XBC * LANES // SSM_CONV_DIM)),
            pl.BlockSpec((t, LANES), lambda b, c: (b * nc + c, 0)),
            small((SC_K, w)), small((SSM_CONV_K, SSM_CONV_DIM)), small((1, SSM_CONV_DIM)),
            small((1, LANES)), small((1, LANES)), small((1, w)), small((1, w)),
        ],
        out_specs=[pl.BlockSpec((t, w), lambda b, c: (b * nc + c, 0)),
                   pl.BlockSpec((t, w), lambda b, c: (b * nc + c, 0))],
        out_shape=[jax.ShapeDtypeStruct((bsz * seq, w), BF16),
                   jax.ShapeDtypeStruct((bsz * seq, w), BF16)],
        scratch_shapes=[
            pltpu.VMEM((t + HIST, w), F32),
            pltpu.VMEM((t + HIST, SSM_CONV_DIM), F32),
            pltpu.VMEM((SSM_GROUPS, LANES, LANES), F32),
        ],
        compiler_params=_cparams(("parallel", "arbitrary")),
        name="conv_ssd",
    )(u, u, u, dt, cwa, cwc, cbias, dtb, alog, dskip, ng)


def _merge_kernel(x_ref, g_ref, wg_ref, ya_ref, yb_ref, yc_ref, yd_ref, wb_ref, wo_ref, o_ref, h_ref):
    x = x_ref[...]
    ms = jnp.mean(x * x, axis=-1, keepdims=True)
    h_ref[...] = (x * lax.rsqrt(ms + RMS_EPS) * g_ref[...]).astype(BF16)
    merged = None
    for i, y_ref in enumerate((ya_ref, yb_ref, yc_ref, yd_ref)):
        pre = jnp.dot(h_ref[...], wg_ref[:, i * D_MODEL:(i + 1) * D_MODEL], preferred_element_type=F32)
        term = (1.0 + jnp.tanh(0.5 * pre)) * jnp.dot(y_ref[...], wb_ref[i], preferred_element_type=F32)
        merged = term if merged is None else merged + term
    merged = (0.5 * merged).astype(BF16)
    o_ref[...] = x_ref[...] + jnp.dot(merged, wo_ref[...], preferred_element_type=F32)


def _merge(x, g, wg, ya, yb, yc, yd, wb, wo, tm):
    m = x.shape[0]
    ybr = pl.BlockSpec((tm, BRANCH_W), lambda i: (i, 0))
    resident = pl.Buffered(1)
    return pl.pallas_call(
        _merge_kernel,
        grid=(m // tm,),
        in_specs=[
            pl.BlockSpec((tm, D_MODEL), lambda i: (i, 0)),
            pl.BlockSpec((1, D_MODEL), lambda i: (0, 0)),
            pl.BlockSpec((D_MODEL, N_BRANCH * D_MODEL), lambda i: (0, 0), pipeline_mode=resident),
            ybr, ybr, ybr, ybr,
            pl.BlockSpec((N_BRANCH, BRANCH_W, D_MODEL), lambda i: (0, 0, 0), pipeline_mode=resident),
            pl.BlockSpec((D_MODEL, D_MODEL), lambda i: (0, 0), pipeline_mode=resident),
        ],
        out_specs=pl.BlockSpec((tm, D_MODEL), lambda i: (i, 0)),
        out_shape=jax.ShapeDtypeStruct((m, D_MODEL), F32),
        scratch_shapes=[pltpu.VMEM((tm, D_MODEL), BF16)],
        compiler_params=_cparams(("parallel",)),
        name="merge",
    )(x, g, wg, ya, yb, yc, yd, wb, wo)


def _ffn_kernel(x_ref, g_ref, wgu_ref, wd_ref, fg_ref, o_ref, h_ref, act_ref, *, th, final_norm):
    x = x_ref[...]
    ms = jnp.mean(x * x, axis=-1, keepdims=True)
    h_ref[...] = (x * lax.rsqrt(ms + RMS_EPS) * g_ref[...]).astype(BF16)
    for c in range(FFN_HIDDEN // th):
        h = h_ref[...]
        gate = jnp.dot(h, wgu_ref[:, c * th:(c + 1) * th], preferred_element_type=F32)
        up = jnp.dot(h, wgu_ref[:, FFN_HIDDEN + c * th:FFN_HIDDEN + (c + 1) * th],
                     preferred_element_type=F32)
        act_ref[:, c * th:(c + 1) * th] = ((gate * _sigmoid(gate)) * up).astype(BF16)
    y = x_ref[...] + jnp.dot(act_ref[...], wd_ref[...], preferred_element_type=F32)
    if final_norm:
        ms = jnp.mean(y * y, axis=-1, keepdims=True)
        y = y * lax.rsqrt(ms + RMS_EPS) * fg_ref[...]
    o_ref[...] = y


def _ffn(x, g, wgu, wd, final_g, tm, th, final_norm):
    m = x.shape[0]
    resident = pl.Buffered(1)
    return pl.pallas_call(
        functools.partial(_ffn_kernel, th=th, final_norm=final_norm),
        grid=(m // tm,),
        in_specs=[
            pl.BlockSpec((tm, D_MODEL), lambda i: (i, 0)),
            pl.BlockSpec((1, D_MODEL), lambda i: (0, 0)),
            pl.BlockSpec((D_MODEL, 2 * FFN_HIDDEN), lambda i: (0, 0), pipeline_mode=resident),
            pl.BlockSpec((FFN_HIDDEN, D_MODEL), lambda i: (0, 0), pipeline_mode=resident),
            pl.BlockSpec((1, D_MODEL), lambda i: (0, 0)),
        ],
        out_specs=pl.BlockSpec((tm, D_MODEL), lambda i: (i, 0)),
        out_shape=jax.ShapeDtypeStruct((m, D_MODEL), F32),
        scratch_shapes=[pltpu.VMEM((tm, D_MODEL), BF16), pltpu.VMEM((tm, FFN_HIDDEN), BF16)],
        compiler_params=_cparams(("parallel",)),
        name="ffn",
    )(x, g, wgu, wd, final_g)


def _row_tile(m, want):
    while m % want:
        want //= 2
    return want


def _pad_lanes(v):
    return jnp.pad(v.astype(F32), (0, LANES - v.shape[0]))[None, :]


def kernel(x, norm1_g, w_in, conv_a_w, ssm_conv_w, ssm_conv_b, ssm_dt_bias, ssm_a_log, ssm_d,
           ssm_norm_g, w_branch, w_o, norm2_g, w_gate_up, w_down, final_g):
    bsz, seq, _ = x.shape
    assert seq % MOBA_BLOCK == 0 and seq % SSM_CHUNK == 0
    depth = w_in.shape[0]
    assert depth >= 1
    m = bsz * seq
    xbc_col = 3 * BRANCH_W + 3 * BRANCH_W + BRANCH_W
    dt_col = xbc_col + SSM_CONV_DIM
    d_col = dt_col + N_HEADS
    g_col = d_col + 3 * BRANCH_W
    h = x.reshape(m, D_MODEL)
    tm = _row_tile(m, 512)
    for l in range(depth):
        w_main = jnp.concatenate([w_in[l, :, :xbc_col], w_in[l, :, d_col:d_col + BRANCH_W],
                                  w_in[l, :, xbc_col:dt_col], w_in[l, :, d_col + BRANCH_W:g_col]],
                                 axis=1).astype(BF16)
        w_gate = w_in[l, :, g_col:].astype(BF16)
        w_dt = jnp.pad(w_in[l, :, dt_col:dt_col + N_HEADS], ((0, 0), (0, LANES - N_HEADS))).astype(BF16)
        u, dt = _inproj(h, norm1_g[l][None, :], w_main, w_dt, tm, 1024)
        y_b, y_d = _attention(u, bsz, seq)
        y_a, y_c = _ssd(u, dt, conv_a_w[l], ssm_conv_w[l], ssm_conv_b[l][None, :],
                        _pad_lanes(ssm_dt_bias[l]), _pad_lanes(ssm_a_log[l]),
                        jnp.repeat(ssm_d[l], HEAD_DIM)[None, :], ssm_norm_g[l][None, :], bsz, seq)
        h = _merge(h, norm1_g[l][None, :], w_gate, y_a, y_b, y_c, y_d,
                   w_branch[l].astype(BF16), w_o[l].astype(BF16), tm)
        h = _ffn(h, norm2_g[l][None, :], w_gate_up[l].astype(BF16), w_down[l].astype(BF16),
                 final_g[None, :], tm, 256, final_norm=(l == depth - 1))
    return h.reshape(bsz, seq, D_MODEL)
```

```python
import functools

import jax
import jax.numpy as jnp
from jax import lax
from jax.experimental import pallas as pl
from jax.experimental.pallas import tpu as pltpu

F32 = jnp.float32
BF16 = jnp.bfloat16

D_MODEL = 1024
HEAD_DIM = 64
BRANCH_W = 256
N_BRANCH = 4
N_HEADS = 4
SC_K = 3
SSM_GROUPS = 2
SSM_STATE = 64
SSM_CONV_K = 4
SSM_CHUNK = 256
SSM_CONV_DIM = BRANCH_W + 2 * SSM_GROUPS * SSM_STATE
MOBA_BLOCK = 256
MOBA_TOPK = 3
FFN_HIDDEN = 2816
RMS_EPS = 1e-6

LANES = 128
HIST = 8
BF16_ROWS = 16
NEG = -1e30

N_ATT = 1536
COL_SB_Q, COL_SB_K, COL_SB_V = 0, 2, 4
COL_MO_Q, COL_MO_K, COL_MO_V = 6, 8, 10
N_LOC = 1536
LOC_Z = 3 * BRANCH_W
LOC_XBC = 4 * BRANCH_W
N_PACK = N_ATT + N_LOC

VMEM_LIMIT = 56 * 1024 * 1024


def _cparams(sem):
    return pltpu.CompilerParams(dimension_semantics=sem, vmem_limit_bytes=VMEM_LIMIT)


def _sigmoid(x):
    return 1.0 / (1.0 + jnp.exp(-x))


def _neg_abs(x):
    return pltpu.bitcast(pltpu.bitcast(x, jnp.uint32) | jnp.uint32(0x80000000), F32)


def _attn_kernel(qs_ref, ks_ref, vs_ref, qm_ref, km_ref, vm_ref, os_ref, om_ref,
                 s_vt, s_z, s_sp, s_d, s_w, s_r, s_acc,
                 m_vt, m_mean, m_sel, m_z, m_p, m_alpha, m_max, m_acc, *, nb, ke):
    t = MOBA_BLOCK
    qi = pl.program_id(1)
    n_tiles = qi + 1
    ones_row = (HEAD_DIM, 0)
    rows = lax.broadcasted_iota(jnp.int32, (LANES, t), 0)
    head0_rows = rows < HEAD_DIM
    heads = range(N_HEADS)

    def pair_lanes(h):
        return slice((h // 2) * LANES, (h // 2 + 1) * LANES)

    def own_rows(h, x, other):
        return jnp.where(head0_rows, x, other) if h % 2 == 0 else jnp.where(head0_rows, other, x)

    @pl.when(qi == 0)
    def _():
        def tr(c, carry):
            kv = pl.ds(pl.multiple_of(c * t, t), t)
            for h in heads:
                vt = vs_ref[kv, pair_lanes(h)].astype(F32).T
                s_vt[h, c] = own_rows(h, vt, 0.0).astype(BF16)
                vt = vm_ref[kv, pair_lanes(h)].astype(F32).T
                m_vt[h, c] = own_rows(h, vt, jnp.where(rows == ones_row[h % 2], 1.0, 0.0)).astype(BF16)
            m_mean[pl.ds(c, 1), :] = jnp.sum(km_ref[kv, :].astype(F32), axis=0, keepdims=True) * (1.0 / t)
            return carry

        m_mean[...] = jnp.zeros_like(m_mean)
        lax.fori_loop(0, nb, tr, 0)

    def split_heads(q_t):
        return tuple(own_rows(h, q_t[(h // 2) * LANES:(h // 2 + 1) * LANES, :], 0.0).astype(BF16) for h in heads)

    s_qt = split_heads((qs_ref[...].astype(F32) * (HEAD_DIM ** -0.5)).T)
    m_qt = split_heads(qm_ref[...].astype(F32).T)
    m_qst = tuple(x * (HEAD_DIM ** -0.5) for x in m_qt)

    mean = m_mean[...]
    mean_hi = mean.astype(BF16)
    mean_lo = (mean - mean_hi.astype(F32)).astype(BF16)
    blk = lax.broadcasted_iota(jnp.int32, (m_mean.shape[0], t), 0).astype(F32)
    qif = qi.astype(F32)
    for h in heads:
        gate = (jnp.dot(mean_hi[:, pair_lanes(h)], m_qt[h], preferred_element_type=F32)
                + jnp.dot(mean_lo[:, pair_lanes(h)], m_qt[h], preferred_element_type=F32))
        g = jnp.where(blk < qif, gate, -jnp.inf)
        sel = jnp.zeros_like(gate)
        for r in range(ke):
            mx = jnp.max(g, axis=0, keepdims=True)
            idx = jnp.min(jnp.where(g == mx, blk, 1e9), axis=0, keepdims=True)
            hit = blk == idx
            sel = jnp.where(jnp.logical_and(hit, qif > r), 1.0, sel)
            g = jnp.where(hit, -jnp.inf, g)
        m_sel[h] = sel

    row = lax.broadcasted_iota(jnp.int32, (t, t), 0)
    col = lax.broadcasted_iota(jnp.int32, (t, t), 1)
    later_t = jnp.where(col > row, 1.0, 0.0).astype(BF16)

    def tile_of(p):
        return jnp.maximum(qi - p, 0)

    def key_rows(p):
        return pl.ds(pl.multiple_of(tile_of(p) * t, t), t)

    def s_score(p, par):
        k = ks_ref[key_rows(p), :]
        for h in heads:
            s_z[par, h] = jnp.dot(k[:, pair_lanes(h)], s_qt[h], preferred_element_type=F32)

    def s_softplus(par, masked):
        for h in heads:
            z = s_z[par, h]
            if masked:
                z = jnp.where(row < col, z, NEG)
            sp = jnp.maximum(z, 0.0) + jnp.log(1.0 + jnp.exp(_neg_abs(z)))
            s_sp[par, h] = sp.astype(BF16)
            s_d[par, h] = z - sp

    def s_weights(par):
        for h in heads:
            sp = s_sp[par, h]
            between = jnp.dot(later_t, sp, preferred_element_type=F32)
            r = s_r[h]
            s_w[par, h] = jnp.exp(s_d[par, h] - between - r).astype(BF16)
            s_r[h] = r + between[0:1, :] + sp[0:1, :].astype(F32)

    def s_value(p, par):
        tile = tile_of(p)
        for i in range(N_HEADS // 2):
            s_acc[i] += (jnp.dot(s_vt[2 * i, tile], s_w[par, 2 * i], preferred_element_type=F32)
                         + jnp.dot(s_vt[2 * i + 1, tile], s_w[par, 2 * i + 1], preferred_element_type=F32))

    def m_score(p, par):
        k = km_ref[key_rows(p), :]
        for h in heads:
            m_z[par, h] = jnp.dot(k[:, pair_lanes(h)], m_qst[h], preferred_element_type=F32)

    def m_softmax(p, par, own):
        for h in heads:
            s = m_z[par, h]
            if own:
                s = jnp.where(row <= col, s, -jnp.inf)
                m_new = jnp.max(s, axis=0, keepdims=True)
                shift = m_new
                m_alpha[par, h] = jnp.zeros((1, t), F32)
            else:
                chosen = m_sel[h, pl.ds(tile_of(p), 1), :] > 0.5
                m_old = m_max[h]
                m_tile = jnp.max(s, axis=0, keepdims=True)
                m_new = jnp.where(chosen, jnp.maximum(m_old, m_tile), m_old)
                shift = jnp.where(chosen, m_new, jnp.inf)
                m_alpha[par, h] = jnp.exp(m_old - m_new)
            m_p[par, h] = jnp.exp(s - shift).astype(BF16)
            m_max[h] = m_new

    def m_value(p, par):
        tile = tile_of(p)
        for h in heads:
            m_acc[h] = (m_acc[h] * m_alpha[par, h]
                        + jnp.dot(m_vt[h, tile], m_p[par, h], preferred_element_type=F32))

    def step(s, par):
        s_value(s - 3, 1 - par)
        m_value(s - 3, 1 - par)
        s_weights(par)
        m_softmax(s - 2, par, False)
        s_softplus(1 - par, False)
        m_score(s - 1, 1 - par)
        s_score(s, par)

    s_acc[...] = jnp.zeros_like(s_acc)
    s_r[...] = jnp.zeros_like(s_r)
    m_acc[...] = jnp.zeros_like(m_acc)
    s_score(0, 0)
    s_softplus(0, True)
    m_score(0, 0)
    s_score(1, 1)
    s_weights(0)
    m_softmax(0, 0, True)
    s_softplus(1, False)
    m_score(1, 1)
    s_score(2, 0)

    n_full = jnp.maximum(n_tiles - 3, 0)

    def body(i, carry):
        s = 3 + 2 * i
        step(s, 1)
        step(s + 1, 0)
        return carry

    lax.fori_loop(0, n_full // 2, body, 0)

    @pl.when(n_full % 2 == 1)
    def _():
        step(n_tiles - 1, 1)

    def drain(par_t, first):
        if first <= 0:
            s_value(n_tiles - 3, 1 - par_t)
            m_value(n_tiles - 3, 1 - par_t)
            s_weights(par_t)
            m_softmax(n_tiles - 2, par_t, False)
            s_softplus(1 - par_t, False)
            m_score(n_tiles - 1, 1 - par_t)
        if first <= 1:
            s_value(n_tiles - 2, par_t)
            m_value(n_tiles - 2, par_t)
            s_weights(1 - par_t)
            m_softmax(n_tiles - 1, 1 - par_t, False)
        s_value(n_tiles - 1, 1 - par_t)
        m_value(n_tiles - 1, 1 - par_t)

    for par_t in range(2):
        @pl.when(jnp.logical_and(n_tiles >= 3, n_tiles % 2 == par_t))
        def _():
            drain(par_t, 0)

    @pl.when(n_tiles == 2)
    def _():
        drain(0, 1)

    @pl.when(n_tiles == 1)
    def _():
        drain(1, 2)

    for i in range(N_HEADS // 2):
        sl = slice(i * LANES, (i + 1) * LANES)
        os_ref[:, sl] = s_acc[i].T.astype(os_ref.dtype)
        a0 = m_acc[2 * i]
        a1 = m_acc[2 * i + 1]
        out_t = jnp.where(head0_rows, a0 / a0[ones_row[0]:ones_row[0] + 1, :],
                          a1 / a1[ones_row[1]:ones_row[1] + 1, :])
        om_ref[:, sl] = out_t.T.astype(om_ref.dtype)


def _attention(u, bsz, seq):
    t = MOBA_BLOCK
    w = BRANCH_W
    nb = seq // t
    nb_rows = -(-nb // BF16_ROWS) * BF16_ROWS
    ke = max(1, min(MOBA_TOPK, nb - 1))
    tile_f32 = pltpu.VMEM((2, N_HEADS, t, t), F32)
    tile_bf16 = pltpu.VMEM((2, N_HEADS, t, t), BF16)
    vt_tiles = pltpu.VMEM((N_HEADS, nb, LANES, t), BF16)
    resident = pl.Buffered(1)

    def q_spec(col):
        return pl.BlockSpec((t, w), lambda b, qi: (b * nb + qi, col * LANES // w))

    def kv_spec(col):
        return pl.BlockSpec((seq, w), lambda b, qi: (b, col * LANES // w), pipeline_mode=resident)

    out_spec = pl.BlockSpec((t, w), lambda b, qi: (b * nb + qi, 0))
    out_shape = jax.ShapeDtypeStruct((bsz * seq, w), BF16)
    return pl.pallas_call(
        functools.partial(_attn_kernel, nb=nb, ke=ke),
        grid=(bsz, nb),
        in_specs=[q_spec(COL_SB_Q), kv_spec(COL_SB_K), kv_spec(COL_SB_V),
                  q_spec(COL_MO_Q), kv_spec(COL_MO_K), kv_spec(COL_MO_V)],
        out_specs=[out_spec, out_spec],
        out_shape=[out_shape, out_shape],
        scratch_shapes=[
            vt_tiles,
            tile_f32,
            tile_bf16,
            tile_f32,
            tile_bf16,
            pltpu.VMEM((N_HEADS, 1, t), F32),
            pltpu.VMEM((N_HEADS // 2, LANES, t), F32),
            vt_tiles,
            pltpu.VMEM((nb_rows, w), F32),
            pltpu.VMEM((N_HEADS, nb_rows, t), F32),
            tile_f32,
            tile_bf16,
            pltpu.VMEM((2, N_HEADS, 1, t), F32),
            pltpu.VMEM((N_HEADS, 1, t), F32),
            pltpu.VMEM((N_HEADS, LANES, t), F32),
        ],
        compiler_params=_cparams(("parallel", "arbitrary")),
        name="attention",
    )(u, u, u, u, u, u)


def _per_head(v, lanes_per_head, width):
    head = lax.broadcasted_iota(jnp.int32, (1, width), 1) // lanes_per_head
    out = jnp.zeros((v.shape[0], width), F32)
    for h in range(N_HEADS):
        out = jnp.where(head == h, v[:, h:h + 1], out)
    return out


def _split3(a):
    hi = a.astype(BF16)
    r1 = a - hi.astype(F32)
    mid = r1.astype(BF16)
    lo = (r1 - mid.astype(F32)).astype(BF16)
    return hi, mid, lo


def _mixer_chunk(loc_ref, dt_ref, rows, first, cwa_ref, cwc_ref, cbias_ref, dtb_ref, alog_ref,
                 dskip_ref, ng_ref, ya_ref, yc_ref, bufa, bufc, hst):
    t = SSM_CHUNK
    w = BRANCH_W

    ua = loc_ref[rows, 0:3 * w].astype(F32)
    bufa[0:HIST, :] = jnp.where(first, 0.0, bufa[0:HIST, :])
    bufa[HIST:, :] = ua[:, 2 * w:] * ua[:, :w]
    conv = cwa_ref[0:1, :] * bufa[pl.ds(HIST - SC_K + 1, t), :]
    for kk in range(1, SC_K):
        conv = conv + cwa_ref[kk:kk + 1, :] * bufa[pl.ds(HIST - SC_K + 1 + kk, t), :]
    ya_ref[rows, :] = (ua[:, w:2 * w] * conv).astype(ya_ref.dtype)
    bufa[0:HIST, :] = bufa[t:t + HIST, :]
    yield

    bufc[0:HIST, :] = jnp.where(first, 0.0, bufc[0:HIST, :])
    bufc[HIST:, :] = loc_ref[rows, LOC_XBC:LOC_XBC + SSM_CONV_DIM].astype(F32)
    xc = cbias_ref[...] + cwc_ref[0:1, :] * bufc[pl.ds(HIST - SSM_CONV_K + 1, t), :]
    for kk in range(1, SSM_CONV_K):
        xc = xc + cwc_ref[kk:kk + 1, :] * bufc[pl.ds(HIST - SSM_CONV_K + 1 + kk, t), :]
    bufc[0:HIST, :] = bufc[t:t + HIST, :]
    xc = xc * _sigmoid(xc)
    xs = xc[:, :w]
    b_in = xc[:, w:w + LANES]
    c_in = xc[:, w + LANES:]
    yield

    dtp = dt_ref[rows, :] + dtb_ref[...]
    dt = jnp.maximum(dtp, 0.0) + jnp.log(1.0 + jnp.exp(-jnp.abs(dtp)))
    a = dt * (-jnp.exp(alog_ref[...]))
    row = lax.broadcasted_iota(jnp.int32, (t, t), 0)
    col = lax.broadcasted_iota(jnp.int32, (t, t), 1)
    causal = row >= col
    tri = jnp.where(causal, 1.0, 0.0).astype(BF16)
    a_hi, a_mid, a_lo = _split3(a)
    acs = (jnp.dot(tri, a_hi, preferred_element_type=F32)
           + jnp.dot(tri, a_mid, preferred_element_type=F32)
           + jnp.dot(tri, a_lo, preferred_element_type=F32))
    acs_t = acs.T
    acs_x = _per_head(acs, HEAD_DIM, w)
    last_x = acs_x[t - 1:t, :]
    x_dt = xs * _per_head(dt, HEAD_DIM, w)
    to_end_x = jnp.exp(last_x - acs_x)
    from_start_x = jnp.exp(acs_x)
    chunk_decay_x = jnp.exp(last_x)
    yield

    lane = lax.broadcasted_iota(jnp.int32, (1, LANES), 1)
    low = lane < HEAD_DIM
    c_bf = c_in.astype(BF16)
    nt = (((1,), (1,)), ((), ()))
    for g in range(SSM_GROUPS):
        gmask = low if g == 0 else jnp.logical_not(low)
        sl = slice(g * LANES, (g + 1) * LANES)
        b_g = jnp.where(gmask, b_in, 0.0)
        cb = lax.dot_general(c_bf, b_g.astype(BF16), nt, preferred_element_type=F32)
        xg = x_dt[:, sl]
        y = jnp.zeros((t, LANES), F32)
        for e in range(2):
            h = 2 * g + e
            seg = acs[:, h:h + 1] - acs_t[h:h + 1, :]
            decay = jnp.exp(jnp.where(causal, seg, -jnp.inf))
            emask = low if e == 0 else jnp.logical_not(low)
            xe = jnp.where(emask, xg, 0.0).astype(BF16)
            y = y + jnp.dot((cb * decay).astype(BF16), xe, preferred_element_type=F32)
        yield
        h_enter = jnp.where(first, 0.0, hst[g])
        y = y + jnp.dot(c_bf, h_enter.astype(BF16), preferred_element_type=F32) * from_start_x[:, sl]
        state = jnp.dot(b_g.T.astype(BF16), (xg * to_end_x[:, sl]).astype(BF16),
                        preferred_element_type=F32)
        hst[g] = h_enter * chunk_decay_x[:, sl] + state

        y = y + xs[:, sl] * dskip_ref[:, sl]
        zg = loc_ref[rows, LOC_Z + g * LANES:LOC_Z + (g + 1) * LANES].astype(F32)
        gated = y * (zg * _sigmoid(zg))
        ms = jnp.mean(gated * gated, axis=-1, keepdims=True)
        yc_ref[rows, sl] = (gated * lax.rsqrt(ms + RMS_EPS) * ng_ref[:, sl]).astype(yc_ref.dtype)
        yield


def _inproj_mixers_kernel(x_ref, g_ref, w_ref, wdt_ref, cwa_ref, cwc_ref, cbias_ref, dtb_ref, alog_ref,
                          dskip_ref, ng_ref, u_ref, ya_ref, yc_ref,
                          h_ref, loc_ref, dt_ref, loc_new, dt_new, bufa, bufc, hst, *, tn, chunks_per_seq):
    i = pl.program_id(0)
    tm = x_ref.shape[0]
    chunks_per_tile = tm // SSM_CHUNK

    @pl.when(i == 0)
    def _():
        loc_ref[...] = jnp.zeros_like(loc_ref)
        dt_ref[...] = jnp.zeros_like(dt_ref)
        bufa[...] = jnp.zeros_like(bufa)
        bufc[...] = jnp.zeros_like(bufc)
        hst[...] = jnp.zeros_like(hst)

    def mixer_phases():
        for c in range(chunks_per_tile):
            chunk = (i - 1) * chunks_per_tile + c
            first = lax.rem(chunk + chunks_per_seq, chunks_per_seq) == 0
            yield from _mixer_chunk(loc_ref, dt_ref, slice(c * SSM_CHUNK, (c + 1) * SSM_CHUNK), first,
                                    cwa_ref, cwc_ref, cbias_ref, dtb_ref, alog_ref, dskip_ref, ng_ref,
                                    ya_ref, yc_ref, bufa, bufc, hst)

    mixers = mixer_phases()
    x = x_ref[...]
    ms = jnp.mean(x * x, axis=-1, keepdims=True)
    h_ref[...] = (x * lax.rsqrt(ms + RMS_EPS) * g_ref[...]).astype(BF16)
    dt_new[...] = jnp.dot(h_ref[...], wdt_ref[...], preferred_element_type=F32)
    for c in range(N_PACK // tn):
        next(mixers, None)
        sl = slice(c * tn, (c + 1) * tn)
        acc = jnp.dot(h_ref[...], w_ref[:, sl], preferred_element_type=F32).astype(BF16)
        if c * tn < N_ATT:
            u_ref[:, sl] = acc
        else:
            loc_new[:, c * tn - N_ATT:(c + 1) * tn - N_ATT] = acc
    for _ in mixers:
        pass

    loc_ref[...] = loc_new[...]
    dt_ref[...] = dt_new[...]


def _inproj_mixers(x, g, w, wdt, cwa, cwc, cbias, dtb, alog, dskip, ng, seq, tm, tn):
    m = x.shape[0]
    n = m // tm
    assert tm % SSM_CHUNK == 0 and seq % tm == 0 and N_ATT % tn == 0 and N_LOC % tn == 0
    wbr = BRANCH_W
    resident = pl.Buffered(1)
    small = lambda shape: pl.BlockSpec(shape, lambda i: (0, 0))
    this_tile = lambda i: (jnp.minimum(i, n - 1), 0)
    prev_tile = lambda i: (jnp.maximum(i - 1, 0), 0)
    return pl.pallas_call(
        functools.partial(_inproj_mixers_kernel, tn=tn, chunks_per_seq=seq // SSM_CHUNK),
        grid=(n + 1,),
        in_specs=[
            pl.BlockSpec((tm, D_MODEL), this_tile),
            small((1, D_MODEL)),
            pl.BlockSpec((D_MODEL, N_PACK), lambda i: (0, 0), pipeline_mode=resident),
            small((D_MODEL, LANES)),
            small((SC_K, wbr)), small((SSM_CONV_K, SSM_CONV_DIM)), small((1, SSM_CONV_DIM)),
            small((1, LANES)), small((1, LANES)), small((1, wbr)), small((1, wbr)),
        ],
        out_specs=[
            pl.BlockSpec((tm, N_ATT), this_tile),
            pl.BlockSpec((tm, wbr), prev_tile),
            pl.BlockSpec((tm, wbr), prev_tile),
        ],
        out_shape=[jax.ShapeDtypeStruct((m, N_ATT), BF16),
                   jax.ShapeDtypeStruct((m, wbr), BF16),
                   jax.ShapeDtypeStruct((m, wbr), BF16)],
        scratch_shapes=[
            pltpu.VMEM((tm, D_MODEL), BF16),
            pltpu.VMEM((tm, N_LOC), BF16),
            pltpu.VMEM((tm, LANES), F32),
            pltpu.VMEM((tm, N_LOC), BF16),
            pltpu.VMEM((tm, LANES), F32),
            pltpu.VMEM((SSM_CHUNK + HIST, wbr), F32),
            pltpu.VMEM((SSM_CHUNK + HIST, SSM_CONV_DIM), F32),
            pltpu.VMEM((SSM_GROUPS, LANES, LANES), F32),
        ],
        compiler_params=_cparams(("arbitrary",)),
        name="inproj_mixers",
    )(x, g, w, wdt, cwa, cwc, cbias, dtb, alog, dskip, ng)


def _merge_kernel(x_ref, g_ref, wg_ref, ya_ref, yb_ref, yc_ref, yd_ref, wb_ref, wo_ref, o_ref, h_ref):
    x = x_ref[...]
    ms = jnp.mean(x * x, axis=-1, keepdims=True)
    h_ref[...] = (x * lax.rsqrt(ms + RMS_EPS) * g_ref[...]).astype(BF16)
    merged = None
    for i, y_ref in enumerate((ya_ref, yb_ref, yc_ref, yd_ref)):
        pre = jnp.dot(h_ref[...], wg_ref[:, i * D_MODEL:(i + 1) * D_MODEL], preferred_element_type=F32)
        term = (1.0 + jnp.tanh(0.5 * pre)) * jnp.dot(y_ref[...], wb_ref[i], preferred_element_type=F32)
        merged = term if merged is None else merged + term
    merged = (0.5 * merged).astype(BF16)
    o_ref[...] = x_ref[...] + jnp.dot(merged, wo_ref[...], preferred_element_type=F32)


def _merge(x, g, wg, ya, yb, yc, yd, wb, wo, tm):
    m = x.shape[0]
    ybr = pl.BlockSpec((tm, BRANCH_W), lambda i: (i, 0))
    resident = pl.Buffered(1)
    return pl.pallas_call(
        _merge_kernel,
        grid=(m // tm,),
        in_specs=[
            pl.BlockSpec((tm, D_MODEL), lambda i: (i, 0)),
            pl.BlockSpec((1, D_MODEL), lambda i: (0, 0)),
            pl.BlockSpec((D_MODEL, N_BRANCH * D_MODEL), lambda i: (0, 0), pipeline_mode=resident),
            ybr, ybr, ybr, ybr,
            pl.BlockSpec((N_BRANCH, BRANCH_W, D_MODEL), lambda i: (0, 0, 0), pipeline_mode=resident),
            pl.BlockSpec((D_MODEL, D_MODEL), lambda i: (0, 0), pipeline_mode=resident),
        ],
        out_specs=pl.BlockSpec((tm, D_MODEL), lambda i: (i, 0)),
        out_shape=jax.ShapeDtypeStruct((m, D_MODEL), F32),
        scratch_shapes=[pltpu.VMEM((tm, D_MODEL), BF16)],
        compiler_params=_cparams(("parallel",)),
        name="merge",
    )(x, g, wg, ya, yb, yc, yd, wb, wo)


def _ffn_kernel(x_ref, g_ref, wgu_ref, wd_ref, fg_ref, o_ref, h_ref, act_ref, *, th, final_norm):
    x = x_ref[...]
    ms = jnp.mean(x * x, axis=-1, keepdims=True)
    h_ref[...] = (x * lax.rsqrt(ms + RMS_EPS) * g_ref[...]).astype(BF16)
    for c in range(FFN_HIDDEN // th):
        h = h_ref[...]
        gate = jnp.dot(h, wgu_ref[:, c * th:(c + 1) * th], preferred_element_type=F32)
        up = jnp.dot(h, wgu_ref[:, FFN_HIDDEN + c * th:FFN_HIDDEN + (c + 1) * th],
                     preferred_element_type=F32)
        act_ref[:, c * th:(c + 1) * th] = ((gate * _sigmoid(gate)) * up).astype(BF16)
    y = x_ref[...] + jnp.dot(act_ref[...], wd_ref[...], preferred_element_type=F32)
    if final_norm:
        ms = jnp.mean(y * y, axis=-1, keepdims=True)
        y = y * lax.rsqrt(ms + RMS_EPS) * fg_ref[...]
    o_ref[...] = y


def _ffn(x, g, wgu, wd, final_g, tm, th, final_norm):
    m = x.shape[0]
    resident = pl.Buffered(1)
    return pl.pallas_call(
        functools.partial(_ffn_kernel, th=th, final_norm=final_norm),
        grid=(m // tm,),
        in_specs=[
            pl.BlockSpec((tm, D_MODEL), lambda i: (i, 0)),
            pl.BlockSpec((1, D_MODEL), lambda i: (0, 0)),
            pl.BlockSpec((D_MODEL, 2 * FFN_HIDDEN), lambda i: (0, 0), pipeline_mode=resident),
            pl.BlockSpec((FFN_HIDDEN, D_MODEL), lambda i: (0, 0), pipeline_mode=resident),
            pl.BlockSpec((1, D_MODEL), lambda i: (0, 0)),
        ],
        out_specs=pl.BlockSpec((tm, D_MODEL), lambda i: (i, 0)),
        out_shape=jax.ShapeDtypeStruct((m, D_MODEL), F32),
        scratch_shapes=[pltpu.VMEM((tm, D_MODEL), BF16), pltpu.VMEM((tm, FFN_HIDDEN), BF16)],
        compiler_params=_cparams(("parallel",)),
        name="ffn",
    )(x, g, wgu, wd, final_g)


def _row_tile(m, want):
    while m % want:
        want //= 2
    return want


def _pad_lanes(v):
    return jnp.pad(v.astype(F32), (0, LANES - v.shape[0]))[None, :]


def kernel(x, norm1_g, w_in, conv_a_w, ssm_conv_w, ssm_conv_b, ssm_dt_bias, ssm_a_log, ssm_d,
           ssm_norm_g, w_branch, w_o, norm2_g, w_gate_up, w_down, final_g):
    bsz, seq, _ = x.shape
    assert seq % MOBA_BLOCK == 0 and seq % SSM_CHUNK == 0
    depth = w_in.shape[0]
    assert depth >= 1
    m = bsz * seq
    b_col = 3 * BRANCH_W
    z_col = b_col + 3 * BRANCH_W
    dt_col = z_col + BRANCH_W + SSM_CONV_DIM
    d_col = dt_col + N_HEADS
    g_col = d_col + 3 * BRANCH_W
    h = x.reshape(m, D_MODEL)
    tm = _row_tile(m, 512)
    for l in range(depth):
        w_main = jnp.concatenate([w_in[l, :, b_col:z_col], w_in[l, :, d_col:g_col],
                                  w_in[l, :, :b_col], w_in[l, :, z_col:dt_col]], axis=1).astype(BF16)
        w_gate = w_in[l, :, g_col:].astype(BF16)
        w_dt = jnp.pad(w_in[l, :, dt_col:d_col], ((0, 0), (0, LANES - N_HEADS))).astype(BF16)
        u, y_a, y_c = _inproj_mixers(h, norm1_g[l][None, :], w_main, w_dt, conv_a_w[l], ssm_conv_w[l],
                                     ssm_conv_b[l][None, :], _pad_lanes(ssm_dt_bias[l]),
                                     _pad_lanes(ssm_a_log[l]), jnp.repeat(ssm_d[l], HEAD_DIM)[None, :],
                                     ssm_norm_g[l][None, :], seq, tm, 256)
        y_b, y_d = _attention(u, bsz, seq)
        h = _merge(h, norm1_g[l][None, :], w_gate, y_a, y_b, y_c, y_d,
                   w_branch[l].astype(BF16), w_o[l].astype(BF16), tm)
        h = _ffn(h, norm2_g[l][None, :], w_gate_up[l].astype(BF16), w_down[l].astype(BF16),
                 final_g[None, :], tm, 256, final_norm=(l == depth - 1))
    return h.reshape(bsz, seq, D_MODEL)
```

```python
import functools

import jax
import jax.numpy as jnp
from jax import lax
from jax.experimental import pallas as pl
from jax.experimental.pallas import tpu as pltpu

F32 = jnp.float32
BF16 = jnp.bfloat16

D_MODEL = 1024
HEAD_DIM = 64
BRANCH_W = 256
N_BRANCH = 4
N_HEADS = 4
SC_K = 3
SSM_GROUPS = 2
SSM_STATE = 64
SSM_CONV_K = 4
SSM_CHUNK = 256
SSM_CONV_DIM = BRANCH_W + 2 * SSM_GROUPS * SSM_STATE
MOBA_BLOCK = 256
MOBA_TOPK = 3
FFN_HIDDEN = 2816
RMS_EPS = 1e-6

LANES = 128
HIST = 8
BF16_ROWS = 16
NEG = -1e30

N_ATT = 1536
COL_SB_Q, COL_SB_K, COL_SB_V = 0, 2, 4
COL_MO_Q, COL_MO_K, COL_MO_V = 6, 8, 10
N_LOC = 1536
LOC_Z = 3 * BRANCH_W
LOC_XBC = 4 * BRANCH_W
N_PACK = N_ATT + N_LOC

VMEM_LIMIT = 56 * 1024 * 1024


def _cparams(sem):
    return pltpu.CompilerParams(dimension_semantics=sem, vmem_limit_bytes=VMEM_LIMIT)


def _sigmoid(x):
    return 1.0 / (1.0 + jnp.exp(-x))


def _neg_abs(x):
    return pltpu.bitcast(pltpu.bitcast(x, jnp.uint32) | jnp.uint32(0x80000000), F32)


def _attn_kernel(qs_ref, ks_ref, vs_ref, qm_ref, km_ref, vm_ref, os_ref, om_ref,
                 s_vt, s_sp, s_d, s_w, s_r, s_acc,
                 m_vt, m_mean, m_sel, m_p, m_alpha, m_max, m_acc, *, nb, ke):
    t = MOBA_BLOCK
    qi = pl.program_id(1)
    n_tiles = qi + 1
    ones_row = (HEAD_DIM, 0)
    rows = lax.broadcasted_iota(jnp.int32, (LANES, t), 0)
    head0_rows = rows < HEAD_DIM
    heads = range(N_HEADS)

    def pair_lanes(h):
        return slice((h // 2) * LANES, (h // 2 + 1) * LANES)

    def own_rows(h, x, other):
        return jnp.where(head0_rows, x, other) if h % 2 == 0 else jnp.where(head0_rows, other, x)

    @pl.when(qi == 0)
    def _():
        def tr(c, carry):
            kv = pl.ds(pl.multiple_of(c * t, t), t)
            for h in heads:
                vt = vs_ref[kv, pair_lanes(h)].astype(F32).T
                s_vt[h, c] = own_rows(h, vt, 0.0).astype(BF16)
                vt = vm_ref[kv, pair_lanes(h)].astype(F32).T
                m_vt[h, c] = own_rows(h, vt, jnp.where(rows == ones_row[h % 2], 1.0, 0.0)).astype(BF16)
            m_mean[pl.ds(c, 1), :] = jnp.sum(km_ref[kv, :].astype(F32), axis=0, keepdims=True) * (1.0 / t)
            return carry

        m_mean[...] = jnp.zeros_like(m_mean)
        lax.fori_loop(0, nb, tr, 0)

    def split_heads(q_t):
        return tuple(own_rows(h, q_t[(h // 2) * LANES:(h // 2 + 1) * LANES, :], 0.0).astype(BF16) for h in heads)

    s_qt = split_heads((qs_ref[...].astype(F32) * (HEAD_DIM ** -0.5)).T)
    m_qt = split_heads(qm_ref[...].astype(F32).T)
    m_qst = tuple(x * (HEAD_DIM ** -0.5) for x in m_qt)

    mean = m_mean[...]
    mean_hi = mean.astype(BF16)
    mean_lo = (mean - mean_hi.astype(F32)).astype(BF16)
    blk = lax.broadcasted_iota(jnp.int32, (m_mean.shape[0], t), 0).astype(F32)
    qif = qi.astype(F32)
    for h in heads:
        gate = (jnp.dot(mean_hi[:, pair_lanes(h)], m_qt[h], preferred_element_type=F32)
                + jnp.dot(mean_lo[:, pair_lanes(h)], m_qt[h], preferred_element_type=F32))
        g = jnp.where(blk < qif, gate, -jnp.inf)
        sel = jnp.zeros_like(gate)
        for r in range(ke):
            mx = jnp.max(g, axis=0, keepdims=True)
            idx = jnp.min(jnp.where(g == mx, blk, 1e9), axis=0, keepdims=True)
            hit = blk == idx
            sel = jnp.where(jnp.logical_and(hit, qif > r), 1.0, sel)
            g = jnp.where(hit, -jnp.inf, g)
        m_sel[h] = sel

    row = lax.broadcasted_iota(jnp.int32, (t, t), 0)
    col = lax.broadcasted_iota(jnp.int32, (t, t), 1)
    later_t = jnp.where(col > row, 1.0, 0.0).astype(BF16)

    def tile_of(p):
        return jnp.maximum(qi - p, 0)

    def key_rows(p):
        return pl.ds(pl.multiple_of(tile_of(p) * t, t), t)

    def s_scores(p, par, masked):
        k = ks_ref[key_rows(p), :]
        for h in heads:
            z = jnp.dot(k[:, pair_lanes(h)], s_qt[h], preferred_element_type=F32)
            if masked:
                z = jnp.where(row < col, z, NEG)
            sp = jnp.maximum(z, 0.0) + jnp.log(1.0 + jnp.exp(_neg_abs(z)))
            s_sp[par, h] = sp.astype(BF16)
            s_d[par, h] = z - sp

    def s_weights(par):
        for h in heads:
            sp = s_sp[par, h]
            between = jnp.dot(later_t, sp, preferred_element_type=F32)
            r = s_r[h]
            s_w[par, h] = jnp.exp(s_d[par, h] - between - r).astype(BF16)
            s_r[h] = r + between[0:1, :] + sp[0:1, :].astype(F32)

    def s_value(p, par):
        tile = tile_of(p)
        for i in range(N_HEADS // 2):
            s_acc[i] += (jnp.dot(s_vt[2 * i, tile], s_w[par, 2 * i], preferred_element_type=F32)
                         + jnp.dot(s_vt[2 * i + 1, tile], s_w[par, 2 * i + 1], preferred_element_type=F32))

    def m_scores(p, par, own):
        k = km_ref[key_rows(p), :]
        for h in heads:
            s = jnp.dot(k[:, pair_lanes(h)], m_qst[h], preferred_element_type=F32)
            if own:
                s = jnp.where(row <= col, s, -jnp.inf)
                m_new = jnp.max(s, axis=0, keepdims=True)
                shift = m_new
                m_alpha[par, h] = jnp.zeros((1, t), F32)
            else:
                chosen = m_sel[h, pl.ds(tile_of(p), 1), :] > 0.5
                m_old = m_max[h]
                m_tile = jnp.max(s, axis=0, keepdims=True)
                m_new = jnp.where(chosen, jnp.maximum(m_old, m_tile), m_old)
                shift = jnp.where(chosen, m_new, jnp.inf)
                m_alpha[par, h] = jnp.exp(m_old - m_new)
            m_p[par, h] = jnp.exp(s - shift).astype(BF16)
            m_max[h] = m_new

    def m_value(p, par):
        tile = tile_of(p)
        for h in heads:
            m_acc[h] = (m_acc[h] * m_alpha[par, h]
                        + jnp.dot(m_vt[h, tile], m_p[par, h], preferred_element_type=F32))

    def step(s, par):
        s_value(s - 2, par)
        m_value(s - 2, par)
        s_weights(1 - par)
        m_scores(s - 1, 1 - par, False)
        s_scores(s, par, False)

    s_acc[...] = jnp.zeros_like(s_acc)
    s_r[...] = jnp.zeros_like(s_r)
    m_acc[...] = jnp.zeros_like(m_acc)
    s_scores(0, 0, True)
    s_weights(0)
    m_scores(0, 0, True)
    s_scores(1, 1, False)

    n_full = jnp.maximum(n_tiles - 2, 0)

    def body(i, carry):
        s = 2 + 2 * i
        step(s, 0)
        step(s + 1, 1)
        return carry

    lax.fori_loop(0, n_full // 2, body, 0)

    @pl.when(n_full % 2 == 1)
    def _():
        step(n_tiles - 1, 0)

    def drain(par_t, first):
        if first <= 0:
            s_value(n_tiles - 2, par_t)
            m_value(n_tiles - 2, par_t)
            s_weights(1 - par_t)
            m_scores(n_tiles - 1, 1 - par_t, False)
        s_value(n_tiles - 1, 1 - par_t)
        m_value(n_tiles - 1, 1 - par_t)

    for par_t in range(2):
        @pl.when(jnp.logical_and(n_tiles >= 2, n_tiles % 2 == par_t))
        def _():
            drain(par_t, 0)

    @pl.when(n_tiles == 1)
    def _():
        drain(1, 1)

    for i in range(N_HEADS // 2):
        sl = slice(i * LANES, (i + 1) * LANES)
        os_ref[:, sl] = s_acc[i].T.astype(os_ref.dtype)
        a0 = m_acc[2 * i]
        a1 = m_acc[2 * i + 1]
        out_t = jnp.where(head0_rows, a0 / a0[ones_row[0]:ones_row[0] + 1, :],
                          a1 / a1[ones_row[1]:ones_row[1] + 1, :])
        om_ref[:, sl] = out_t.T.astype(om_ref.dtype)


def _attention(u, bsz, seq):
    t = MOBA_BLOCK
    w = BRANCH_W
    nb = seq // t
    nb_rows = -(-nb // BF16_ROWS) * BF16_ROWS
    ke = max(1, min(MOBA_TOPK, nb - 1))
    tile_f32 = pltpu.VMEM((2, N_HEADS, t, t), F32)
    tile_bf16 = pltpu.VMEM((2, N_HEADS, t, t), BF16)
    vt_tiles = pltpu.VMEM((N_HEADS, nb, LANES, t), BF16)
    resident = pl.Buffered(1)

    def q_spec(col):
        return pl.BlockSpec((t, w), lambda b, qi: (b * nb + qi, col * LANES // w))

    def kv_spec(col):
        return pl.BlockSpec((seq, w), lambda b, qi: (b, col * LANES // w), pipeline_mode=resident)

    out_spec = pl.BlockSpec((t, w), lambda b, qi: (b * nb + qi, 0))
    out_shape = jax.ShapeDtypeStruct((bsz * seq, w), BF16)
    return pl.pallas_call(
        functools.partial(_attn_kernel, nb=nb, ke=ke),
        grid=(bsz, nb),
        in_specs=[q_spec(COL_SB_Q), kv_spec(COL_SB_K), kv_spec(COL_SB_V),
                  q_spec(COL_MO_Q), kv_spec(COL_MO_K), kv_spec(COL_MO_V)],
        out_specs=[out_spec, out_spec],
        out_shape=[out_shape, out_shape],
        scratch_shapes=[
            vt_tiles,
            tile_bf16,
            tile_f32,
            tile_bf16,
            pltpu.VMEM((N_HEADS, 1, t), F32),
            pltpu.VMEM((N_HEADS // 2, LANES, t), F32),
            vt_tiles,
            pltpu.VMEM((nb_rows, w), F32),
            pltpu.VMEM((N_HEADS, nb_rows, t), F32),
            tile_bf16,
            pltpu.VMEM((2, N_HEADS, 1, t), F32),
            pltpu.VMEM((N_HEADS, 1, t), F32),
            pltpu.VMEM((N_HEADS, LANES, t), F32),
        ],
        compiler_params=_cparams(("parallel", "arbitrary")),
        name="attention",
    )(u, u, u, u, u, u)


def _per_head(v, lanes_per_head, width):
    head = lax.broadcasted_iota(jnp.int32, (1, width), 1) // lanes_per_head
    out = jnp.zeros((v.shape[0], width), F32)
    for h in range(N_HEADS):
        out = jnp.where(head == h, v[:, h:h + 1], out)
    return out


def _split3(a):
    hi = a.astype(BF16)
    r1 = a - hi.astype(F32)
    mid = r1.astype(BF16)
    lo = (r1 - mid.astype(F32)).astype(BF16)
    return hi, mid, lo


def _mixer_chunk(loc_ref, dt_ref, rows, first, cwa_ref, cwc_ref, cbias_ref, dtb_ref, alog_ref,
                 dskip_ref, ng_ref, ya_ref, yc_ref, bufa, bufc, hst):
    t = SSM_CHUNK
    w = BRANCH_W

    ua = loc_ref[rows, 0:3 * w].astype(F32)
    bufa[0:HIST, :] = jnp.where(first, 0.0, bufa[0:HIST, :])
    bufa[HIST:, :] = ua[:, 2 * w:] * ua[:, :w]
    conv = cwa_ref[0:1, :] * bufa[pl.ds(HIST - SC_K + 1, t), :]
    for kk in range(1, SC_K):
        conv = conv + cwa_ref[kk:kk + 1, :] * bufa[pl.ds(HIST - SC_K + 1 + kk, t), :]
    ya_ref[rows, :] = (ua[:, w:2 * w] * conv).astype(ya_ref.dtype)
    bufa[0:HIST, :] = bufa[t:t + HIST, :]
    yield

    bufc[0:HIST, :] = jnp.where(first, 0.0, bufc[0:HIST, :])
    bufc[HIST:, :] = loc_ref[rows, LOC_XBC:LOC_XBC + SSM_CONV_DIM].astype(F32)
    xc = cbias_ref[...] + cwc_ref[0:1, :] * bufc[pl.ds(HIST - SSM_CONV_K + 1, t), :]
    for kk in range(1, SSM_CONV_K):
        xc = xc + cwc_ref[kk:kk + 1, :] * bufc[pl.ds(HIST - SSM_CONV_K + 1 + kk, t), :]
    bufc[0:HIST, :] = bufc[t:t + HIST, :]
    xc = xc * _sigmoid(xc)
    xs = xc[:, :w]
    b_in = xc[:, w:w + LANES]
    c_in = xc[:, w + LANES:]
    yield

    dtp = dt_ref[rows, :] + dtb_ref[...]
    dt = jnp.maximum(dtp, 0.0) + jnp.log(1.0 + jnp.exp(-jnp.abs(dtp)))
    a = dt * (-jnp.exp(alog_ref[...]))
    row = lax.broadcasted_iota(jnp.int32, (t, t), 0)
    col = lax.broadcasted_iota(jnp.int32, (t, t), 1)
    causal = row >= col
    tri = jnp.where(causal, 1.0, 0.0).astype(BF16)
    a_hi, a_mid, a_lo = _split3(a)
    acs = (jnp.dot(tri, a_hi, preferred_element_type=F32)
           + jnp.dot(tri, a_mid, preferred_element_type=F32)
           + jnp.dot(tri, a_lo, preferred_element_type=F32))
    acs_t = acs.T
    acs_x = _per_head(acs, HEAD_DIM, w)
    last_x = acs_x[t - 1:t, :]
    x_dt = xs * _per_head(dt, HEAD_DIM, w)
    to_end_x = jnp.exp(last_x - acs_x)
    from_start_x = jnp.exp(acs_x)
    chunk_decay_x = jnp.exp(last_x)
    yield

    lane = lax.broadcasted_iota(jnp.int32, (1, LANES), 1)
    low = lane < HEAD_DIM
    c_bf = c_in.astype(BF16)
    nt = (((1,), (1,)), ((), ()))
    for g in range(SSM_GROUPS):
        gmask = low if g == 0 else jnp.logical_not(low)
        sl = slice(g * LANES, (g + 1) * LANES)
        b_g = jnp.where(gmask, b_in, 0.0)
        cb = lax.dot_general(c_bf, b_g.astype(BF16), nt, preferred_element_type=F32)
        xg = x_dt[:, sl]
        y = jnp.zeros((t, LANES), F32)
        for e in range(2):
            h = 2 * g + e
            seg = acs[:, h:h + 1] - acs_t[h:h + 1, :]
            decay = jnp.exp(jnp.where(causal, seg, -jnp.inf))
            emask = low if e == 0 else jnp.logical_not(low)
            xe = jnp.where(emask, xg, 0.0).astype(BF16)
            y = y + jnp.dot((cb * decay).astype(BF16), xe, preferred_element_type=F32)
        yield
        h_enter = jnp.where(first, 0.0, hst[g])
        y = y + jnp.dot(c_bf, h_enter.astype(BF16), preferred_element_type=F32) * from_start_x[:, sl]
        state = jnp.dot(b_g.T.astype(BF16), (xg * to_end_x[:, sl]).astype(BF16),
                        preferred_element_type=F32)
        hst[g] = h_enter * chunk_decay_x[:, sl] + state

        y = y + xs[:, sl] * dskip_ref[:, sl]
        zg = loc_ref[rows, LOC_Z + g * LANES:LOC_Z + (g + 1) * LANES].astype(F32)
        gated = y * (zg * _sigmoid(zg))
        ms = jnp.mean(gated * gated, axis=-1, keepdims=True)
        yc_ref[rows, sl] = (gated * lax.rsqrt(ms + RMS_EPS) * ng_ref[:, sl]).astype(yc_ref.dtype)
        yield


def _inproj_mixers_kernel(x_ref, g_ref, w_ref, wdt_ref, cwa_ref, cwc_ref, cbias_ref, dtb_ref, alog_ref,
                          dskip_ref, ng_ref, u_ref, ya_ref, yc_ref,
                          h_ref, loc_ref, dt_ref, loc_new, dt_new, bufa, bufc, hst, *, tn, chunks_per_seq):
    i = pl.program_id(0)
    tm = x_ref.shape[0]
    chunks_per_tile = tm // SSM_CHUNK

    @pl.when(i == 0)
    def _():
        loc_ref[...] = jnp.zeros_like(loc_ref)
        dt_ref[...] = jnp.zeros_like(dt_ref)
        bufa[...] = jnp.zeros_like(bufa)
        bufc[...] = jnp.zeros_like(bufc)
        hst[...] = jnp.zeros_like(hst)

    def mixer_phases():
        for c in range(chunks_per_tile):
            chunk = (i - 1) * chunks_per_tile + c
            first = lax.rem(chunk + chunks_per_seq, chunks_per_seq) == 0
            yield from _mixer_chunk(loc_ref, dt_ref, slice(c * SSM_CHUNK, (c + 1) * SSM_CHUNK), first,
                                    cwa_ref, cwc_ref, cbias_ref, dtb_ref, alog_ref, dskip_ref, ng_ref,
                                    ya_ref, yc_ref, bufa, bufc, hst)

    mixers = mixer_phases()
    x = x_ref[...]
    ms = jnp.mean(x * x, axis=-1, keepdims=True)
    h_ref[...] = (x * lax.rsqrt(ms + RMS_EPS) * g_ref[...]).astype(BF16)
    dt_new[...] = jnp.dot(h_ref[...], wdt_ref[...], preferred_element_type=F32)
    for c in range(N_PACK // tn):
        next(mixers, None)
        sl = slice(c * tn, (c + 1) * tn)
        acc = jnp.dot(h_ref[...], w_ref[:, sl], preferred_element_type=F32).astype(BF16)
        if c * tn < N_ATT:
            u_ref[:, sl] = acc
        else:
            loc_new[:, c * tn - N_ATT:(c + 1) * tn - N_ATT] = acc
    for _ in mixers:
        pass

    loc_ref[...] = loc_new[...]
    dt_ref[...] = dt_new[...]


def _inproj_mixers(x, g, w, wdt, cwa, cwc, cbias, dtb, alog, dskip, ng, seq, tm, tn):
    m = x.shape[0]
    n = m // tm
    assert tm % SSM_CHUNK == 0 and seq % tm == 0 and N_ATT % tn == 0 and N_LOC % tn == 0
    wbr = BRANCH_W
    resident = pl.Buffered(1)
    small = lambda shape: pl.BlockSpec(shape, lambda i: (0, 0))
    this_tile = lambda i: (jnp.minimum(i, n - 1), 0)
    prev_tile = lambda i: (jnp.maximum(i - 1, 0), 0)
    return pl.pallas_call(
        functools.partial(_inproj_mixers_kernel, tn=tn, chunks_per_seq=seq // SSM_CHUNK),
        grid=(n + 1,),
        in_specs=[
            pl.BlockSpec((tm, D_MODEL), this_tile),
            small((1, D_MODEL)),
            pl.BlockSpec((D_MODEL, N_PACK), lambda i: (0, 0), pipeline_mode=resident),
            small((D_MODEL, LANES)),
            small((SC_K, wbr)), small((SSM_CONV_K, SSM_CONV_DIM)), small((1, SSM_CONV_DIM)),
            small((1, LANES)), small((1, LANES)), small((1, wbr)), small((1, wbr)),
        ],
        out_specs=[
            pl.BlockSpec((tm, N_ATT), this_tile),
            pl.BlockSpec((tm, wbr), prev_tile),
            pl.BlockSpec((tm, wbr), prev_tile),
        ],
        out_shape=[jax.ShapeDtypeStruct((m, N_ATT), BF16),
                   jax.ShapeDtypeStruct((m, wbr), BF16),
                   jax.ShapeDtypeStruct((m, wbr), BF16)],
        scratch_shapes=[
            pltpu.VMEM((tm, D_MODEL), BF16),
            pltpu.VMEM((tm, N_LOC), BF16),
            pltpu.VMEM((tm, LANES), F32),
            pltpu.VMEM((tm, N_LOC), BF16),
            pltpu.VMEM((tm, LANES), F32),
            pltpu.VMEM((SSM_CHUNK + HIST, wbr), F32),
            pltpu.VMEM((SSM_CHUNK + HIST, SSM_CONV_DIM), F32),
            pltpu.VMEM((SSM_GROUPS, LANES, LANES), F32),
        ],
        compiler_params=_cparams(("arbitrary",)),
        name="inproj_mixers",
    )(x, g, w, wdt, cwa, cwc, cbias, dtb, alog, dskip, ng)


def _merge_kernel(x_ref, g_ref, wg_ref, ya_ref, yb_ref, yc_ref, yd_ref, wb_ref, wo_ref, o_ref, h_ref):
    x = x_ref[...]
    ms = jnp.mean(x * x, axis=-1, keepdims=True)
    h_ref[...] = (x * lax.rsqrt(ms + RMS_EPS) * g_ref[...]).astype(BF16)
    merged = None
    for i, y_ref in enumerate((ya_ref, yb_ref, yc_ref, yd_ref)):
        pre = jnp.dot(h_ref[...], wg_ref[:, i * D_MODEL:(i + 1) * D_MODEL], preferred_element_type=F32)
        term = (1.0 + jnp.tanh(0.5 * pre)) * jnp.dot(y_ref[...], wb_ref[i], preferred_element_type=F32)
        merged = term if merged is None else merged + term
    merged = (0.5 * merged).astype(BF16)
    o_ref[...] = x_ref[...] + jnp.dot(merged, wo_ref[...], preferred_element_type=F32)


def _merge(x, g, wg, ya, yb, yc, yd, wb, wo, tm):
    m = x.shape[0]
    ybr = pl.BlockSpec((tm, BRANCH_W), lambda i: (i, 0))
    resident = pl.Buffered(1)
    return pl.pallas_call(
        _merge_kernel,
        grid=(m // tm,),
        in_specs=[
            pl.BlockSpec((tm, D_MODEL), lambda i: (i, 0)),
            pl.BlockSpec((1, D_MODEL), lambda i: (0, 0)),
            pl.BlockSpec((D_MODEL, N_BRANCH * D_MODEL), lambda i: (0, 0), pipeline_mode=resident),
            ybr, ybr, ybr, ybr,
            pl.BlockSpec((N_BRANCH, BRANCH_W, D_MODEL), lambda i: (0, 0, 0), pipeline_mode=resident),
            pl.BlockSpec((D_MODEL, D_MODEL), lambda i: (0, 0), pipeline_mode=resident),
        ],
        out_specs=pl.BlockSpec((tm, D_MODEL), lambda i: (i, 0)),
        out_shape=jax.ShapeDtypeStruct((m, D_MODEL), F32),
        scratch_shapes=[pltpu.VMEM((tm, D_MODEL), BF16)],
        compiler_params=_cparams(("parallel",)),
        name="merge",
    )(x, g, wg, ya, yb, yc, yd, wb, wo)


def _ffn_kernel(x_ref, g_ref, wgu_ref, wd_ref, fg_ref, o_ref, h_ref, act_ref, *, th, final_norm):
    x = x_ref[...]
    ms = jnp.mean(x * x, axis=-1, keepdims=True)
    h_ref[...] = (x * lax.rsqrt(ms + RMS_EPS) * g_ref[...]).astype(BF16)
    for c in range(FFN_HIDDEN // th):
        h = h_ref[...]
        gate = jnp.dot(h, wgu_ref[:, c * th:(c + 1) * th], preferred_element_type=F32)
        up = jnp.dot(h, wgu_ref[:, FFN_HIDDEN + c * th:FFN_HIDDEN + (c + 1) * th],
                     preferred_element_type=F32)
        act_ref[:, c * th:(c + 1) * th] = ((gate * _sigmoid(gate)) * up).astype(BF16)
    y = x_ref[...] + jnp.dot(act_ref[...], wd_ref[...], preferred_element_type=F32)
    if final_norm:
        ms = jnp.mean(y * y, axis=-1, keepdims=True)
        y = y * lax.rsqrt(ms + RMS_EPS) * fg_ref[...]
    o_ref[...] = y


def _ffn(x, g, wgu, wd, final_g, tm, th, final_norm):
    m = x.shape[0]
    resident = pl.Buffered(1)
    return pl.pallas_call(
        functools.partial(_ffn_kernel, th=th, final_norm=final_norm),
        grid=(m // tm,),
        in_specs=[
            pl.BlockSpec((tm, D_MODEL), lambda i: (i, 0)),
            pl.BlockSpec((1, D_MODEL), lambda i: (0, 0)),
            pl.BlockSpec((D_MODEL, 2 * FFN_HIDDEN), lambda i: (0, 0), pipeline_mode=resident),
            pl.BlockSpec((FFN_HIDDEN, D_MODEL), lambda i: (0, 0), pipeline_mode=resident),
            pl.BlockSpec((1, D_MODEL), lambda i: (0, 0)),
        ],
        out_specs=pl.BlockSpec((tm, D_MODEL), lambda i: (i, 0)),
        out_shape=jax.ShapeDtypeStruct((m, D_MODEL), F32),
        scratch_shapes=[pltpu.VMEM((tm, D_MODEL), BF16), pltpu.VMEM((tm, FFN_HIDDEN), BF16)],
        compiler_params=_cparams(("parallel",)),
        name="ffn",
    )(x, g, wgu, wd, final_g)


def _row_tile(m, want):
    while m % want:
        want //= 2
    return want


def _pad_lanes(v):
    return jnp.pad(v.astype(F32), (0, LANES - v.shape[0]))[None, :]


def kernel(x, norm1_g, w_in, conv_a_w, ssm_conv_w, ssm_conv_b, ssm_dt_bias, ssm_a_log, ssm_d,
           ssm_norm_g, w_branch, w_o, norm2_g, w_gate_up, w_down, final_g):
    bsz, seq, _ = x.shape
    assert seq % MOBA_BLOCK == 0 and seq % SSM_CHUNK == 0
    depth = w_in.shape[0]
    assert depth >= 1
    m = bsz * seq
    b_col = 3 * BRANCH_W
    z_col = b_col + 3 * BRANCH_W
    dt_col = z_col + BRANCH_W + SSM_CONV_DIM
    d_col = dt_col + N_HEADS
    g_col = d_col + 3 * BRANCH_W
    h = x.reshape(m, D_MODEL)
    tm = _row_tile(m, 512)
    for l in range(depth):
        w_main = jnp.concatenate([w_in[l, :, b_col:z_col], w_in[l, :, d_col:g_col],
                                  w_in[l, :, :b_col], w_in[l, :, z_col:dt_col]], axis=1).astype(BF16)
        w_gate = w_in[l, :, g_col:].astype(BF16)
        w_dt = jnp.pad(w_in[l, :, dt_col:d_col], ((0, 0), (0, LANES - N_HEADS))).astype(BF16)
        u, y_a, y_c = _inproj_mixers(h, norm1_g[l][None, :], w_main, w_dt, conv_a_w[l], ssm_conv_w[l],
                                     ssm_conv_b[l][None, :], _pad_lanes(ssm_dt_bias[l]),
                                     _pad_lanes(ssm_a_log[l]), jnp.repeat(ssm_d[l], HEAD_DIM)[None, :],
                                     ssm_norm_g[l][None, :], seq, tm, 256)
        y_b, y_d = _attention(u, bsz, seq)
        h = _merge(h, norm1_g[l][None, :], w_gate, y_a, y_b, y_c, y_d,
                   w_branch[l].astype(BF16), w_o[l].astype(BF16), tm)
        h = _ffn(h, norm2_g[l][None, :], w_gate_up[l].astype(BF16), w_down[l].astype(BF16),
                 final_g[None, :], tm, 256, final_norm=(l == depth - 1))
    return h.reshape(bsz, seq, D_MODEL)
```

```python
import functools

import jax
import jax.numpy as jnp
from jax import lax
from jax.experimental import pallas as pl
from jax.experimental.pallas import tpu as pltpu

F32 = jnp.float32
BF16 = jnp.bfloat16

D_MODEL = 1024
HEAD_DIM = 64
BRANCH_W = 256
N_BRANCH = 4
N_HEADS = 4
SC_K = 3
SSM_GROUPS = 2
SSM_STATE = 64
SSM_CONV_K = 4
SSM_CHUNK = 256
SSM_CONV_DIM = BRANCH_W + 2 * SSM_GROUPS * SSM_STATE
MOBA_BLOCK = 256
MOBA_TOPK = 3
FFN_HIDDEN = 2816
RMS_EPS = 1e-6

LANES = 128
HIST = 8
BF16_ROWS = 16
NEG = -1e30

N_ATT = 1536
COL_SB_Q, COL_SB_K, COL_SB_V = 0, 2, 4
COL_MO_Q, COL_MO_K, COL_MO_V = 6, 8, 10
N_LOC = 1536
LOC_Z = 3 * BRANCH_W
LOC_XBC = 4 * BRANCH_W
N_PACK = N_ATT + N_LOC

VMEM_LIMIT = 56 * 1024 * 1024


def _cparams(sem):
    return pltpu.CompilerParams(dimension_semantics=sem, vmem_limit_bytes=VMEM_LIMIT)


def _sigmoid(x):
    return 1.0 / (1.0 + jnp.exp(-x))


def _neg_abs(x):
    return pltpu.bitcast(pltpu.bitcast(x, jnp.uint32) | jnp.uint32(0x80000000), F32)


def _attn_kernel(qs_ref, ks_ref, vs_ref, qm_ref, km_ref, vm_ref, os_ref, om_ref,
                 s_vt, s_sp, s_d, s_w, s_r, s_acc,
                 m_vt, m_mean, m_sel, m_p, m_alpha, m_max, m_acc, *, nb, ke):
    t = MOBA_BLOCK
    qi = pl.program_id(1)
    n_tiles = qi + 1
    ones_row = (HEAD_DIM, 0)
    rows = lax.broadcasted_iota(jnp.int32, (LANES, t), 0)
    head0_rows = rows < HEAD_DIM
    heads = range(N_HEADS)

    def pair_lanes(h):
        return slice((h // 2) * LANES, (h // 2 + 1) * LANES)

    def own_rows(h, x, other):
        return jnp.where(head0_rows, x, other) if h % 2 == 0 else jnp.where(head0_rows, other, x)

    @pl.when(qi == 0)
    def _():
        def tr(c, carry):
            kv = pl.ds(pl.multiple_of(c * t, t), t)
            for h in heads:
                vt = vs_ref[kv, pair_lanes(h)].astype(F32).T
                s_vt[h, c] = own_rows(h, vt, 0.0).astype(BF16)
                vt = vm_ref[kv, pair_lanes(h)].astype(F32).T
                m_vt[h, c] = own_rows(h, vt, jnp.where(rows == ones_row[h % 2], 1.0, 0.0)).astype(BF16)
            m_mean[pl.ds(c, 1), :] = jnp.sum(km_ref[kv, :].astype(F32), axis=0, keepdims=True) * (1.0 / t)
            return carry

        m_mean[...] = jnp.zeros_like(m_mean)
        lax.fori_loop(0, nb, tr, 0)

    def split_heads(q_t):
        return tuple(own_rows(h, q_t[(h // 2) * LANES:(h // 2 + 1) * LANES, :], 0.0).astype(BF16) for h in heads)

    s_qt = split_heads((qs_ref[...].astype(F32) * (HEAD_DIM ** -0.5)).T)
    m_qt = split_heads(qm_ref[...].astype(F32).T)
    m_qst = tuple(x * (HEAD_DIM ** -0.5) for x in m_qt)

    mean = m_mean[...]
    mean_hi = mean.astype(BF16)
    mean_lo = (mean - mean_hi.astype(F32)).astype(BF16)
    blk = lax.broadcasted_iota(jnp.int32, (m_mean.shape[0], t), 0).astype(F32)
    qif = qi.astype(F32)
    for h in heads:
        gate = (jnp.dot(mean_hi[:, pair_lanes(h)], m_qt[h], preferred_element_type=F32)
                + jnp.dot(mean_lo[:, pair_lanes(h)], m_qt[h], preferred_element_type=F32))
        g = jnp.where(blk < qif, gate, -jnp.inf)
        sel = jnp.zeros_like(gate)
        for r in range(ke):
            mx = jnp.max(g, axis=0, keepdims=True)
            idx = jnp.min(jnp.where(g == mx, blk, 1e9), axis=0, keepdims=True)
            hit = blk == idx
            sel = jnp.where(jnp.logical_and(hit, qif > r), 1.0, sel)
            g = jnp.where(hit, -jnp.inf, g)
        m_sel[h] = sel

    row = lax.broadcasted_iota(jnp.int32, (t, t), 0)
    col = lax.broadcasted_iota(jnp.int32, (t, t), 1)
    later_t = jnp.where(col > row, 1.0, 0.0).astype(BF16)

    def tile_of(p):
        return jnp.maximum(qi - p, 0)

    def key_rows(p):
        return pl.ds(pl.multiple_of(tile_of(p) * t, t), t)

    def s_scores(p, par, masked):
        k = ks_ref[key_rows(p), :]
        for h in heads:
            z = jnp.dot(k[:, pair_lanes(h)], s_qt[h], preferred_element_type=F32)
            if masked:
                z = jnp.where(row < col, z, NEG)
            sp = jnp.maximum(z, 0.0) + jnp.log(1.0 + jnp.exp(_neg_abs(z)))
            s_sp[par, h] = sp.astype(BF16)
            s_d[par, h] = z - sp

    def s_weights(par):
        for h in heads:
            sp = s_sp[par, h]
            between = jnp.dot(later_t, sp, preferred_element_type=F32)
            r = s_r[h]
            s_w[par, h] = jnp.exp(s_d[par, h] - between - r).astype(BF16)
            s_r[h] = r + between[0:1, :] + sp[0:1, :].astype(F32)

    def s_value(p, par):
        tile = tile_of(p)
        for i in range(N_HEADS // 2):
            s_acc[i] += (jnp.dot(s_vt[2 * i, tile], s_w[par, 2 * i], preferred_element_type=F32)
                         + jnp.dot(s_vt[2 * i + 1, tile], s_w[par, 2 * i + 1], preferred_element_type=F32))

    def m_scores(p, par, own):
        k = km_ref[key_rows(p), :]
        for h in heads:
            s = jnp.dot(k[:, pair_lanes(h)], m_qst[h], preferred_element_type=F32)
            if own:
                s = jnp.where(row <= col, s, -jnp.inf)
                m_new = jnp.max(s, axis=0, keepdims=True)
                shift = m_new
                m_alpha[par, h] = jnp.zeros((1, t), F32)
            else:
                chosen = m_sel[h, pl.ds(tile_of(p), 1), :] > 0.5
                m_old = m_max[h]
                m_tile = jnp.max(s, axis=0, keepdims=True)
                m_new = jnp.where(chosen, jnp.maximum(m_old, m_tile), m_old)
                shift = jnp.where(chosen, m_new, jnp.inf)
                m_alpha[par, h] = jnp.exp(m_old - m_new)
            m_p[par, h] = jnp.exp(s - shift).astype(BF16)
            m_max[h] = m_new

    def m_value(p, par):
        tile = tile_of(p)
        for h in heads:
            m_acc[h] = (m_acc[h] * m_alpha[par, h]
                        + jnp.dot(m_vt[h, tile], m_p[par, h], preferred_element_type=F32))

    def step(s, par):
        s_scores(s, par, False)
        m_scores(s - 1, 1 - par, False)
        s_weights(1 - par)
        s_value(s - 2, par)
        m_value(s - 2, par)

    s_acc[...] = jnp.zeros_like(s_acc)
    s_r[...] = jnp.zeros_like(s_r)
    m_acc[...] = jnp.zeros_like(m_acc)
    s_scores(0, 0, True)
    s_weights(0)
    m_scores(0, 0, True)
    s_scores(1, 1, False)

    n_full = jnp.maximum(n_tiles - 2, 0)

    def body(i, carry):
        s = 2 + 2 * i
        step(s, 0)
        step(s + 1, 1)
        return carry

    lax.fori_loop(0, n_full // 2, body, 0)

    @pl.when(n_full % 2 == 1)
    def _():
        step(n_tiles - 1, 0)

    def drain(par_t, first):
        if first <= 0:
            s_value(n_tiles - 2, par_t)
            m_value(n_tiles - 2, par_t)
            s_weights(1 - par_t)
            m_scores(n_tiles - 1, 1 - par_t, False)
        s_value(n_tiles - 1, 1 - par_t)
        m_value(n_tiles - 1, 1 - par_t)

    for par_t in range(2):
        @pl.when(jnp.logical_and(n_tiles >= 2, n_tiles % 2 == par_t))
        def _():
            drain(par_t, 0)

    @pl.when(n_tiles == 1)
    def _():
        drain(1, 1)

    for i in range(N_HEADS // 2):
        sl = slice(i * LANES, (i + 1) * LANES)
        os_ref[:, sl] = s_acc[i].T.astype(os_ref.dtype)
        a0 = m_acc[2 * i]
        a1 = m_acc[2 * i + 1]
        out_t = jnp.where(head0_rows, a0 / a0[ones_row[0]:ones_row[0] + 1, :],
                          a1 / a1[ones_row[1]:ones_row[1] + 1, :])
        om_ref[:, sl] = out_t.T.astype(om_ref.dtype)


def _attention(u, bsz, seq):
    t = MOBA_BLOCK
    w = BRANCH_W
    nb = seq // t
    nb_rows = -(-nb // BF16_ROWS) * BF16_ROWS
    ke = max(1, min(MOBA_TOPK, nb - 1))
    tile_f32 = pltpu.VMEM((2, N_HEADS, t, t), F32)
    tile_bf16 = pltpu.VMEM((2, N_HEADS, t, t), BF16)
    vt_tiles = pltpu.VMEM((N_HEADS, nb, LANES, t), BF16)
    resident = pl.Buffered(1)

    def q_spec(col):
        return pl.BlockSpec((t, w), lambda b, qi: (b * nb + qi, col * LANES // w))

    def kv_spec(col):
        return pl.BlockSpec((seq, w), lambda b, qi: (b, col * LANES // w), pipeline_mode=resident)

    out_spec = pl.BlockSpec((t, w), lambda b, qi: (b * nb + qi, 0))
    out_shape = jax.ShapeDtypeStruct((bsz * seq, w), BF16)
    return pl.pallas_call(
        functools.partial(_attn_kernel, nb=nb, ke=ke),
        grid=(bsz, nb),
        in_specs=[q_spec(COL_SB_Q), kv_spec(COL_SB_K), kv_spec(COL_SB_V),
                  q_spec(COL_MO_Q), kv_spec(COL_MO_K), kv_spec(COL_MO_V)],
        out_specs=[out_spec, out_spec],
        out_shape=[out_shape, out_shape],
        scratch_shapes=[
            vt_tiles,
            tile_bf16,
            tile_f32,
            tile_bf16,
            pltpu.VMEM((N_HEADS, 1, t), F32),
            pltpu.VMEM((N_HEADS // 2, LANES, t), F32),
            vt_tiles,
            pltpu.VMEM((nb_rows, w), F32),
            pltpu.VMEM((N_HEADS, nb_rows, t), F32),
            tile_bf16,
            pltpu.VMEM((2, N_HEADS, 1, t), F32),
            pltpu.VMEM((N_HEADS, 1, t), F32),
            pltpu.VMEM((N_HEADS, LANES, t), F32),
        ],
        compiler_params=_cparams(("parallel", "arbitrary")),
        name="attention",
    )(u, u, u, u, u, u)


def _per_head(v, lanes_per_head, width):
    head = lax.broadcasted_iota(jnp.int32, (1, width), 1) // lanes_per_head
    out = jnp.zeros((v.shape[0], width), F32)
    for h in range(N_HEADS):
        out = jnp.where(head == h, v[:, h:h + 1], out)
    return out


def _split3(a):
    hi = a.astype(BF16)
    r1 = a - hi.astype(F32)
    mid = r1.astype(BF16)
    lo = (r1 - mid.astype(F32)).astype(BF16)
    return hi, mid, lo


def _mixer_chunk(loc_ref, dt_ref, rows, first, cwa_ref, cwc_ref, cbias_ref, dtb_ref, alog_ref,
                 dskip_ref, ng_ref, ya_ref, yc_ref, bufa, bufc, hst):
    t = SSM_CHUNK
    w = BRANCH_W

    ua = loc_ref[rows, 0:3 * w].astype(F32)
    bufa[0:HIST, :] = jnp.where(first, 0.0, bufa[0:HIST, :])
    bufa[HIST:, :] = ua[:, 2 * w:] * ua[:, :w]
    conv = cwa_ref[0:1, :] * bufa[pl.ds(HIST - SC_K + 1, t), :]
    for kk in range(1, SC_K):
        conv = conv + cwa_ref[kk:kk + 1, :] * bufa[pl.ds(HIST - SC_K + 1 + kk, t), :]
    ya_ref[rows, :] = (ua[:, w:2 * w] * conv).astype(ya_ref.dtype)
    bufa[0:HIST, :] = bufa[t:t + HIST, :]
    yield

    bufc[0:HIST, :] = jnp.where(first, 0.0, bufc[0:HIST, :])
    bufc[HIST:, :] = loc_ref[rows, LOC_XBC:LOC_XBC + SSM_CONV_DIM].astype(F32)
    xc = cbias_ref[...] + cwc_ref[0:1, :] * bufc[pl.ds(HIST - SSM_CONV_K + 1, t), :]
    for kk in range(1, SSM_CONV_K):
        xc = xc + cwc_ref[kk:kk + 1, :] * bufc[pl.ds(HIST - SSM_CONV_K + 1 + kk, t), :]
    bufc[0:HIST, :] = bufc[t:t + HIST, :]
    xc = xc * _sigmoid(xc)
    xs = xc[:, :w]
    b_in = xc[:, w:w + LANES]
    c_in = xc[:, w + LANES:]
    yield

    dtp = dt_ref[rows, :] + dtb_ref[...]
    dt = jnp.maximum(dtp, 0.0) + jnp.log(1.0 + jnp.exp(-jnp.abs(dtp)))
    a = dt * (-jnp.exp(alog_ref[...]))
    row = lax.broadcasted_iota(jnp.int32, (t, t), 0)
    col = lax.broadcasted_iota(jnp.int32, (t, t), 1)
    causal = row >= col
    tri = jnp.where(causal, 1.0, 0.0).astype(BF16)
    a_hi, a_mid, a_lo = _split3(a)
    acs = (jnp.dot(tri, a_hi, preferred_element_type=F32)
           + jnp.dot(tri, a_mid, preferred_element_type=F32)
           + jnp.dot(tri, a_lo, preferred_element_type=F32))
    acs_t = acs.T
    acs_x = _per_head(acs, HEAD_DIM, w)
    last_x = acs_x[t - 1:t, :]
    x_dt = xs * _per_head(dt, HEAD_DIM, w)
    to_end_x = jnp.exp(last_x - acs_x)
    from_start_x = jnp.exp(acs_x)
    chunk_decay_x = jnp.exp(last_x)
    yield

    lane = lax.broadcasted_iota(jnp.int32, (1, LANES), 1)
    low = lane < HEAD_DIM
    c_bf = c_in.astype(BF16)
    nt = (((1,), (1,)), ((), ()))
    for g in range(SSM_GROUPS):
        gmask = low if g == 0 else jnp.logical_not(low)
        sl = slice(g * LANES, (g + 1) * LANES)
        b_g = jnp.where(gmask, b_in, 0.0)
        cb = lax.dot_general(c_bf, b_g.astype(BF16), nt, preferred_element_type=F32)
        xg = x_dt[:, sl]
        y = jnp.zeros((t, LANES), F32)
        for e in range(2):
            h = 2 * g + e
            seg = acs[:, h:h + 1] - acs_t[h:h + 1, :]
            decay = jnp.exp(jnp.where(causal, seg, -jnp.inf))
            emask = low if e == 0 else jnp.logical_not(low)
            xe = jnp.where(emask, xg, 0.0).astype(BF16)
            y = y + jnp.dot((cb * decay).astype(BF16), xe, preferred_element_type=F32)
        yield
        h_enter = jnp.where(first, 0.0, hst[g])
        y = y + jnp.dot(c_bf, h_enter.astype(BF16), preferred_element_type=F32) * from_start_x[:, sl]
        state = jnp.dot(b_g.T.astype(BF16), (xg * to_end_x[:, sl]).astype(BF16),
                        preferred_element_type=F32)
        hst[g] = h_enter * chunk_decay_x[:, sl] + state

        y = y + xs[:, sl] * dskip_ref[:, sl]
        zg = loc_ref[rows, LOC_Z + g * LANES:LOC_Z + (g + 1) * LANES].astype(F32)
        gated = y * (zg * _sigmoid(zg))
        ms = jnp.mean(gated * gated, axis=-1, keepdims=True)
        yc_ref[rows, sl] = (gated * lax.rsqrt(ms + RMS_EPS) * ng_ref[:, sl]).astype(yc_ref.dtype)
        yield


def _inproj_mixers_kernel(x_ref, g_ref, w_ref, wdt_ref, cwa_ref, cwc_ref, cbias_ref, dtb_ref, alog_ref,
                          dskip_ref, ng_ref, u_ref, ya_ref, yc_ref,
                          h_ref, loc_ref, dt_ref, loc_new, dt_new, bufa, bufc, hst, *, tn, chunks_per_seq):
    i = pl.program_id(0)
    tm = x_ref.shape[0]
    chunks_per_tile = tm // SSM_CHUNK

    @pl.when(i == 0)
    def _():
        loc_ref[...] = jnp.zeros_like(loc_ref)
        dt_ref[...] = jnp.zeros_like(dt_ref)
        bufa[...] = jnp.zeros_like(bufa)
        bufc[...] = jnp.zeros_like(bufc)
        hst[...] = jnp.zeros_like(hst)

    def mixer_phases():
        for c in range(chunks_per_tile):
            chunk = (i - 1) * chunks_per_tile + c
            first = lax.rem(chunk + chunks_per_seq, chunks_per_seq) == 0
            yield from _mixer_chunk(loc_ref, dt_ref, slice(c * SSM_CHUNK, (c + 1) * SSM_CHUNK), first,
                                    cwa_ref, cwc_ref, cbias_ref, dtb_ref, alog_ref, dskip_ref, ng_ref,
                                    ya_ref, yc_ref, bufa, bufc, hst)

    mixers = mixer_phases()
    x = x_ref[...]
    ms = jnp.mean(x * x, axis=-1, keepdims=True)
    h_ref[...] = (x * lax.rsqrt(ms + RMS_EPS) * g_ref[...]).astype(BF16)
    dt_new[...] = jnp.dot(h_ref[...], wdt_ref[...], preferred_element_type=F32)
    for c in range(N_PACK // tn):
        next(mixers, None)
        sl = slice(c * tn, (c + 1) * tn)
        acc = jnp.dot(h_ref[...], w_ref[:, sl], preferred_element_type=F32).astype(BF16)
        if c * tn < N_ATT:
            u_ref[:, sl] = acc
        else:
            loc_new[:, c * tn - N_ATT:(c + 1) * tn - N_ATT] = acc
    for _ in mixers:
        pass

    loc_ref[...] = loc_new[...]
    dt_ref[...] = dt_new[...]


def _inproj_mixers(x, g, w, wdt, cwa, cwc, cbias, dtb, alog, dskip, ng, seq, tm, tn):
    m = x.shape[0]
    n = m // tm
    assert tm % SSM_CHUNK == 0 and seq % tm == 0 and N_ATT % tn == 0 and N_LOC % tn == 0
    wbr = BRANCH_W
    resident = pl.Buffered(1)
    small = lambda shape: pl.BlockSpec(shape, lambda i: (0, 0))
    this_tile = lambda i: (jnp.minimum(i, n - 1), 0)
    prev_tile = lambda i: (jnp.maximum(i - 1, 0), 0)
    return pl.pallas_call(
        functools.partial(_inproj_mixers_kernel, tn=tn, chunks_per_seq=seq // SSM_CHUNK),
        grid=(n + 1,),
        in_specs=[
            pl.BlockSpec((tm, D_MODEL), this_tile),
            small((1, D_MODEL)),
            pl.BlockSpec((D_MODEL, N_PACK), lambda i: (0, 0), pipeline_mode=resident),
            small((D_MODEL, LANES)),
            small((SC_K, wbr)), small((SSM_CONV_K, SSM_CONV_DIM)), small((1, SSM_CONV_DIM)),
            small((1, LANES)), small((1, LANES)), small((1, wbr)), small((1, wbr)),
        ],
        out_specs=[
            pl.BlockSpec((tm, N_ATT), this_tile),
            pl.BlockSpec((tm, wbr), prev_tile),
            pl.BlockSpec((tm, wbr), prev_tile),
        ],
        out_shape=[jax.ShapeDtypeStruct((m, N_ATT), BF16),
                   jax.ShapeDtypeStruct((m, wbr), BF16),
                   jax.ShapeDtypeStruct((m, wbr), BF16)],
        scratch_shapes=[
            pltpu.VMEM((tm, D_MODEL), BF16),
            pltpu.VMEM((tm, N_LOC), BF16),
            pltpu.VMEM((tm, LANES), F32),
            pltpu.VMEM((tm, N_LOC), BF16),
            pltpu.VMEM((tm, LANES), F32),
            pltpu.VMEM((SSM_CHUNK + HIST, wbr), F32),
            pltpu.VMEM((SSM_CHUNK + HIST, SSM_CONV_DIM), F32),
            pltpu.VMEM((SSM_GROUPS, LANES, LANES), F32),
        ],
        compiler_params=_cparams(("arbitrary",)),
        name="inproj_mixers",
    )(x, g, w, wdt, cwa, cwc, cbias, dtb, alog, dskip, ng)


def _merge_kernel(x_ref, g_ref, wg_ref, ya_ref, yb_ref, yc_ref, yd_ref, wb_ref, wo_ref, o_ref, h_ref):
    x = x_ref[...]
    ms = jnp.mean(x * x, axis=-1, keepdims=True)
    h_ref[...] = (x * lax.rsqrt(ms + RMS_EPS) * g_ref[...]).astype(BF16)
    merged = None
    for i, y_ref in enumerate((ya_ref, yb_ref, yc_ref, yd_ref)):
        pre = jnp.dot(h_ref[...], wg_ref[:, i * D_MODEL:(i + 1) * D_MODEL], preferred_element_type=F32)
        term = (1.0 + jnp.tanh(0.5 * pre)) * jnp.dot(y_ref[...], wb_ref[i], preferred_element_type=F32)
        merged = term if merged is None else merged + term
    merged = (0.5 * merged).astype(BF16)
    o_ref[...] = x_ref[...] + jnp.dot(merged, wo_ref[...], preferred_element_type=F32)


def _merge(x, g, wg, ya, yb, yc, yd, wb, wo, tm):
    m = x.shape[0]
    ybr = pl.BlockSpec((tm, BRANCH_W), lambda i: (i, 0))
    resident = pl.Buffered(1)
    return pl.pallas_call(
        _merge_kernel,
        grid=(m // tm,),
        in_specs=[
            pl.BlockSpec((tm, D_MODEL), lambda i: (i, 0)),
            pl.BlockSpec((1, D_MODEL), lambda i: (0, 0)),
            pl.BlockSpec((D_MODEL, N_BRANCH * D_MODEL), lambda i: (0, 0), pipeline_mode=resident),
            ybr, ybr, ybr, ybr,
            pl.BlockSpec((N_BRANCH, BRANCH_W, D_MODEL), lambda i: (0, 0, 0), pipeline_mode=resident),
            pl.BlockSpec((D_MODEL, D_MODEL), lambda i: (0, 0), pipeline_mode=resident),
        ],
        out_specs=pl.BlockSpec((tm, D_MODEL), lambda i: (i, 0)),
        out_shape=jax.ShapeDtypeStruct((m, D_MODEL), F32),
        scratch_shapes=[pltpu.VMEM((tm, D_MODEL), BF16)],
        compiler_params=_cparams(("parallel",)),
        name="merge",
    )(x, g, wg, ya, yb, yc, yd, wb, wo)


def _ffn_kernel(x_ref, g_ref, wgu_ref, wd_ref, fg_ref, o_ref, h_ref, act_ref, *, th, final_norm):
    x = x_ref[...]
    ms = jnp.mean(x * x, axis=-1, keepdims=True)
    h_ref[...] = (x * lax.rsqrt(ms + RMS_EPS) * g_ref[...]).astype(BF16)
    for c in range(FFN_HIDDEN // th):
        h = h_ref[...]
        gate = jnp.dot(h, wgu_ref[:, c * th:(c + 1) * th], preferred_element_type=F32)
        up = jnp.dot(h, wgu_ref[:, FFN_HIDDEN + c * th:FFN_HIDDEN + (c + 1) * th],
                     preferred_element_type=F32)
        act_ref[:, c * th:(c + 1) * th] = ((gate * _sigmoid(gate)) * up).astype(BF16)
    y = x_ref[...] + jnp.dot(act_ref[...], wd_ref[...], preferred_element_type=F32)
    if final_norm:
        ms = jnp.mean(y * y, axis=-1, keepdims=True)
        y = y * lax.rsqrt(ms + RMS_EPS) * fg_ref[...]
    o_ref[...] = y


def _ffn(x, g, wgu, wd, final_g, tm, th, final_norm):
    m = x.shape[0]
    resident = pl.Buffered(1)
    return pl.pallas_call(
        functools.partial(_ffn_kernel, th=th, final_norm=final_norm),
        grid=(m // tm,),
        in_specs=[
            pl.BlockSpec((tm, D_MODEL), lambda i: (i, 0)),
            pl.BlockSpec((1, D_MODEL), lambda i: (0, 0)),
            pl.BlockSpec((D_MODEL, 2 * FFN_HIDDEN), lambda i: (0, 0), pipeline_mode=resident),
            pl.BlockSpec((FFN_HIDDEN, D_MODEL), lambda i: (0, 0), pipeline_mode=resident),
            pl.BlockSpec((1, D_MODEL), lambda i: (0, 0)),
        ],
        out_specs=pl.BlockSpec((tm, D_MODEL), lambda i: (i, 0)),
        out_shape=jax.ShapeDtypeStruct((m, D_MODEL), F32),
        scratch_shapes=[pltpu.VMEM((tm, D_MODEL), BF16), pltpu.VMEM((tm, FFN_HIDDEN), BF16)],
        compiler_params=_cparams(("parallel",)),
        name="ffn",
    )(x, g, wgu, wd, final_g)


def _row_tile(m, want):
    while m % want:
        want //= 2
    return want


def _pad_lanes(v):
    return jnp.pad(v.astype(F32), (0, LANES - v.shape[0]))[None, :]


def kernel(x, norm1_g, w_in, conv_a_w, ssm_conv_w, ssm_conv_b, ssm_dt_bias, ssm_a_log, ssm_d,
           ssm_norm_g, w_branch, w_o, norm2_g, w_gate_up, w_down, final_g):
    bsz, seq, _ = x.shape
    assert seq % MOBA_BLOCK == 0 and seq % SSM_CHUNK == 0
    depth = w_in.shape[0]
    assert depth >= 1
    m = bsz * seq
    b_col = 3 * BRANCH_W
    z_col = b_col + 3 * BRANCH_W
    dt_col = z_col + BRANCH_W + SSM_CONV_DIM
    d_col = dt_col + N_HEADS
    g_col = d_col + 3 * BRANCH_W
    h = x.reshape(m, D_MODEL)
    tm = _row_tile(m, 512)
    for l in range(depth):
        w_main = jnp.concatenate([w_in[l, :, b_col:z_col], w_in[l, :, d_col:g_col],
                                  w_in[l, :, :b_col], w_in[l, :, z_col:dt_col]], axis=1).astype(BF16)
        w_gate = w_in[l, :, g_col:].astype(BF16)
        w_dt = jnp.pad(w_in[l, :, dt_col:d_col], ((0, 0), (0, LANES - N_HEADS))).astype(BF16)
        u, y_a, y_c = _inproj_mixers(h, norm1_g[l][None, :], w_main, w_dt, conv_a_w[l], ssm_conv_w[l],
                                     ssm_conv_b[l][None, :], _pad_lanes(ssm_dt_bias[l]),
                                     _pad_lanes(ssm_a_log[l]), jnp.repeat(ssm_d[l], HEAD_DIM)[None, :],
                                     ssm_norm_g[l][None, :], seq, tm, 256)
        y_b, y_d = _attention(u, bsz, seq)
        h = _merge(h, norm1_g[l][None, :], w_gate, y_a, y_b, y_c, y_d,
                   w_branch[l].astype(BF16), w_o[l].astype(BF16), tm)
        h = _ffn(h, norm2_g[l][None, :], w_gate_up[l].astype(BF16), w_down[l].astype(BF16),
                 final_g[None, :], tm, 256, final_norm=(l == depth - 1))
    return h.reshape(bsz, seq, D_MODEL)
```

```python
import functools

import jax
import jax.numpy as jnp
from jax import lax
from jax.experimental import pallas as pl
from jax.experimental.pallas import tpu as pltpu

F32 = jnp.float32
BF16 = jnp.bfloat16

D_MODEL = 1024
HEAD_DIM = 64
BRANCH_W = 256
N_BRANCH = 4
N_HEADS = 4
SC_K = 3
SSM_GROUPS = 2
SSM_STATE = 64
SSM_CONV_K = 4
SSM_CHUNK = 256
SSM_CONV_DIM = BRANCH_W + 2 * SSM_GROUPS * SSM_STATE
MOBA_BLOCK = 256
MOBA_TOPK = 3
FFN_HIDDEN = 2816
RMS_EPS = 1e-6

LANES = 128
HIST = 8
BF16_ROWS = 16
NEG = -1e30

N_ATT = 1536
COL_SB_Q, COL_SB_K, COL_SB_V = 0, 2, 4
COL_MO_Q, COL_MO_K, COL_MO_V = 6, 8, 10
N_LOC = 1536
LOC_Z = 3 * BRANCH_W
LOC_XBC = 4 * BRANCH_W
N_PACK = N_ATT + N_LOC

VMEM_LIMIT = 56 * 1024 * 1024


def _cparams(sem):
    return pltpu.CompilerParams(dimension_semantics=sem, vmem_limit_bytes=VMEM_LIMIT)


def _sigmoid(x):
    return 1.0 / (1.0 + jnp.exp(-x))


def _neg_abs(x):
    return pltpu.bitcast(pltpu.bitcast(x, jnp.uint32) | jnp.uint32(0x80000000), F32)


def _attn_kernel(qs_ref, ks_ref, vs_ref, qm_ref, km_ref, vm_ref, os_ref, om_ref,
                 s_vt, s_sp, s_d, s_w, s_r, s_acc,
                 m_vt, m_mean, m_sel, m_p, m_alpha, m_max, m_acc, *, ke):
    t = MOBA_BLOCK
    qi = pl.program_id(1)
    n_tiles = qi + 1
    ones_row = (HEAD_DIM, 0)
    rows = lax.broadcasted_iota(jnp.int32, (LANES, t), 0)
    head0_rows = rows < HEAD_DIM
    heads = range(N_HEADS)

    def pair_lanes(h):
        return slice((h // 2) * LANES, (h // 2 + 1) * LANES)

    def own_rows(h, x, other):
        return jnp.where(head0_rows, x, other) if h % 2 == 0 else jnp.where(head0_rows, other, x)

    @pl.when(qi == 0)
    def _():
        m_mean[...] = jnp.zeros_like(m_mean)

    own_keys = pl.ds(pl.multiple_of(qi * t, t), t)
    for h in heads:
        vt = vs_ref[own_keys, pair_lanes(h)].astype(F32).T
        s_vt[h, qi] = own_rows(h, vt, 0.0).astype(BF16)
        vt = vm_ref[own_keys, pair_lanes(h)].astype(F32).T
        m_vt[h, qi] = own_rows(h, vt, jnp.where(rows == ones_row[h % 2], 1.0, 0.0)).astype(BF16)
    own_mean = jnp.sum(km_ref[own_keys, :].astype(F32), axis=0, keepdims=True) * (1.0 / t)

    def split_heads(q_t):
        return tuple(own_rows(h, q_t[(h // 2) * LANES:(h // 2 + 1) * LANES, :], 0.0).astype(BF16) for h in heads)

    s_qt = split_heads((qs_ref[...].astype(F32) * (HEAD_DIM ** -0.5)).T)
    m_qt = split_heads(qm_ref[...].astype(F32).T)
    m_qst = tuple(x * (HEAD_DIM ** -0.5) for x in m_qt)

    mean = m_mean[...]
    mean_hi = mean.astype(BF16)
    mean_lo = (mean - mean_hi.astype(F32)).astype(BF16)
    blk = lax.broadcasted_iota(jnp.int32, (m_mean.shape[0], t), 0).astype(F32)
    qif = qi.astype(F32)
    for h in heads:
        gate = (jnp.dot(mean_hi[:, pair_lanes(h)], m_qt[h], preferred_element_type=F32)
                + jnp.dot(mean_lo[:, pair_lanes(h)], m_qt[h], preferred_element_type=F32))
        g = jnp.where(blk < qif, gate, -jnp.inf)
        sel = jnp.zeros_like(gate)
        for r in range(ke):
            mx = jnp.max(g, axis=0, keepdims=True)
            idx = jnp.min(jnp.where(g == mx, blk, 1e9), axis=0, keepdims=True)
            hit = blk == idx
            sel = jnp.where(jnp.logical_and(hit, qif > r), 1.0, sel)
            g = jnp.where(hit, -jnp.inf, g)
        m_sel[h] = sel
    m_mean[pl.ds(qi, 1), :] = own_mean

    row = lax.broadcasted_iota(jnp.int32, (t, t), 0)
    col = lax.broadcasted_iota(jnp.int32, (t, t), 1)
    later_t = jnp.where(col > row, 1.0, 0.0).astype(BF16)

    def tile_of(p):
        return jnp.maximum(qi - p, 0)

    def key_rows(p):
        return pl.ds(pl.multiple_of(tile_of(p) * t, t), t)

    def s_scores(p, par, masked):
        k = ks_ref[key_rows(p), :]
        for h in heads:
            z = jnp.dot(k[:, pair_lanes(h)], s_qt[h], preferred_element_type=F32)
            if masked:
                z = jnp.where(row < col, z, NEG)
            sp = jnp.maximum(z, 0.0) + jnp.log(1.0 + jnp.exp(_neg_abs(z)))
            s_sp[par, h] = sp.astype(BF16)
            s_d[par, h] = z - sp

    def s_weights(par):
        for h in heads:
            sp = s_sp[par, h]
            between = jnp.dot(later_t, sp, preferred_element_type=F32)
            r = s_r[h]
            s_w[par, h] = jnp.exp(s_d[par, h] - between - r).astype(BF16)
            s_r[h] = r + between[0:1, :] + sp[0:1, :].astype(F32)

    def s_value(p, par):
        tile = tile_of(p)
        for i in range(N_HEADS // 2):
            s_acc[i] += (jnp.dot(s_vt[2 * i, tile], s_w[par, 2 * i], preferred_element_type=F32)
                         + jnp.dot(s_vt[2 * i + 1, tile], s_w[par, 2 * i + 1], preferred_element_type=F32))

    def m_scores(p, par, own):
        k = km_ref[key_rows(p), :]
        for h in heads:
            s = jnp.dot(k[:, pair_lanes(h)], m_qst[h], preferred_element_type=F32)
            if own:
                s = jnp.where(row <= col, s, -jnp.inf)
                m_new = jnp.max(s, axis=0, keepdims=True)
                shift = m_new
                m_alpha[par, h] = jnp.zeros((1, t), F32)
            else:
                chosen = m_sel[h, pl.ds(tile_of(p), 1), :] > 0.5
                m_old = m_max[h]
                m_tile = jnp.max(s, axis=0, keepdims=True)
                m_new = jnp.where(chosen, jnp.maximum(m_old, m_tile), m_old)
                shift = jnp.where(chosen, m_new, jnp.inf)
                m_alpha[par, h] = jnp.exp(m_old - m_new)
            m_p[par, h] = jnp.exp(s - shift).astype(BF16)
            m_max[h] = m_new

    def m_value(p, par):
        tile = tile_of(p)
        for h in heads:
            m_acc[h] = (m_acc[h] * m_alpha[par, h]
                        + jnp.dot(m_vt[h, tile], m_p[par, h], preferred_element_type=F32))

    def step(s, par):
        s_scores(s, par, False)
        m_scores(s - 1, 1 - par, False)
        s_weights(1 - par)
        s_value(s - 2, par)
        m_value(s - 2, par)

    s_acc[...] = jnp.zeros_like(s_acc)
    s_r[...] = jnp.zeros_like(s_r)
    m_acc[...] = jnp.zeros_like(m_acc)
    s_scores(0, 0, True)
    s_weights(0)
    m_scores(0, 0, True)
    s_scores(1, 1, False)

    n_full = jnp.maximum(n_tiles - 2, 0)

    def body(i, carry):
        s = 2 + 2 * i
        step(s, 0)
        step(s + 1, 1)
        return carry

    lax.fori_loop(0, n_full // 2, body, 0)

    @pl.when(n_full % 2 == 1)
    def _():
        step(n_tiles - 1, 0)

    def drain(par_t, first):
        if first <= 0:
            s_value(n_tiles - 2, par_t)
            m_value(n_tiles - 2, par_t)
            s_weights(1 - par_t)
            m_scores(n_tiles - 1, 1 - par_t, False)
        s_value(n_tiles - 1, 1 - par_t)
        m_value(n_tiles - 1, 1 - par_t)

    for par_t in range(2):
        @pl.when(jnp.logical_and(n_tiles >= 2, n_tiles % 2 == par_t))
        def _():
            drain(par_t, 0)

    @pl.when(n_tiles == 1)
    def _():
        drain(1, 1)

    for i in range(N_HEADS // 2):
        sl = slice(i * LANES, (i + 1) * LANES)
        os_ref[:, sl] = s_acc[i].T.astype(os_ref.dtype)
        a0 = m_acc[2 * i]
        a1 = m_acc[2 * i + 1]
        out_t = jnp.where(head0_rows, a0 / a0[ones_row[0]:ones_row[0] + 1, :],
                          a1 / a1[ones_row[1]:ones_row[1] + 1, :])
        om_ref[:, sl] = out_t.T.astype(om_ref.dtype)


def _attention(u, bsz, seq):
    t = MOBA_BLOCK
    w = BRANCH_W
    nb = seq // t
    nb_rows = -(-nb // BF16_ROWS) * BF16_ROWS
    ke = max(1, min(MOBA_TOPK, nb - 1))
    tile_f32 = pltpu.VMEM((2, N_HEADS, t, t), F32)
    tile_bf16 = pltpu.VMEM((2, N_HEADS, t, t), BF16)
    vt_tiles = pltpu.VMEM((N_HEADS, nb, LANES, t), BF16)
    resident = pl.Buffered(1)

    def q_spec(col):
        return pl.BlockSpec((t, w), lambda b, qi: (b * nb + qi, col * LANES // w))

    def kv_spec(col):
        return pl.BlockSpec((seq, w), lambda b, qi: (b, col * LANES // w), pipeline_mode=resident)

    out_spec = pl.BlockSpec((t, w), lambda b, qi: (b * nb + qi, 0))
    out_shape = jax.ShapeDtypeStruct((bsz * seq, w), BF16)
    return pl.pallas_call(
        functools.partial(_attn_kernel, ke=ke),
        grid=(bsz, nb),
        in_specs=[q_spec(COL_SB_Q), kv_spec(COL_SB_K), kv_spec(COL_SB_V),
                  q_spec(COL_MO_Q), kv_spec(COL_MO_K), kv_spec(COL_MO_V)],
        out_specs=[out_spec, out_spec],
        out_shape=[out_shape, out_shape],
        scratch_shapes=[
            vt_tiles,
            tile_bf16,
            tile_f32,
            tile_bf16,
            pltpu.VMEM((N_HEADS, 1, t), F32),
            pltpu.VMEM((N_HEADS // 2, LANES, t), F32),
            vt_tiles,
            pltpu.VMEM((nb_rows, w), F32),
            pltpu.VMEM((N_HEADS, nb_rows, t), F32),
            tile_bf16,
            pltpu.VMEM((2, N_HEADS, 1, t), F32),
            pltpu.VMEM((N_HEADS, 1, t), F32),
            pltpu.VMEM((N_HEADS, LANES, t), F32),
        ],
        compiler_params=_cparams(("parallel", "arbitrary")),
        name="attention",
    )(u, u, u, u, u, u)


def _per_head(v, lanes_per_head, width):
    head = lax.broadcasted_iota(jnp.int32, (1, width), 1) // lanes_per_head
    out = jnp.zeros((v.shape[0], width), F32)
    for h in range(N_HEADS):
        out = jnp.where(head == h, v[:, h:h + 1], out)
    return out


def _split3(a):
    hi = a.astype(BF16)
    r1 = a - hi.astype(F32)
    mid = r1.astype(BF16)
    lo = (r1 - mid.astype(F32)).astype(BF16)
    return hi, mid, lo


def _mixer_chunk(loc_ref, dt_ref, rows, first, cwa_ref, cwc_ref, cbias_ref, dtb_ref, alog_ref,
                 dskip_ref, ng_ref, ya_ref, yc_ref, bufa, bufc, hst):
    t = SSM_CHUNK
    w = BRANCH_W

    ua = loc_ref[rows, 0:3 * w].astype(F32)
    bufa[0:HIST, :] = jnp.where(first, 0.0, bufa[0:HIST, :])
    bufa[HIST:, :] = ua[:, 2 * w:] * ua[:, :w]
    conv = cwa_ref[0:1, :] * bufa[pl.ds(HIST - SC_K + 1, t), :]
    for kk in range(1, SC_K):
        conv = conv + cwa_ref[kk:kk + 1, :] * bufa[pl.ds(HIST - SC_K + 1 + kk, t), :]
    ya_ref[rows, :] = (ua[:, w:2 * w] * conv).astype(ya_ref.dtype)
    bufa[0:HIST, :] = bufa[t:t + HIST, :]
    yield

    bufc[0:HIST, :] = jnp.where(first, 0.0, bufc[0:HIST, :])
    bufc[HIST:, :] = loc_ref[rows, LOC_XBC:LOC_XBC + SSM_CONV_DIM].astype(F32)
    xc = cbias_ref[...] + cwc_ref[0:1, :] * bufc[pl.ds(HIST - SSM_CONV_K + 1, t), :]
    for kk in range(1, SSM_CONV_K):
        xc = xc + cwc_ref[kk:kk + 1, :] * bufc[pl.ds(HIST - SSM_CONV_K + 1 + kk, t), :]
    bufc[0:HIST, :] = bufc[t:t + HIST, :]
    xc = xc * _sigmoid(xc)
    xs = xc[:, :w]
    b_in = xc[:, w:w + LANES]
    c_in = xc[:, w + LANES:]
    yield

    dtp = dt_ref[rows, :] + dtb_ref[...]
    dt = jnp.maximum(dtp, 0.0) + jnp.log(1.0 + jnp.exp(-jnp.abs(dtp)))
    a = dt * (-jnp.exp(alog_ref[...]))
    row = lax.broadcasted_iota(jnp.int32, (t, t), 0)
    col = lax.broadcasted_iota(jnp.int32, (t, t), 1)
    causal = row >= col
    tri = jnp.where(causal, 1.0, 0.0).astype(BF16)
    a_hi, a_mid, a_lo = _split3(a)
    acs = (jnp.dot(tri, a_hi, preferred_element_type=F32)
           + jnp.dot(tri, a_mid, preferred_element_type=F32)
           + jnp.dot(tri, a_lo, preferred_element_type=F32))
    acs_t = acs.T
    acs_x = _per_head(acs, HEAD_DIM, w)
    last_x = acs_x[t - 1:t, :]
    x_dt = xs * _per_head(dt, HEAD_DIM, w)
    to_end_x = jnp.exp(last_x - acs_x)
    from_start_x = jnp.exp(acs_x)
    chunk_decay_x = jnp.exp(last_x)
    yield

    lane = lax.broadcasted_iota(jnp.int32, (1, LANES), 1)
    low = lane < HEAD_DIM
    c_bf = c_in.astype(BF16)
    nt = (((1,), (1,)), ((), ()))
    for g in range(SSM_GROUPS):
        gmask = low if g == 0 else jnp.logical_not(low)
        sl = slice(g * LANES, (g + 1) * LANES)
        b_g = jnp.where(gmask, b_in, 0.0)
        cb = lax.dot_general(c_bf, b_g.astype(BF16), nt, preferred_element_type=F32)
        xg = x_dt[:, sl]
        y = jnp.zeros((t, LANES), F32)
        for e in range(2):
            h = 2 * g + e
            seg = acs[:, h:h + 1] - acs_t[h:h + 1, :]
            decay = jnp.exp(jnp.where(causal, seg, -jnp.inf))
            emask = low if e == 0 else jnp.logical_not(low)
            xe = jnp.where(emask, xg, 0.0).astype(BF16)
            y = y + jnp.dot((cb * decay).astype(BF16), xe, preferred_element_type=F32)
        yield
        h_enter = jnp.where(first, 0.0, hst[g])
        y = y + jnp.dot(c_bf, h_enter.astype(BF16), preferred_element_type=F32) * from_start_x[:, sl]
        state = jnp.dot(b_g.T.astype(BF16), (xg * to_end_x[:, sl]).astype(BF16),
                        preferred_element_type=F32)
        hst[g] = h_enter * chunk_decay_x[:, sl] + state

        y = y + xs[:, sl] * dskip_ref[:, sl]
        zg = loc_ref[rows, LOC_Z + g * LANES:LOC_Z + (g + 1) * LANES].astype(F32)
        gated = y * (zg * _sigmoid(zg))
        ms = jnp.mean(gated * gated, axis=-1, keepdims=True)
        yc_ref[rows, sl] = (gated * lax.rsqrt(ms + RMS_EPS) * ng_ref[:, sl]).astype(yc_ref.dtype)
        yield


def _inproj_mixers_kernel(x_ref, g_ref, w_ref, wdt_ref, cwa_ref, cwc_ref, cbias_ref, dtb_ref, alog_ref,
                          dskip_ref, ng_ref, u_ref, ya_ref, yc_ref,
                          h_ref, loc_ref, dt_ref, loc_new, dt_new, bufa, bufc, hst, *, tn, chunks_per_seq):
    i = pl.program_id(0)
    tm = x_ref.shape[0]
    chunks_per_tile = tm // SSM_CHUNK

    @pl.when(i == 0)
    def _():
        loc_ref[...] = jnp.zeros_like(loc_ref)
        dt_ref[...] = jnp.zeros_like(dt_ref)
        bufa[...] = jnp.zeros_like(bufa)
        bufc[...] = jnp.zeros_like(bufc)
        hst[...] = jnp.zeros_like(hst)

    def mixer_phases():
        for c in range(chunks_per_tile):
            chunk = (i - 1) * chunks_per_tile + c
            first = lax.rem(chunk + chunks_per_seq, chunks_per_seq) == 0
            yield from _mixer_chunk(loc_ref, dt_ref, slice(c * SSM_CHUNK, (c + 1) * SSM_CHUNK), first,
                                    cwa_ref, cwc_ref, cbias_ref, dtb_ref, alog_ref, dskip_ref, ng_ref,
                                    ya_ref, yc_ref, bufa, bufc, hst)

    mixers = mixer_phases()
    x = x_ref[...]
    ms = jnp.mean(x * x, axis=-1, keepdims=True)
    h_ref[...] = (x * lax.rsqrt(ms + RMS_EPS) * g_ref[...]).astype(BF16)
    dt_new[...] = jnp.dot(h_ref[...], wdt_ref[...], preferred_element_type=F32)
    for c in range(N_PACK // tn):
        next(mixers, None)
        sl = slice(c * tn, (c + 1) * tn)
        acc = jnp.dot(h_ref[...], w_ref[:, sl], preferred_element_type=F32).astype(BF16)
        if c * tn < N_ATT:
            u_ref[:, sl] = acc
        else:
            loc_new[:, c * tn - N_ATT:(c + 1) * tn - N_ATT] = acc
    for _ in mixers:
        pass

    loc_ref[...] = loc_new[...]
    dt_ref[...] = dt_new[...]


def _inproj_mixers(x, g, w, wdt, cwa, cwc, cbias, dtb, alog, dskip, ng, seq, tm, tn):
    m = x.shape[0]
    n = m // tm
    assert tm % SSM_CHUNK == 0 and seq % tm == 0 and N_ATT % tn == 0 and N_LOC % tn == 0
    wbr = BRANCH_W
    resident = pl.Buffered(1)
    small = lambda shape: pl.BlockSpec(shape, lambda i: (0, 0))
    this_tile = lambda i: (jnp.minimum(i, n - 1), 0)
    prev_tile = lambda i: (jnp.maximum(i - 1, 0), 0)
    return pl.pallas_call(
        functools.partial(_inproj_mixers_kernel, tn=tn, chunks_per_seq=seq // SSM_CHUNK),
        grid=(n + 1,),
        in_specs=[
            pl.BlockSpec((tm, D_MODEL), this_tile),
            small((1, D_MODEL)),
            pl.BlockSpec((D_MODEL, N_PACK), lambda i: (0, 0), pipeline_mode=resident),
            small((D_MODEL, LANES)),
            small((SC_K, wbr)), small((SSM_CONV_K, SSM_CONV_DIM)), small((1, SSM_CONV_DIM)),
            small((1, LANES)), small((1, LANES)), small((1, wbr)), small((1, wbr)),
        ],
        out_specs=[
            pl.BlockSpec((tm, N_ATT), this_tile),
            pl.BlockSpec((tm, wbr), prev_tile),
            pl.BlockSpec((tm, wbr), prev_tile),
        ],
        out_shape=[jax.ShapeDtypeStruct((m, N_ATT), BF16),
                   jax.ShapeDtypeStruct((m, wbr), BF16),
                   jax.ShapeDtypeStruct((m, wbr), BF16)],
        scratch_shapes=[
            pltpu.VMEM((tm, D_MODEL), BF16),
            pltpu.VMEM((tm, N_LOC), BF16),
            pltpu.VMEM((tm, LANES), F32),
            pltpu.VMEM((tm, N_LOC), BF16),
            pltpu.VMEM((tm, LANES), F32),
            pltpu.VMEM((SSM_CHUNK + HIST, wbr), F32),
            pltpu.VMEM((SSM_CHUNK + HIST, SSM_CONV_DIM), F32),
            pltpu.VMEM((SSM_GROUPS, LANES, LANES), F32),
        ],
        compiler_params=_cparams(("arbitrary",)),
        name="inproj_mixers",
    )(x, g, w, wdt, cwa, cwc, cbias, dtb, alog, dskip, ng)


def _merge_kernel(x_ref, g_ref, wg_ref, ya_ref, yb_ref, yc_ref, yd_ref, wb_ref, wo_ref, o_ref, h_ref):
    x = x_ref[...]
    ms = jnp.mean(x * x, axis=-1, keepdims=True)
    h_ref[...] = (x * lax.rsqrt(ms + RMS_EPS) * g_ref[...]).astype(BF16)
    merged = None
    for i, y_ref in enumerate((ya_ref, yb_ref, yc_ref, yd_ref)):
        pre = jnp.dot(h_ref[...], wg_ref[:, i * D_MODEL:(i + 1) * D_MODEL], preferred_element_type=F32)
        term = (1.0 + jnp.tanh(0.5 * pre)) * jnp.dot(y_ref[...], wb_ref[i], preferred_element_type=F32)
        merged = term if merged is None else merged + term
    merged = (0.5 * merged).astype(BF16)
    o_ref[...] = x_ref[...] + jnp.dot(merged, wo_ref[...], preferred_element_type=F32)


def _merge(x, g, wg, ya, yb, yc, yd, wb, wo, tm):
    m = x.shape[0]
    ybr = pl.BlockSpec((tm, BRANCH_W), lambda i: (i, 0))
    resident = pl.Buffered(1)
    return pl.pallas_call(
        _merge_kernel,
        grid=(m // tm,),
        in_specs=[
            pl.BlockSpec((tm, D_MODEL), lambda i: (i, 0)),
            pl.BlockSpec((1, D_MODEL), lambda i: (0, 0)),
            pl.BlockSpec((D_MODEL, N_BRANCH * D_MODEL), lambda i: (0, 0), pipeline_mode=resident),
            ybr, ybr, ybr, ybr,
            pl.BlockSpec((N_BRANCH, BRANCH_W, D_MODEL), lambda i: (0, 0, 0), pipeline_mode=resident),
            pl.BlockSpec((D_MODEL, D_MODEL), lambda i: (0, 0), pipeline_mode=resident),
        ],
        out_specs=pl.BlockSpec((tm, D_MODEL), lambda i: (i, 0)),
        out_shape=jax.ShapeDtypeStruct((m, D_MODEL), F32),
        scratch_shapes=[pltpu.VMEM((tm, D_MODEL), BF16)],
        compiler_params=_cparams(("parallel",)),
        name="merge",
    )(x, g, wg, ya, yb, yc, yd, wb, wo)


def _ffn_kernel(x_ref, g_ref, wgu_ref, wd_ref, fg_ref, o_ref, h_ref, act_ref, *, th, final_norm):
    x = x_ref[...]
    ms = jnp.mean(x * x, axis=-1, keepdims=True)
    h_ref[...] = (x * lax.rsqrt(ms + RMS_EPS) * g_ref[...]).astype(BF16)
    for c in range(FFN_HIDDEN // th):
        h = h_ref[...]
        gate = jnp.dot(h, wgu_ref[:, c * th:(c + 1) * th], preferred_element_type=F32)
        up = jnp.dot(h, wgu_ref[:, FFN_HIDDEN + c * th:FFN_HIDDEN + (c + 1) * th],
                     preferred_element_type=F32)
        act_ref[:, c * th:(c + 1) * th] = ((gate * _sigmoid(gate)) * up).astype(BF16)
    y = x_ref[...] + jnp.dot(act_ref[...], wd_ref[...], preferred_element_type=F32)
    if final_norm:
        ms = jnp.mean(y * y, axis=-1, keepdims=True)
        y = y * lax.rsqrt(ms + RMS_EPS) * fg_ref[...]
    o_ref[...] = y


def _ffn(x, g, wgu, wd, final_g, tm, th, final_norm):
    m = x.shape[0]
    resident = pl.Buffered(1)
    return pl.pallas_call(
        functools.partial(_ffn_kernel, th=th, final_norm=final_norm),
        grid=(m // tm,),
        in_specs=[
            pl.BlockSpec((tm, D_MODEL), lambda i: (i, 0)),
            pl.BlockSpec((1, D_MODEL), lambda i: (0, 0)),
            pl.BlockSpec((D_MODEL, 2 * FFN_HIDDEN), lambda i: (0, 0), pipeline_mode=resident),
            pl.BlockSpec((FFN_HIDDEN, D_MODEL), lambda i: (0, 0), pipeline_mode=resident),
            pl.BlockSpec((1, D_MODEL), lambda i: (0, 0)),
        ],
        out_specs=pl.BlockSpec((tm, D_MODEL), lambda i: (i, 0)),
        out_shape=jax.ShapeDtypeStruct((m, D_MODEL), F32),
        scratch_shapes=[pltpu.VMEM((tm, D_MODEL), BF16), pltpu.VMEM((tm, FFN_HIDDEN), BF16)],
        compiler_params=_cparams(("parallel",)),
        name="ffn",
    )(x, g, wgu, wd, final_g)


def _row_tile(m, want):
    while m % want:
        want //= 2
    return want


def _pad_lanes(v):
    return jnp.pad(v.astype(F32), (0, LANES - v.shape[0]))[None, :]


def kernel(x, norm1_g, w_in, conv_a_w, ssm_conv_w, ssm_conv_b, ssm_dt_bias, ssm_a_log, ssm_d,
           ssm_norm_g, w_branch, w_o, norm2_g, w_gate_up, w_down, final_g):
    bsz, seq, _ = x.shape
    assert seq % MOBA_BLOCK == 0 and seq % SSM_CHUNK == 0
    depth = w_in.shape[0]
    assert depth >= 1
    m = bsz * seq
    b_col = 3 * BRANCH_W
    z_col = b_col + 3 * BRANCH_W
    dt_col = z_col + BRANCH_W + SSM_CONV_DIM
    d_col = dt_col + N_HEADS
    g_col = d_col + 3 * BRANCH_W
    h = x.reshape(m, D_MODEL)
    tm = _row_tile(m, 512)
    for l in range(depth):
        w_main = jnp.concatenate([w_in[l, :, b_col:z_col], w_in[l, :, d_col:g_col],
                                  w_in[l, :, :b_col], w_in[l, :, z_col:dt_col]], axis=1).astype(BF16)
        w_gate = w_in[l, :, g_col:].astype(BF16)
        w_dt = jnp.pad(w_in[l, :, dt_col:d_col], ((0, 0), (0, LANES - N_HEADS))).astype(BF16)
        u, y_a, y_c = _inproj_mixers(h, norm1_g[l][None, :], w_main, w_dt, conv_a_w[l], ssm_conv_w[l],
                                     ssm_conv_b[l][None, :], _pad_lanes(ssm_dt_bias[l]),
                                     _pad_lanes(ssm_a_log[l]), jnp.repeat(ssm_d[l], HEAD_DIM)[None, :],
                                     ssm_norm_g[l][None, :], seq, tm, 256)
        y_b, y_d = _attention(u, bsz, seq)
        h = _merge(h, norm1_g[l][None, :], w_gate, y_a, y_b, y_c, y_d,
                   w_branch[l].astype(BF16), w_o[l].astype(BF16), tm)
        h = _ffn(h, norm2_g[l][None, :], w_gate_up[l].astype(BF16), w_down[l].astype(BF16),
                 final_g[None, :], tm, 256, final_norm=(l == depth - 1))
    return h.reshape(bsz, seq, D_MODEL)
```

```python
import functools

import jax
import jax.numpy as jnp
from jax import lax
from jax.experimental import pallas as pl
from jax.experimental.pallas import tpu as pltpu

F32 = jnp.float32
BF16 = jnp.bfloat16

D_MODEL = 1024
HEAD_DIM = 64
BRANCH_W = 256
N_BRANCH = 4
N_HEADS = 4
SC_K = 3
SSM_GROUPS = 2
SSM_STATE = 64
SSM_CONV_K = 4
SSM_CHUNK = 256
SSM_CONV_DIM = BRANCH_W + 2 * SSM_GROUPS * SSM_STATE
MOBA_BLOCK = 256
MOBA_TOPK = 3
FFN_HIDDEN = 2816
RMS_EPS = 1e-6

LANES = 128
MXU_COLS = 256
ROW_TILE = 512
NO_BLOCK = 1e9
HIST = 8
BF16_ROWS = 16
NEG = -1e30

N_ATT = 1536
COL_SB_Q, COL_SB_K, COL_SB_V = 0, 2, 4
COL_MO_Q, COL_MO_K, COL_MO_V = 6, 8, 10
N_LOC = 1536
LOC_Z = 3 * BRANCH_W
LOC_XBC = 4 * BRANCH_W
N_PACK = N_ATT + N_LOC

VMEM_LIMIT = 56 * 1024 * 1024


def _cparams(sem):
    return pltpu.CompilerParams(dimension_semantics=sem, vmem_limit_bytes=VMEM_LIMIT)


def _sigmoid(x):
    return 1.0 / (1.0 + jnp.exp(-x))


def _neg_abs(x):
    return pltpu.bitcast(pltpu.bitcast(x, jnp.uint32) | jnp.uint32(0x80000000), F32)


def _attn_kernel(qs_ref, ks_ref, vs_ref, qm_ref, km_ref, vm_ref, os_ref, om_ref,
                 s_vt, s_sp, s_d, s_w, s_r, s_acc,
                 m_vt, m_mean, m_sel, m_p, m_alpha, m_max, m_acc, *, ke):
    t = MOBA_BLOCK
    qi = pl.program_id(1)
    n_tiles = qi + 1
    ones_row = (HEAD_DIM, 0)
    rows = lax.broadcasted_iota(jnp.int32, (LANES, t), 0)
    head0_rows = rows < HEAD_DIM
    heads = range(N_HEADS)

    def pair_lanes(h):
        return slice((h // 2) * LANES, (h // 2 + 1) * LANES)

    def own_rows(h, x, other):
        return jnp.where(head0_rows, x, other) if h % 2 == 0 else jnp.where(head0_rows, other, x)

    @pl.when(qi == 0)
    def _():
        m_mean[...] = jnp.zeros_like(m_mean)

    own_keys = pl.ds(pl.multiple_of(qi * t, t), t)
    for h in heads:
        vt = vs_ref[own_keys, pair_lanes(h)].astype(F32).T
        s_vt[h, qi] = own_rows(h, vt, 0.0).astype(BF16)
        vt = vm_ref[own_keys, pair_lanes(h)].astype(F32).T
        m_vt[h, qi] = own_rows(h, vt, jnp.where(rows == ones_row[h % 2], 1.0, 0.0)).astype(BF16)
    own_mean = jnp.sum(km_ref[own_keys, :].astype(F32), axis=0, keepdims=True) * (1.0 / t)

    def split_heads(q_t):
        return tuple(own_rows(h, q_t[(h // 2) * LANES:(h // 2 + 1) * LANES, :], 0.0).astype(BF16) for h in heads)

    s_qt = split_heads((qs_ref[...].astype(F32) * (HEAD_DIM ** -0.5)).T)
    m_qt = split_heads(qm_ref[...].astype(F32).T)
    m_qst = tuple(x * (HEAD_DIM ** -0.5) for x in m_qt)

    mean = m_mean[...]
    mean_hi = mean.astype(BF16)
    mean_lo = (mean - mean_hi.astype(F32)).astype(BF16)
    blk = lax.broadcasted_iota(jnp.int32, (m_mean.shape[0], t), 0).astype(F32)
    qif = qi.astype(F32)
    for h in heads:
        gate = (jnp.dot(mean_hi[:, pair_lanes(h)], m_qt[h], preferred_element_type=F32)
                + jnp.dot(mean_lo[:, pair_lanes(h)], m_qt[h], preferred_element_type=F32))
        g = jnp.where(blk < qif, gate, -jnp.inf)
        sel = jnp.zeros_like(gate)
        for r in range(ke):
            mx = jnp.max(g, axis=0, keepdims=True)
            idx = jnp.min(jnp.where(g == mx, blk, NO_BLOCK), axis=0, keepdims=True)
            hit = blk == idx
            sel = jnp.where(jnp.logical_and(hit, qif > r), 1.0, sel)
            g = jnp.where(hit, -jnp.inf, g)
        m_sel[h] = sel
    m_mean[pl.ds(qi, 1), :] = own_mean

    row = lax.broadcasted_iota(jnp.int32, (t, t), 0)
    col = lax.broadcasted_iota(jnp.int32, (t, t), 1)
    later_t = jnp.where(col > row, 1.0, 0.0).astype(BF16)

    def tile_of(p):
        return jnp.maximum(qi - p, 0)

    def key_rows(p):
        return pl.ds(pl.multiple_of(tile_of(p) * t, t), t)

    def s_scores(p, par, masked):
        k = ks_ref[key_rows(p), :]
        for h in heads:
            z = jnp.dot(k[:, pair_lanes(h)], s_qt[h], preferred_element_type=F32)
            if masked:
                z = jnp.where(row < col, z, NEG)
            sp = jnp.maximum(z, 0.0) + jnp.log(1.0 + jnp.exp(_neg_abs(z)))
            s_sp[par, h] = sp.astype(BF16)
            s_d[par, h] = z - sp

    def s_weights(par):
        for h in heads:
            sp = s_sp[par, h]
            between = jnp.dot(later_t, sp, preferred_element_type=F32)
            r = s_r[h]
            s_w[par, h] = jnp.exp(s_d[par, h] - between - r).astype(BF16)
            s_r[h] = r + between[0:1, :] + sp[0:1, :].astype(F32)

    def s_value(p, par):
        tile = tile_of(p)
        for i in range(N_HEADS // 2):
            s_acc[i] += (jnp.dot(s_vt[2 * i, tile], s_w[par, 2 * i], preferred_element_type=F32)
                         + jnp.dot(s_vt[2 * i + 1, tile], s_w[par, 2 * i + 1], preferred_element_type=F32))

    def m_scores(p, par, own):
        k = km_ref[key_rows(p), :]
        for h in heads:
            s = jnp.dot(k[:, pair_lanes(h)], m_qst[h], preferred_element_type=F32)
            if own:
                s = jnp.where(row <= col, s, -jnp.inf)
                m_new = jnp.max(s, axis=0, keepdims=True)
                shift = m_new
                m_alpha[par, h] = jnp.zeros((1, t), F32)
            else:
                chosen = m_sel[h, pl.ds(tile_of(p), 1), :] > 0.5
                m_old = m_max[h]
                m_tile = jnp.max(s, axis=0, keepdims=True)
                m_new = jnp.where(chosen, jnp.maximum(m_old, m_tile), m_old)
                shift = jnp.where(chosen, m_new, jnp.inf)
                m_alpha[par, h] = jnp.exp(m_old - m_new)
            m_p[par, h] = jnp.exp(s - shift).astype(BF16)
            m_max[h] = m_new

    def m_value(p, par):
        tile = tile_of(p)
        for h in heads:
            m_acc[h] = (m_acc[h] * m_alpha[par, h]
                        + jnp.dot(m_vt[h, tile], m_p[par, h], preferred_element_type=F32))

    def step(s, par):
        s_scores(s, par, False)
        m_scores(s - 1, 1 - par, False)
        s_weights(1 - par)
        s_value(s - 2, par)
        m_value(s - 2, par)

    s_acc[...] = jnp.zeros_like(s_acc)
    s_r[...] = jnp.zeros_like(s_r)
    m_acc[...] = jnp.zeros_like(m_acc)
    s_scores(0, 0, True)
    s_weights(0)
    m_scores(0, 0, True)
    s_scores(1, 1, False)

    n_full = jnp.maximum(n_tiles - 2, 0)

    def body(i, carry):
        s = 2 + 2 * i
        step(s, 0)
        step(s + 1, 1)
        return carry

    lax.fori_loop(0, n_full // 2, body, 0)

    @pl.when(n_full % 2 == 1)
    def _():
        step(n_tiles - 1, 0)

    def drain(par_t, first):
        if first <= 0:
            s_value(n_tiles - 2, par_t)
            m_value(n_tiles - 2, par_t)
            s_weights(1 - par_t)
            m_scores(n_tiles - 1, 1 - par_t, False)
        s_value(n_tiles - 1, 1 - par_t)
        m_value(n_tiles - 1, 1 - par_t)

    for par_t in range(2):
        @pl.when(jnp.logical_and(n_tiles >= 2, n_tiles % 2 == par_t))
        def _():
            drain(par_t, 0)

    @pl.when(n_tiles == 1)
    def _():
        drain(1, 1)

    for i in range(N_HEADS // 2):
        sl = slice(i * LANES, (i + 1) * LANES)
        os_ref[:, sl] = s_acc[i].T.astype(os_ref.dtype)
        a0 = m_acc[2 * i]
        a1 = m_acc[2 * i + 1]
        out_t = jnp.where(head0_rows, a0 / a0[ones_row[0]:ones_row[0] + 1, :],
                          a1 / a1[ones_row[1]:ones_row[1] + 1, :])
        om_ref[:, sl] = out_t.T.astype(om_ref.dtype)


def _attention(u, bsz, seq):
    t = MOBA_BLOCK
    w = BRANCH_W
    nb = seq // t
    nb_rows = -(-nb // BF16_ROWS) * BF16_ROWS
    ke = max(1, min(MOBA_TOPK, nb - 1))
    tile_f32 = pltpu.VMEM((2, N_HEADS, t, t), F32)
    tile_bf16 = pltpu.VMEM((2, N_HEADS, t, t), BF16)
    vt_tiles = pltpu.VMEM((N_HEADS, nb, LANES, t), BF16)
    resident = pl.Buffered(1)

    def q_spec(col):
        return pl.BlockSpec((t, w), lambda b, qi: (b * nb + qi, col * LANES // w))

    def kv_spec(col):
        return pl.BlockSpec((seq, w), lambda b, qi: (b, col * LANES // w), pipeline_mode=resident)

    out_spec = pl.BlockSpec((t, w), lambda b, qi: (b * nb + qi, 0))
    out_shape = jax.ShapeDtypeStruct((bsz * seq, w), BF16)
    return pl.pallas_call(
        functools.partial(_attn_kernel, ke=ke),
        grid=(bsz, nb),
        in_specs=[q_spec(COL_SB_Q), kv_spec(COL_SB_K), kv_spec(COL_SB_V),
                  q_spec(COL_MO_Q), kv_spec(COL_MO_K), kv_spec(COL_MO_V)],
        out_specs=[out_spec, out_spec],
        out_shape=[out_shape, out_shape],
        scratch_shapes=[
            vt_tiles,
            tile_bf16,
            tile_f32,
            tile_bf16,
            pltpu.VMEM((N_HEADS, 1, t), F32),
            pltpu.VMEM((N_HEADS // 2, LANES, t), F32),
            vt_tiles,
            pltpu.VMEM((nb_rows, w), F32),
            pltpu.VMEM((N_HEADS, nb_rows, t), F32),
            tile_bf16,
            pltpu.VMEM((2, N_HEADS, 1, t), F32),
            pltpu.VMEM((N_HEADS, 1, t), F32),
            pltpu.VMEM((N_HEADS, LANES, t), F32),
        ],
        compiler_params=_cparams(("parallel", "arbitrary")),
        name="attention",
    )(u, u, u, u, u, u)


def _per_head(v, lanes_per_head, width):
    head = lax.broadcasted_iota(jnp.int32, (1, width), 1) // lanes_per_head
    out = jnp.zeros((v.shape[0], width), F32)
    for h in range(N_HEADS):
        out = jnp.where(head == h, v[:, h:h + 1], out)
    return out


def _split3(a):
    hi = a.astype(BF16)
    r1 = a - hi.astype(F32)
    mid = r1.astype(BF16)
    lo = (r1 - mid.astype(F32)).astype(BF16)
    return hi, mid, lo


def _mixer_chunk(loc_ref, dt_ref, rows, first, cwa_ref, cwc_ref, cbias_ref, dtb_ref, alog_ref,
                 dskip_ref, ng_ref, ya_ref, yc_ref, bufa, bufc, hst):
    t = SSM_CHUNK
    w = BRANCH_W

    ua = loc_ref[rows, 0:3 * w].astype(F32)
    bufa[0:HIST, :] = jnp.where(first, 0.0, bufa[0:HIST, :])
    bufa[HIST:, :] = ua[:, 2 * w:] * ua[:, :w]
    conv = cwa_ref[0:1, :] * bufa[pl.ds(HIST - SC_K + 1, t), :]
    for kk in range(1, SC_K):
        conv = conv + cwa_ref[kk:kk + 1, :] * bufa[pl.ds(HIST - SC_K + 1 + kk, t), :]
    ya_ref[rows, :] = (ua[:, w:2 * w] * conv).astype(ya_ref.dtype)
    bufa[0:HIST, :] = bufa[t:t + HIST, :]
    yield

    bufc[0:HIST, :] = jnp.where(first, 0.0, bufc[0:HIST, :])
    bufc[HIST:, :] = loc_ref[rows, LOC_XBC:LOC_XBC + SSM_CONV_DIM].astype(F32)
    xc = cbias_ref[...] + cwc_ref[0:1, :] * bufc[pl.ds(HIST - SSM_CONV_K + 1, t), :]
    for kk in range(1, SSM_CONV_K):
        xc = xc + cwc_ref[kk:kk + 1, :] * bufc[pl.ds(HIST - SSM_CONV_K + 1 + kk, t), :]
    bufc[0:HIST, :] = bufc[t:t + HIST, :]
    xc = xc * _sigmoid(xc)
    xs = xc[:, :w]
    b_in = xc[:, w:w + LANES]
    c_in = xc[:, w + LANES:]
    yield

    dtp = dt_ref[rows, :] + dtb_ref[...]
    dt = jnp.maximum(dtp, 0.0) + jnp.log(1.0 + jnp.exp(-jnp.abs(dtp)))
    a = dt * (-jnp.exp(alog_ref[...]))
    row = lax.broadcasted_iota(jnp.int32, (t, t), 0)
    col = lax.broadcasted_iota(jnp.int32, (t, t), 1)
    causal = row >= col
    tri = jnp.where(causal, 1.0, 0.0).astype(BF16)
    a_hi, a_mid, a_lo = _split3(a)
    acs = (jnp.dot(tri, a_hi, preferred_element_type=F32)
           + jnp.dot(tri, a_mid, preferred_element_type=F32)
           + jnp.dot(tri, a_lo, preferred_element_type=F32))
    acs_t = acs.T
    acs_x = _per_head(acs, HEAD_DIM, w)
    last_x = acs_x[t - 1:t, :]
    x_dt = xs * _per_head(dt, HEAD_DIM, w)
    to_end_x = jnp.exp(last_x - acs_x)
    from_start_x = jnp.exp(acs_x)
    chunk_decay_x = jnp.exp(last_x)
    yield

    lane = lax.broadcasted_iota(jnp.int32, (1, LANES), 1)
    low = lane < HEAD_DIM
    c_bf = c_in.astype(BF16)
    nt = (((1,), (1,)), ((), ()))
    for g in range(SSM_GROUPS):
        gmask = low if g == 0 else jnp.logical_not(low)
        sl = slice(g * LANES, (g + 1) * LANES)
        b_g = jnp.where(gmask, b_in, 0.0)
        cb = lax.dot_general(c_bf, b_g.astype(BF16), nt, preferred_element_type=F32)
        xg = x_dt[:, sl]
        y = jnp.zeros((t, LANES), F32)
        for e in range(2):
            h = 2 * g + e
            seg = acs[:, h:h + 1] - acs_t[h:h + 1, :]
            decay = jnp.exp(jnp.where(causal, seg, -jnp.inf))
            emask = low if e == 0 else jnp.logical_not(low)
            xe = jnp.where(emask, xg, 0.0).astype(BF16)
            y = y + jnp.dot((cb * decay).astype(BF16), xe, preferred_element_type=F32)
        yield
        h_enter = jnp.where(first, 0.0, hst[g])
        y = y + jnp.dot(c_bf, h_enter.astype(BF16), preferred_element_type=F32) * from_start_x[:, sl]
        state = jnp.dot(b_g.T.astype(BF16), (xg * to_end_x[:, sl]).astype(BF16),
                        preferred_element_type=F32)
        hst[g] = h_enter * chunk_decay_x[:, sl] + state

        y = y + xs[:, sl] * dskip_ref[:, sl]
        zg = loc_ref[rows, LOC_Z + g * LANES:LOC_Z + (g + 1) * LANES].astype(F32)
        gated = y * (zg * _sigmoid(zg))
        ms = jnp.mean(gated * gated, axis=-1, keepdims=True)
        yc_ref[rows, sl] = (gated * lax.rsqrt(ms + RMS_EPS) * ng_ref[:, sl]).astype(yc_ref.dtype)
        yield


def _inproj_mixers_kernel(x_ref, g_ref, w_ref, wdt_ref, cwa_ref, cwc_ref, cbias_ref, dtb_ref, alog_ref,
                          dskip_ref, ng_ref, u_ref, ya_ref, yc_ref,
                          h_ref, loc_ref, dt_ref, loc_new, dt_new, bufa, bufc, hst, *, tn, chunks_per_seq):
    i = pl.program_id(0)
    tm = x_ref.shape[0]
    chunks_per_tile = tm // SSM_CHUNK

    @pl.when(i == 0)
    def _():
        loc_ref[...] = jnp.zeros_like(loc_ref)
        dt_ref[...] = jnp.zeros_like(dt_ref)
        bufa[...] = jnp.zeros_like(bufa)
        bufc[...] = jnp.zeros_like(bufc)
        hst[...] = jnp.zeros_like(hst)

    def mixer_phases():
        for c in range(chunks_per_tile):
            chunk = (i - 1) * chunks_per_tile + c
            first = lax.rem(chunk + chunks_per_seq, chunks_per_seq) == 0
            yield from _mixer_chunk(loc_ref, dt_ref, slice(c * SSM_CHUNK, (c + 1) * SSM_CHUNK), first,
                                    cwa_ref, cwc_ref, cbias_ref, dtb_ref, alog_ref, dskip_ref, ng_ref,
                                    ya_ref, yc_ref, bufa, bufc, hst)

    mixers = mixer_phases()
    x = x_ref[...]
    ms = jnp.mean(x * x, axis=-1, keepdims=True)
    h_ref[...] = (x * lax.rsqrt(ms + RMS_EPS) * g_ref[...]).astype(BF16)
    dt_new[...] = jnp.dot(h_ref[...], wdt_ref[...], preferred_element_type=F32)
    for c in range(N_PACK // tn):
        next(mixers, None)
        sl = slice(c * tn, (c + 1) * tn)
        acc = jnp.dot(h_ref[...], w_ref[:, sl], preferred_element_type=F32).astype(BF16)
        if c * tn < N_ATT:
            u_ref[:, sl] = acc
        else:
            loc_new[:, c * tn - N_ATT:(c + 1) * tn - N_ATT] = acc
    for _ in mixers:
        pass

    loc_ref[...] = loc_new[...]
    dt_ref[...] = dt_new[...]


def _inproj_mixers(x, g, w, wdt, cwa, cwc, cbias, dtb, alog, dskip, ng, seq, tm, tn):
    m = x.shape[0]
    n = m // tm
    assert tm % SSM_CHUNK == 0 and seq % tm == 0 and N_ATT % tn == 0 and N_LOC % tn == 0
    wbr = BRANCH_W
    resident = pl.Buffered(1)
    small = lambda shape: pl.BlockSpec(shape, lambda i: (0, 0))
    this_tile = lambda i: (jnp.minimum(i, n - 1), 0)
    prev_tile = lambda i: (jnp.maximum(i - 1, 0), 0)
    return pl.pallas_call(
        functools.partial(_inproj_mixers_kernel, tn=tn, chunks_per_seq=seq // SSM_CHUNK),
        grid=(n + 1,),
        in_specs=[
            pl.BlockSpec((tm, D_MODEL), this_tile),
            small((1, D_MODEL)),
            pl.BlockSpec((D_MODEL, N_PACK), lambda i: (0, 0), pipeline_mode=resident),
            small((D_MODEL, LANES)),
            small((SC_K, wbr)), small((SSM_CONV_K, SSM_CONV_DIM)), small((1, SSM_CONV_DIM)),
            small((1, LANES)), small((1, LANES)), small((1, wbr)), small((1, wbr)),
        ],
        out_specs=[
            pl.BlockSpec((tm, N_ATT), this_tile),
            pl.BlockSpec((tm, wbr), prev_tile),
            pl.BlockSpec((tm, wbr), prev_tile),
        ],
        out_shape=[jax.ShapeDtypeStruct((m, N_ATT), BF16),
                   jax.ShapeDtypeStruct((m, wbr), BF16),
                   jax.ShapeDtypeStruct((m, wbr), BF16)],
        scratch_shapes=[
            pltpu.VMEM((tm, D_MODEL), BF16),
            pltpu.VMEM((tm, N_LOC), BF16),
            pltpu.VMEM((tm, LANES), F32),
            pltpu.VMEM((tm, N_LOC), BF16),
            pltpu.VMEM((tm, LANES), F32),
            pltpu.VMEM((SSM_CHUNK + HIST, wbr), F32),
            pltpu.VMEM((SSM_CHUNK + HIST, SSM_CONV_DIM), F32),
            pltpu.VMEM((SSM_GROUPS, LANES, LANES), F32),
        ],
        compiler_params=_cparams(("arbitrary",)),
        name="inproj_mixers",
    )(x, g, w, wdt, cwa, cwc, cbias, dtb, alog, dskip, ng)


def _merge_kernel(x_ref, g_ref, wg_ref, ya_ref, yb_ref, yc_ref, yd_ref, wb_ref, wo_ref, o_ref, h_ref):
    x = x_ref[...]
    ms = jnp.mean(x * x, axis=-1, keepdims=True)
    h_ref[...] = (x * lax.rsqrt(ms + RMS_EPS) * g_ref[...]).astype(BF16)
    merged = None
    for i, y_ref in enumerate((ya_ref, yb_ref, yc_ref, yd_ref)):
        pre = jnp.dot(h_ref[...], wg_ref[:, i * D_MODEL:(i + 1) * D_MODEL], preferred_element_type=F32)
        term = (1.0 + jnp.tanh(0.5 * pre)) * jnp.dot(y_ref[...], wb_ref[i], preferred_element_type=F32)
        merged = term if merged is None else merged + term
    merged = (0.5 * merged).astype(BF16)
    o_ref[...] = x_ref[...] + jnp.dot(merged, wo_ref[...], preferred_element_type=F32)


def _merge(x, g, wg, ya, yb, yc, yd, wb, wo, tm):
    m = x.shape[0]
    ybr = pl.BlockSpec((tm, BRANCH_W), lambda i: (i, 0))
    resident = pl.Buffered(1)
    return pl.pallas_call(
        _merge_kernel,
        grid=(m // tm,),
        in_specs=[
            pl.BlockSpec((tm, D_MODEL), lambda i: (i, 0)),
            pl.BlockSpec((1, D_MODEL), lambda i: (0, 0)),
            pl.BlockSpec((D_MODEL, N_BRANCH * D_MODEL), lambda i: (0, 0), pipeline_mode=resident),
            ybr, ybr, ybr, ybr,
            pl.BlockSpec((N_BRANCH, BRANCH_W, D_MODEL), lambda i: (0, 0, 0), pipeline_mode=resident),
            pl.BlockSpec((D_MODEL, D_MODEL), lambda i: (0, 0), pipeline_mode=resident),
        ],
        out_specs=pl.BlockSpec((tm, D_MODEL), lambda i: (i, 0)),
        out_shape=jax.ShapeDtypeStruct((m, D_MODEL), F32),
        scratch_shapes=[pltpu.VMEM((tm, D_MODEL), BF16)],
        compiler_params=_cparams(("parallel",)),
        name="merge",
    )(x, g, wg, ya, yb, yc, yd, wb, wo)


def _ffn_kernel(x_ref, g_ref, wgu_ref, wd_ref, fg_ref, o_ref, h_ref, act_ref, *, th, final_norm):
    x = x_ref[...]
    ms = jnp.mean(x * x, axis=-1, keepdims=True)
    h_ref[...] = (x * lax.rsqrt(ms + RMS_EPS) * g_ref[...]).astype(BF16)
    for c in range(FFN_HIDDEN // th):
        h = h_ref[...]
        gate = jnp.dot(h, wgu_ref[:, c * th:(c + 1) * th], preferred_element_type=F32)
        up = jnp.dot(h, wgu_ref[:, FFN_HIDDEN + c * th:FFN_HIDDEN + (c + 1) * th],
                     preferred_element_type=F32)
        act_ref[:, c * th:(c + 1) * th] = ((gate * _sigmoid(gate)) * up).astype(BF16)
    y = x_ref[...] + jnp.dot(act_ref[...], wd_ref[...], preferred_element_type=F32)
    if final_norm:
        ms = jnp.mean(y * y, axis=-1, keepdims=True)
        y = y * lax.rsqrt(ms + RMS_EPS) * fg_ref[...]
    o_ref[...] = y


def _ffn(x, g, wgu, wd, final_g, tm, th, final_norm):
    m = x.shape[0]
    resident = pl.Buffered(1)
    return pl.pallas_call(
        functools.partial(_ffn_kernel, th=th, final_norm=final_norm),
        grid=(m // tm,),
        in_specs=[
            pl.BlockSpec((tm, D_MODEL), lambda i: (i, 0)),
            pl.BlockSpec((1, D_MODEL), lambda i: (0, 0)),
            pl.BlockSpec((D_MODEL, 2 * FFN_HIDDEN), lambda i: (0, 0), pipeline_mode=resident),
            pl.BlockSpec((FFN_HIDDEN, D_MODEL), lambda i: (0, 0), pipeline_mode=resident),
            pl.BlockSpec((1, D_MODEL), lambda i: (0, 0)),
        ],
        out_specs=pl.BlockSpec((tm, D_MODEL), lambda i: (i, 0)),
        out_shape=jax.ShapeDtypeStruct((m, D_MODEL), F32),
        scratch_shapes=[pltpu.VMEM((tm, D_MODEL), BF16), pltpu.VMEM((tm, FFN_HIDDEN), BF16)],
        compiler_params=_cparams(("parallel",)),
        name="ffn",
    )(x, g, wgu, wd, final_g)


def _row_tile(m, want):
    while m % want:
        want //= 2
    return want


def _pad_lanes(v):
    return jnp.pad(v.astype(F32), (0, LANES - v.shape[0]))[None, :]


def kernel(x, norm1_g, w_in, conv_a_w, ssm_conv_w, ssm_conv_b, ssm_dt_bias, ssm_a_log, ssm_d,
           ssm_norm_g, w_branch, w_o, norm2_g, w_gate_up, w_down, final_g):
    bsz, seq, _ = x.shape
    assert seq % MOBA_BLOCK == 0 and seq % SSM_CHUNK == 0
    depth = w_in.shape[0]
    assert depth >= 1
    m = bsz * seq
    b_col = 3 * BRANCH_W
    z_col = b_col + 3 * BRANCH_W
    dt_col = z_col + BRANCH_W + SSM_CONV_DIM
    d_col = dt_col + N_HEADS
    g_col = d_col + 3 * BRANCH_W
    h = x.reshape(m, D_MODEL)
    tm = _row_tile(m, ROW_TILE)
    for l in range(depth):
        w_main = jnp.concatenate([w_in[l, :, b_col:z_col], w_in[l, :, d_col:g_col],
                                  w_in[l, :, :b_col], w_in[l, :, z_col:dt_col]], axis=1).astype(BF16)
        w_gate = w_in[l, :, g_col:].astype(BF16)
        w_dt = jnp.pad(w_in[l, :, dt_col:d_col], ((0, 0), (0, LANES - N_HEADS))).astype(BF16)
        u, y_a, y_c = _inproj_mixers(h, norm1_g[l][None, :], w_main, w_dt, conv_a_w[l], ssm_conv_w[l],
                                     ssm_conv_b[l][None, :], _pad_lanes(ssm_dt_bias[l]),
                                     _pad_lanes(ssm_a_log[l]), jnp.repeat(ssm_d[l], HEAD_DIM)[None, :],
                                     ssm_norm_g[l][None, :], seq, tm, MXU_COLS)
        y_b, y_d = _attention(u, bsz, seq)
        h = _merge(h, norm1_g[l][None, :], w_gate, y_a, y_b, y_c, y_d,
                   w_branch[l].astype(BF16), w_o[l].astype(BF16), tm)
        h = _ffn(h, norm2_g[l][None, :], w_gate_up[l].astype(BF16), w_down[l].astype(BF16),
                 final_g[None, :], tm, MXU_COLS, final_norm=(l == depth - 1))
    return h.reshape(bsz, seq, D_MODEL)
```

```python
import functools

import jax
import jax.numpy as jnp
from jax import lax
from jax.experimental import pallas as pl
from jax.experimental.pallas import tpu as pltpu

F32 = jnp.float32
BF16 = jnp.bfloat16

D_MODEL = 1024
HEAD_DIM = 64
BRANCH_W = 256
N_BRANCH = 4
N_HEADS = 4
SC_K = 3
SSM_GROUPS = 2
SSM_STATE = 64
SSM_CONV_K = 4
SSM_CHUNK = 256
SSM_CONV_DIM = BRANCH_W + 2 * SSM_GROUPS * SSM_STATE
MOBA_BLOCK = 256
MOBA_TOPK = 3
FFN_HIDDEN = 2816
RMS_EPS = 1e-6

LANES = 128
MXU_COLS = 256
ROW_TILE = 512
NO_BLOCK = 1e9
HIST = 8
BF16_ROWS = 16
NEG = -1e30

N_ATT = 1536
COL_SB_Q, COL_SB_K, COL_SB_V = 0, 2, 4
COL_MO_Q, COL_MO_K, COL_MO_V = 6, 8, 10
N_LOC = 1536
LOC_Z = 3 * BRANCH_W
LOC_XBC = 4 * BRANCH_W
N_PACK = N_ATT + N_LOC

VMEM_LIMIT = 56 * 1024 * 1024


def _cparams(sem):
    return pltpu.CompilerParams(dimension_semantics=sem, vmem_limit_bytes=VMEM_LIMIT)


def _sigmoid(x):
    return 1.0 / (1.0 + jnp.exp(-x))


def _neg_abs(x):
    return pltpu.bitcast(pltpu.bitcast(x, jnp.uint32) | jnp.uint32(0x80000000), F32)


def _attn_kernel(qs_ref, ks_ref, vs_ref, qm_ref, km_ref, vm_ref, os_ref, om_ref,
                 s_vt, s_sp, s_d, s_w, s_r, s_acc,
                 m_vt, m_mean, m_sel, m_p, m_alpha, m_max, m_acc, *, ke):
    t = MOBA_BLOCK
    qi = pl.program_id(1)
    n_tiles = qi + 1
    ones_row = (HEAD_DIM, 0)
    rows = lax.broadcasted_iota(jnp.int32, (LANES, t), 0)
    head0_rows = rows < HEAD_DIM
    heads = range(N_HEADS)

    def pair_lanes(h):
        return slice((h // 2) * LANES, (h // 2 + 1) * LANES)

    def own_rows(h, x, other):
        return jnp.where(head0_rows, x, other) if h % 2 == 0 else jnp.where(head0_rows, other, x)

    @pl.when(qi == 0)
    def _():
        m_mean[...] = jnp.zeros_like(m_mean)

    own_keys = pl.ds(pl.multiple_of(qi * t, t), t)
    for h in heads:
        vt = vs_ref[own_keys, pair_lanes(h)].astype(F32).T
        s_vt[h, qi] = own_rows(h, vt, 0.0).astype(BF16)
        vt = vm_ref[own_keys, pair_lanes(h)].astype(F32).T
        m_vt[h, qi] = own_rows(h, vt, jnp.where(rows == ones_row[h % 2], 1.0, 0.0)).astype(BF16)
    own_mean = jnp.sum(km_ref[own_keys, :].astype(F32), axis=0, keepdims=True) * (1.0 / t)

    def split_heads(q_t):
        return tuple(own_rows(h, q_t[(h // 2) * LANES:(h // 2 + 1) * LANES, :], 0.0).astype(BF16) for h in heads)

    s_qt = split_heads((qs_ref[...].astype(F32) * (HEAD_DIM ** -0.5)).T)
    m_qt = split_heads(qm_ref[...].astype(F32).T)
    m_qst = tuple(x * (HEAD_DIM ** -0.5) for x in m_qt)

    mean = m_mean[...]
    mean_hi = mean.astype(BF16)
    mean_lo = (mean - mean_hi.astype(F32)).astype(BF16)
    blk = lax.broadcasted_iota(jnp.int32, (m_mean.shape[0], t), 0).astype(F32)
    qif = qi.astype(F32)
    for h in heads:
        gate = (jnp.dot(mean_hi[:, pair_lanes(h)], m_qt[h], preferred_element_type=F32)
                + jnp.dot(mean_lo[:, pair_lanes(h)], m_qt[h], preferred_element_type=F32))
        g = jnp.where(blk < qif, gate, -jnp.inf)
        sel = jnp.zeros_like(gate)
        for r in range(ke):
            mx = jnp.max(g, axis=0, keepdims=True)
            idx = jnp.min(jnp.where(g == mx, blk, NO_BLOCK), axis=0, keepdims=True)
            hit = blk == idx
            sel = jnp.where(jnp.logical_and(hit, qif > r), 1.0, sel)
            g = jnp.where(hit, -jnp.inf, g)
        m_sel[h] = sel
    m_mean[pl.ds(qi, 1), :] = own_mean

    row = lax.broadcasted_iota(jnp.int32, (t, t), 0)
    col = lax.broadcasted_iota(jnp.int32, (t, t), 1)
    later_t = jnp.where(col > row, 1.0, 0.0).astype(BF16)

    def tile_of(p):
        return jnp.maximum(qi - p, 0)

    def key_rows(p):
        return pl.ds(pl.multiple_of(tile_of(p) * t, t), t)

    def s_scores(p, par, masked):
        k = ks_ref[key_rows(p), :]
        for h in heads:
            z = jnp.dot(k[:, pair_lanes(h)], s_qt[h], preferred_element_type=F32)
            if masked:
                z = jnp.where(row < col, z, NEG)
            sp = jnp.maximum(z, 0.0) + jnp.log(1.0 + jnp.exp(_neg_abs(z)))
            s_sp[par, h] = sp.astype(BF16)
            s_d[par, h] = z - sp

    def s_weights(par):
        for h in heads:
            sp = s_sp[par, h]
            between = jnp.dot(later_t, sp, preferred_element_type=F32)
            r = s_r[h]
            s_w[par, h] = jnp.exp(s_d[par, h] - between - r).astype(BF16)
            s_r[h] = r + between[0:1, :] + sp[0:1, :].astype(F32)

    def s_value(p, par):
        tile = tile_of(p)
        for i in range(N_HEADS // 2):
            s_acc[i] += (jnp.dot(s_vt[2 * i, tile], s_w[par, 2 * i], preferred_element_type=F32)
                         + jnp.dot(s_vt[2 * i + 1, tile], s_w[par, 2 * i + 1], preferred_element_type=F32))

    def m_scores(p, par, own):
        k = km_ref[key_rows(p), :]
        for h in heads:
            s = jnp.dot(k[:, pair_lanes(h)], m_qst[h], preferred_element_type=F32)
            if own:
                s = jnp.where(row <= col, s, -jnp.inf)
                m_new = jnp.max(s, axis=0, keepdims=True)
                shift = m_new
                m_alpha[par, h] = jnp.zeros((1, t), F32)
            else:
                chosen = m_sel[h, pl.ds(tile_of(p), 1), :] > 0.5
                m_old = m_max[h]
                m_tile = jnp.max(s, axis=0, keepdims=True)
                m_new = jnp.where(chosen, jnp.maximum(m_old, m_tile), m_old)
                shift = jnp.where(chosen, m_new, jnp.inf)
                m_alpha[par, h] = jnp.exp(m_old - m_new)
            m_p[par, h] = jnp.exp(s - shift).astype(BF16)
            m_max[h] = m_new

    def m_value(p, par):
        tile = tile_of(p)
        for h in heads:
            m_acc[h] = (m_acc[h] * m_alpha[par, h]
                        + jnp.dot(m_vt[h, tile], m_p[par, h], preferred_element_type=F32))

    def step(s, par):
        s_scores(s, par, False)
        m_scores(s - 1, 1 - par, False)
        s_weights(1 - par)
        s_value(s - 2, par)
        m_value(s - 2, par)

    s_acc[...] = jnp.zeros_like(s_acc)
    s_r[...] = jnp.zeros_like(s_r)
    m_acc[...] = jnp.zeros_like(m_acc)
    s_scores(0, 0, True)
    s_weights(0)
    m_scores(0, 0, True)
    s_scores(1, 1, False)

    n_full = jnp.maximum(n_tiles - 2, 0)

    def body(i, carry):
        s = 2 + 2 * i
        step(s, 0)
        step(s + 1, 1)
        return carry

    lax.fori_loop(0, n_full // 2, body, 0)

    @pl.when(n_full % 2 == 1)
    def _():
        step(n_tiles - 1, 0)

    def drain(par_t, first):
        if first <= 0:
            s_value(n_tiles - 2, par_t)
            m_value(n_tiles - 2, par_t)
            s_weights(1 - par_t)
            m_scores(n_tiles - 1, 1 - par_t, False)
        s_value(n_tiles - 1, 1 - par_t)
        m_value(n_tiles - 1, 1 - par_t)

    for par_t in range(2):
        @pl.when(jnp.logical_and(n_tiles >= 2, n_tiles % 2 == par_t))
        def _():
            drain(par_t, 0)

    @pl.when(n_tiles == 1)
    def _():
        drain(1, 1)

    for i in range(N_HEADS // 2):
        sl = slice(i * LANES, (i + 1) * LANES)
        os_ref[:, sl] = s_acc[i].T.astype(os_ref.dtype)
        a0 = m_acc[2 * i]
        a1 = m_acc[2 * i + 1]
        out_t = jnp.where(head0_rows, a0 / a0[ones_row[0]:ones_row[0] + 1, :],
                          a1 / a1[ones_row[1]:ones_row[1] + 1, :])
        om_ref[:, sl] = out_t.T.astype(om_ref.dtype)


def _attention(u, bsz, seq):
    t = MOBA_BLOCK
    w = BRANCH_W
    nb = seq // t
    nb_rows = -(-nb // BF16_ROWS) * BF16_ROWS
    ke = max(1, min(MOBA_TOPK, nb - 1))
    tile_f32 = pltpu.VMEM((2, N_HEADS, t, t), F32)
    tile_bf16 = pltpu.VMEM((2, N_HEADS, t, t), BF16)
    vt_tiles = pltpu.VMEM((N_HEADS, nb, LANES, t), BF16)
    resident = pl.Buffered(1)

    def q_spec(col):
        return pl.BlockSpec((t, w), lambda b, qi: (b * nb + qi, col * LANES // w))

    def kv_spec(col):
        return pl.BlockSpec((seq, w), lambda b, qi: (b, col * LANES // w), pipeline_mode=resident)

    out_spec = pl.BlockSpec((t, w), lambda b, qi: (b * nb + qi, 0))
    out_shape = jax.ShapeDtypeStruct((bsz * seq, w), BF16)
    return pl.pallas_call(
        functools.partial(_attn_kernel, ke=ke),
        grid=(bsz, nb),
        in_specs=[q_spec(COL_SB_Q), kv_spec(COL_SB_K), kv_spec(COL_SB_V),
                  q_spec(COL_MO_Q), kv_spec(COL_MO_K), kv_spec(COL_MO_V)],
        out_specs=[out_spec, out_spec],
        out_shape=[out_shape, out_shape],
        scratch_shapes=[
            vt_tiles,
            tile_bf16,
            tile_f32,
            tile_bf16,
            pltpu.VMEM((N_HEADS, 1, t), F32),
            pltpu.VMEM((N_HEADS // 2, LANES, t), F32),
            vt_tiles,
            pltpu.VMEM((nb_rows, w), F32),
            pltpu.VMEM((N_HEADS, nb_rows, t), F32),
            tile_bf16,
            pltpu.VMEM((2, N_HEADS, 1, t), F32),
            pltpu.VMEM((N_HEADS, 1, t), F32),
            pltpu.VMEM((N_HEADS, LANES, t), F32),
        ],
        compiler_params=_cparams(("parallel", "arbitrary")),
        name="attention",
    )(u, u, u, u, u, u)


def _per_head(v, lanes_per_head, width):
    head = lax.broadcasted_iota(jnp.int32, (1, width), 1) // lanes_per_head
    out = jnp.zeros((v.shape[0], width), F32)
    for h in range(N_HEADS):
        out = jnp.where(head == h, v[:, h:h + 1], out)
    return out


def _split3(a):
    hi = a.astype(BF16)
    r1 = a - hi.astype(F32)
    mid = r1.astype(BF16)
    lo = (r1 - mid.astype(F32)).astype(BF16)
    return hi, mid, lo


def _mixer_chunk(loc_ref, dt_ref, rows, first, cwa_ref, cwc_ref, cbias_ref, dtb_ref, alog_ref,
                 dskip_ref, ng_ref, ya_ref, yc_ref, bufa, bufc, hst):
    t = SSM_CHUNK
    w = BRANCH_W

    ua = loc_ref[rows, 0:3 * w].astype(F32)
    bufa[0:HIST, :] = jnp.where(first, 0.0, bufa[0:HIST, :])
    bufa[HIST:, :] = ua[:, 2 * w:] * ua[:, :w]
    conv = cwa_ref[0:1, :] * bufa[pl.ds(HIST - SC_K + 1, t), :]
    for kk in range(1, SC_K):
        conv = conv + cwa_ref[kk:kk + 1, :] * bufa[pl.ds(HIST - SC_K + 1 + kk, t), :]
    ya_ref[rows, :] = (ua[:, w:2 * w] * conv).astype(ya_ref.dtype)
    bufa[0:HIST, :] = bufa[t:t + HIST, :]
    yield

    bufc[0:HIST, :] = jnp.where(first, 0.0, bufc[0:HIST, :])
    bufc[HIST:, :] = loc_ref[rows, LOC_XBC:LOC_XBC + SSM_CONV_DIM].astype(F32)
    xc = cbias_ref[...] + cwc_ref[0:1, :] * bufc[pl.ds(HIST - SSM_CONV_K + 1, t), :]
    for kk in range(1, SSM_CONV_K):
        xc = xc + cwc_ref[kk:kk + 1, :] * bufc[pl.ds(HIST - SSM_CONV_K + 1 + kk, t), :]
    bufc[0:HIST, :] = bufc[t:t + HIST, :]
    xc = xc * _sigmoid(xc)
    xs = xc[:, :w]
    b_in = xc[:, w:w + LANES]
    c_in = xc[:, w + LANES:]
    yield

    dtp = dt_ref[rows, :] + dtb_ref[...]
    dt = jnp.maximum(dtp, 0.0) + jnp.log(1.0 + jnp.exp(-jnp.abs(dtp)))
    a = dt * (-jnp.exp(alog_ref[...]))
    row = lax.broadcasted_iota(jnp.int32, (t, t), 0)
    col = lax.broadcasted_iota(jnp.int32, (t, t), 1)
    causal = row >= col
    tri = jnp.where(causal, 1.0, 0.0).astype(BF16)
    a_hi, a_mid, a_lo = _split3(a)
    acs = (jnp.dot(tri, a_hi, preferred_element_type=F32)
           + jnp.dot(tri, a_mid, preferred_element_type=F32)
           + jnp.dot(tri, a_lo, preferred_element_type=F32))
    acs_t = acs.T
    acs_x = _per_head(acs, HEAD_DIM, w)
    last_x = acs_x[t - 1:t, :]
    x_dt = xs * _per_head(dt, HEAD_DIM, w)
    to_end_x = jnp.exp(last_x - acs_x)
    from_start_x = jnp.exp(acs_x)
    chunk_decay_x = jnp.exp(last_x)
    yield

    lane = lax.broadcasted_iota(jnp.int32, (1, LANES), 1)
    low = lane < HEAD_DIM
    c_bf = c_in.astype(BF16)
    nt = (((1,), (1,)), ((), ()))
    for g in range(SSM_GROUPS):
        gmask = low if g == 0 else jnp.logical_not(low)
        sl = slice(g * LANES, (g + 1) * LANES)
        b_g = jnp.where(gmask, b_in, 0.0)
        cb = lax.dot_general(c_bf, b_g.astype(BF16), nt, preferred_element_type=F32)
        xg = x_dt[:, sl]
        y = jnp.zeros((t, LANES), F32)
        for e in range(2):
            h = 2 * g + e
            seg = acs[:, h:h + 1] - acs_t[h:h + 1, :]
            decay = jnp.exp(jnp.where(causal, seg, -jnp.inf))
            emask = low if e == 0 else jnp.logical_not(low)
            xe = jnp.where(emask, xg, 0.0).astype(BF16)
            y = y + jnp.dot((cb * decay).astype(BF16), xe, preferred_element_type=F32)
        yield
        h_enter = jnp.where(first, 0.0, hst[g])
        y = y + jnp.dot(c_bf, h_enter.astype(BF16), preferred_element_type=F32) * from_start_x[:, sl]
        state = jnp.dot(b_g.T.astype(BF16), (xg * to_end_x[:, sl]).astype(BF16),
                        preferred_element_type=F32)
        hst[g] = h_enter * chunk_decay_x[:, sl] + state

        y = y + xs[:, sl] * dskip_ref[:, sl]
        zg = loc_ref[rows, LOC_Z + g * LANES:LOC_Z + (g + 1) * LANES].astype(F32)
        gated = y * (zg * _sigmoid(zg))
        ms = jnp.mean(gated * gated, axis=-1, keepdims=True)
        yc_ref[rows, sl] = (gated * lax.rsqrt(ms + RMS_EPS) * ng_ref[:, sl]).astype(yc_ref.dtype)
        yield


def _inproj_mixers_kernel(x_ref, g_ref, w_ref, wdt_ref, cwa_ref, cwc_ref, cbias_ref, dtb_ref, alog_ref,
                          dskip_ref, ng_ref, u_ref, ya_ref, yc_ref,
                          h_ref, loc_ref, dt_ref, loc_new, dt_new, bufa, bufc, hst, *, tn, chunks_per_seq):
    i = pl.program_id(0)
    tm = x_ref.shape[0]
    chunks_per_tile = tm // SSM_CHUNK

    @pl.when(i == 0)
    def _():
        loc_ref[...] = jnp.zeros_like(loc_ref)
        dt_ref[...] = jnp.zeros_like(dt_ref)
        bufa[...] = jnp.zeros_like(bufa)
        bufc[...] = jnp.zeros_like(bufc)
        hst[...] = jnp.zeros_like(hst)

    def mixer_phases():
        for c in range(chunks_per_tile):
            chunk = (i - 1) * chunks_per_tile + c
            first = lax.rem(chunk + chunks_per_seq, chunks_per_seq) == 0
            yield from _mixer_chunk(loc_ref, dt_ref, slice(c * SSM_CHUNK, (c + 1) * SSM_CHUNK), first,
                                    cwa_ref, cwc_ref, cbias_ref, dtb_ref, alog_ref, dskip_ref, ng_ref,
                                    ya_ref, yc_ref, bufa, bufc, hst)

    mixers = mixer_phases()
    x = x_ref[...]
    ms = jnp.mean(x * x, axis=-1, keepdims=True)
    h_ref[...] = (x * lax.rsqrt(ms + RMS_EPS) * g_ref[...]).astype(BF16)
    dt_new[...] = jnp.dot(h_ref[...], wdt_ref[...], preferred_element_type=F32)
    for c in range(N_PACK // tn):
        next(mixers, None)
        sl = slice(c * tn, (c + 1) * tn)
        acc = jnp.dot(h_ref[...], w_ref[:, sl], preferred_element_type=F32).astype(BF16)
        if c * tn < N_ATT:
            u_ref[:, sl] = acc
        else:
            loc_new[:, c * tn - N_ATT:(c + 1) * tn - N_ATT] = acc
    for _ in mixers:
        pass

    loc_ref[...] = loc_new[...]
    dt_ref[...] = dt_new[...]


def _inproj_mixers(x, g, w, wdt, cwa, cwc, cbias, dtb, alog, dskip, ng, seq, tm, tn):
    m = x.shape[0]
    n = m // tm
    assert tm % SSM_CHUNK == 0 and seq % tm == 0 and N_ATT % tn == 0 and N_LOC % tn == 0
    wbr = BRANCH_W
    resident = pl.Buffered(1)
    small = lambda shape: pl.BlockSpec(shape, lambda i: (0, 0))
    this_tile = lambda i: (jnp.minimum(i, n - 1), 0)
    prev_tile = lambda i: (jnp.maximum(i - 1, 0), 0)
    return pl.pallas_call(
        functools.partial(_inproj_mixers_kernel, tn=tn, chunks_per_seq=seq // SSM_CHUNK),
        grid=(n + 1,),
        in_specs=[
            pl.BlockSpec((tm, D_MODEL), this_tile),
            small((1, D_MODEL)),
            pl.BlockSpec((D_MODEL, N_PACK), lambda i: (0, 0), pipeline_mode=resident),
            small((D_MODEL, LANES)),
            small((SC_K, wbr)), small((SSM_CONV_K, SSM_CONV_DIM)), small((1, SSM_CONV_DIM)),
            small((1, LANES)), small((1, LANES)), small((1, wbr)), small((1, wbr)),
        ],
        out_specs=[
            pl.BlockSpec((tm, N_ATT), this_tile),
            pl.BlockSpec((tm, wbr), prev_tile),
            pl.BlockSpec((tm, wbr), prev_tile),
        ],
        out_shape=[jax.ShapeDtypeStruct((m, N_ATT), BF16),
                   jax.ShapeDtypeStruct((m, wbr), BF16),
                   jax.ShapeDtypeStruct((m, wbr), BF16)],
        scratch_shapes=[
            pltpu.VMEM((tm, D_MODEL), BF16),
            pltpu.VMEM((tm, N_LOC), BF16),
            pltpu.VMEM((tm, LANES), F32),
            pltpu.VMEM((tm, N_LOC), BF16),
            pltpu.VMEM((tm, LANES), F32),
            pltpu.VMEM((SSM_CHUNK + HIST, wbr), F32),
            pltpu.VMEM((SSM_CHUNK + HIST, SSM_CONV_DIM), F32),
            pltpu.VMEM((SSM_GROUPS, LANES, LANES), F32),
        ],
        compiler_params=_cparams(("arbitrary",)),
        name="inproj_mixers",
    )(x, g, w, wdt, cwa, cwc, cbias, dtb, alog, dskip, ng)


def _merge_kernel(x_ref, g_ref, wg_ref, ya_ref, yb_ref, yc_ref, yd_ref, wb_ref, wo_ref, o_ref, h_ref, mg_ref):
    x = x_ref[...]
    ms = jnp.mean(x * x, axis=-1, keepdims=True)
    h_ref[...] = (x * lax.rsqrt(ms + RMS_EPS) * g_ref[...]).astype(BF16)
    for c in range(D_MODEL // MXU_COLS):
        sl = slice(c * MXU_COLS, (c + 1) * MXU_COLS)
        merged = None
        for i, y_ref in enumerate((ya_ref, yb_ref, yc_ref, yd_ref)):
            pre = jnp.dot(h_ref[...], wg_ref[:, i * D_MODEL + c * MXU_COLS:i * D_MODEL + (c + 1) * MXU_COLS],
                          preferred_element_type=F32)
            term = (1.0 + jnp.tanh(0.5 * pre)) * jnp.dot(y_ref[...], wb_ref[i, :, sl], preferred_element_type=F32)
            merged = term if merged is None else merged + term
        mg_ref[:, sl] = (0.5 * merged).astype(BF16)
    o_ref[...] = x_ref[...] + jnp.dot(mg_ref[...], wo_ref[...], preferred_element_type=F32)


def _merge(x, g, wg, ya, yb, yc, yd, wb, wo, tm):
    m = x.shape[0]
    ybr = pl.BlockSpec((tm, BRANCH_W), lambda i: (i, 0))
    resident = pl.Buffered(1)
    return pl.pallas_call(
        _merge_kernel,
        grid=(m // tm,),
        in_specs=[
            pl.BlockSpec((tm, D_MODEL), lambda i: (i, 0)),
            pl.BlockSpec((1, D_MODEL), lambda i: (0, 0)),
            pl.BlockSpec((D_MODEL, N_BRANCH * D_MODEL), lambda i: (0, 0), pipeline_mode=resident),
            ybr, ybr, ybr, ybr,
            pl.BlockSpec((N_BRANCH, BRANCH_W, D_MODEL), lambda i: (0, 0, 0), pipeline_mode=resident),
            pl.BlockSpec((D_MODEL, D_MODEL), lambda i: (0, 0), pipeline_mode=resident),
        ],
        out_specs=pl.BlockSpec((tm, D_MODEL), lambda i: (i, 0)),
        out_shape=jax.ShapeDtypeStruct((m, D_MODEL), F32),
        scratch_shapes=[pltpu.VMEM((tm, D_MODEL), BF16), pltpu.VMEM((tm, D_MODEL), BF16)],
        compiler_params=_cparams(("parallel",)),
        name="merge",
    )(x, g, wg, ya, yb, yc, yd, wb, wo)


def _ffn_kernel(x_ref, g_ref, wgu_ref, wd_ref, fg_ref, o_ref, h_ref, act_ref, *, th, final_norm):
    x = x_ref[...]
    ms = jnp.mean(x * x, axis=-1, keepdims=True)
    h_ref[...] = (x * lax.rsqrt(ms + RMS_EPS) * g_ref[...]).astype(BF16)
    for c in range(FFN_HIDDEN // th):
        h = h_ref[...]
        gate = jnp.dot(h, wgu_ref[:, c * th:(c + 1) * th], preferred_element_type=F32)
        up = jnp.dot(h, wgu_ref[:, FFN_HIDDEN + c * th:FFN_HIDDEN + (c + 1) * th],
                     preferred_element_type=F32)
        act_ref[:, c * th:(c + 1) * th] = ((gate * _sigmoid(gate)) * up).astype(BF16)
    y = x_ref[...] + jnp.dot(act_ref[...], wd_ref[...], preferred_element_type=F32)
    if final_norm:
        ms = jnp.mean(y * y, axis=-1, keepdims=True)
        y = y * lax.rsqrt(ms + RMS_EPS) * fg_ref[...]
    o_ref[...] = y


def _ffn(x, g, wgu, wd, final_g, tm, th, final_norm):
    m = x.shape[0]
    resident = pl.Buffered(1)
    return pl.pallas_call(
        functools.partial(_ffn_kernel, th=th, final_norm=final_norm),
        grid=(m // tm,),
        in_specs=[
            pl.BlockSpec((tm, D_MODEL), lambda i: (i, 0)),
            pl.BlockSpec((1, D_MODEL), lambda i: (0, 0)),
            pl.BlockSpec((D_MODEL, 2 * FFN_HIDDEN), lambda i: (0, 0), pipeline_mode=resident),
            pl.BlockSpec((FFN_HIDDEN, D_MODEL), lambda i: (0, 0), pipeline_mode=resident),
            pl.BlockSpec((1, D_MODEL), lambda i: (0, 0)),
        ],
        out_specs=pl.BlockSpec((tm, D_MODEL), lambda i: (i, 0)),
        out_shape=jax.ShapeDtypeStruct((m, D_MODEL), F32),
        scratch_shapes=[pltpu.VMEM((tm, D_MODEL), BF16), pltpu.VMEM((tm, FFN_HIDDEN), BF16)],
        compiler_params=_cparams(("parallel",)),
        name="ffn",
    )(x, g, wgu, wd, final_g)


def _row_tile(m, want):
    while m % want:
        want //= 2
    return want


def _pad_lanes(v):
    return jnp.pad(v.astype(F32), (0, LANES - v.shape[0]))[None, :]


def kernel(x, norm1_g, w_in, conv_a_w, ssm_conv_w, ssm_conv_b, ssm_dt_bias, ssm_a_log, ssm_d,
           ssm_norm_g, w_branch, w_o, norm2_g, w_gate_up, w_down, final_g):
    bsz, seq, _ = x.shape
    assert seq % MOBA_BLOCK == 0 and seq % SSM_CHUNK == 0
    depth = w_in.shape[0]
    assert depth >= 1
    m = bsz * seq
    b_col = 3 * BRANCH_W
    z_col = b_col + 3 * BRANCH_W
    dt_col = z_col + BRANCH_W + SSM_CONV_DIM
    d_col = dt_col + N_HEADS
    g_col = d_col + 3 * BRANCH_W
    h = x.reshape(m, D_MODEL)
    tm = _row_tile(m, ROW_TILE)
    for l in range(depth):
        w_main = jnp.concatenate([w_in[l, :, b_col:z_col], w_in[l, :, d_col:g_col],
                                  w_in[l, :, :b_col], w_in[l, :, z_col:dt_col]], axis=1).astype(BF16)
        w_gate = w_in[l, :, g_col:].astype(BF16)
        w_dt = jnp.pad(w_in[l, :, dt_col:d_col], ((0, 0), (0, LANES - N_HEADS))).astype(BF16)
        u, y_a, y_c = _inproj_mixers(h, norm1_g[l][None, :], w_main, w_dt, conv_a_w[l], ssm_conv_w[l],
                                     ssm_conv_b[l][None, :], _pad_lanes(ssm_dt_bias[l]),
                                     _pad_lanes(ssm_a_log[l]), jnp.repeat(ssm_d[l], HEAD_DIM)[None, :],
                                     ssm_norm_g[l][None, :], seq, tm, MXU_COLS)
        y_b, y_d = _attention(u, bsz, seq)
        h = _merge(h, norm1_g[l][None, :], w_gate, y_a, y_b, y_c, y_d,
                   w_branch[l].astype(BF16), w_o[l].astype(BF16), tm)
        h = _ffn(h, norm2_g[l][None, :], w_gate_up[l].astype(BF16), w_down[l].astype(BF16),
                 final_g[None, :], tm, MXU_COLS, final_norm=(l == depth - 1))
    return h.reshape(bsz, seq, D_MODEL)
```
